```python
import jax, jax.numpy as jnp
from jax import lax
import numpy as np

D_MODEL = 1024
BATCH = 4
SEQ = 4096
DEPTH = 1
DEC_BATCH = 4
DEC_SEQ = 8192
PAST_LEN = 128

GRID_W = 64
N_MEM = 256

ATT_HEADS = 8
ATT_KV_HEADS = 2
ATT_GROUP = ATT_HEADS // ATT_KV_HEADS
ATT_HEAD_DIM = 64
ATT_WIDTH = ATT_HEADS * ATT_HEAD_DIM
ATT_KV_WIDTH = ATT_KV_HEADS * ATT_HEAD_DIM
ROPE_AXIS_DIM = ATT_HEAD_DIM // 2
ROPE_THETA = 10000.0
Q_BLOCK = 128

HG_HEADS = 4
HG_DK = 128
HG_DV = 128
HG_KEY_WIDTH = HG_HEADS * HG_DK
HG_WIDTH = HG_HEADS * HG_DV
HG_CHUNK = 64

MIX_WIDTH = ATT_WIDTH + HG_WIDTH
IN_SPLITS = (ATT_WIDTH, ATT_KV_WIDTH, ATT_KV_WIDTH, HG_KEY_WIDTH, HG_KEY_WIDTH, HG_KEY_WIDTH, HG_WIDTH, HG_WIDTH)
IN_WIDTH = ATT_WIDTH + 2 * ATT_KV_WIDTH + 3 * HG_KEY_WIDTH + 2 * HG_WIDTH

X_HEADS = 4
X_HEAD_DIM = D_MODEL // X_HEADS

N_EXPERTS = 32
TOP_K = 4
D_FF = D_MODEL
SWIGLU_LIMIT = 7.0
SWIGLU_ALPHA = 1.702
MOE_BLOCK = 128

LN_EPS = 1e-5
RMS_EPS = 1e-6
DEEPNORM_ALPHA = (2.0 * DEPTH) ** 0.25
DEEPNORM_BETA = (8.0 * DEPTH) ** -0.25

kernel_name = 'hymba_gqa_hgrn2_moe_encoder'


def layer_norm(x, g, b):
    xf = x.astype(jnp.float32)
    mu = jnp.mean(xf, axis=-1, keepdims=True)
    xc = xf - mu
    var = jnp.mean(xc * xc, axis=-1, keepdims=True)
    return (xc * lax.rsqrt(var + LN_EPS) * g.astype(jnp.float32) + b.astype(jnp.float32)).astype(x.dtype)


def rms_norm_f32(x, g):
    xf = x.astype(jnp.float32)
    return xf * lax.rsqrt(jnp.mean(xf * xf, axis=-1, keepdims=True) + RMS_EPS) * g.astype(jnp.float32)


def axial_rope_angles(seq_len):
    rows = seq_len // GRID_W
    row = jnp.repeat(jnp.arange(rows, dtype=jnp.float32), GRID_W)
    col = (jnp.arange(seq_len) % GRID_W).astype(jnp.float32)
    inv_freq = ROPE_THETA ** (-jnp.arange(0, ROPE_AXIS_DIM, 2, dtype=jnp.float32) / ROPE_AXIS_DIM)
    ang = jnp.concatenate([row[:, None] * inv_freq, col[:, None] * inv_freq], axis=-1)
    return jnp.cos(ang), jnp.sin(ang)


def apply_rope(x, cos, sin):
    xp = x.reshape(*x.shape[:-1], ATT_HEAD_DIM // 2, 2)
    x0, x1 = xp[..., 0], xp[..., 1]
    return jnp.stack([x0 * cos - x1 * sin, x0 * sin + x1 * cos], axis=-1).reshape(x.shape)


def attention_group(q, k, v, q_gain, k_gain):
    B, S, _ = q.shape
    dt = q.dtype
    q = rms_norm_f32(q.reshape(B, S, ATT_KV_HEADS, ATT_GROUP, ATT_HEAD_DIM), q_gain).transpose(0, 2, 3, 1, 4)
    k = rms_norm_f32(k.reshape(B, S, ATT_KV_HEADS, ATT_HEAD_DIM), k_gain).transpose(0, 2, 1, 3)
    v = v.reshape(B, S, ATT_KV_HEADS, ATT_HEAD_DIM).transpose(0, 2, 1, 3)
    cos, sin = axial_rope_angles(S)
    q = (apply_rope(q, cos, sin) * ATT_HEAD_DIM ** -0.5).astype(dt)
    k = apply_rope(k, cos, sin).astype(dt)
    n_blk = S // Q_BLOCK
    q_blocks = jnp.moveaxis(q.reshape(B, ATT_KV_HEADS, ATT_GROUP, n_blk, Q_BLOCK, ATT_HEAD_DIM), 3, 0)

    def one_block(qb):
        s = jnp.einsum('bkgqd,bksd->bkgqs', qb, k, preferred_element_type=jnp.float32)
        p = jax.nn.softmax(s, axis=-1).astype(dt)
        return jnp.einsum('bkgqs,bksd->bkgqd', p, v)

    o = lax.map(one_block, q_blocks)
    o = jnp.moveaxis(o, 0, 3).reshape(B, ATT_KV_HEADS, ATT_GROUP, S, ATT_HEAD_DIM)
    return o.transpose(0, 3, 1, 2, 4).reshape(B, S, ATT_WIDTH)


def hgrn_lower_bound(lb_logits, layer):
    sm = jax.nn.softmax(lb_logits.astype(jnp.float32), axis=0)
    return jnp.cumsum(sm, axis=0)[layer + 1] - sm[0]


def chunk_gla_scan(q, k, v, log_f):
    B, H, S, DK = q.shape
    DV = v.shape[-1]
    nc = S // HG_CHUNK

    def to_chunks(t):
        return jnp.moveaxis(t.reshape(B, H, nc, HG_CHUNK, t.shape[-1]), 2, 0)

    lower = jnp.tril(jnp.ones((HG_CHUNK, HG_CHUNK), dtype=bool))

    def step(state, inp):
        qc, kc, vc, lc = inp
        b = jnp.cumsum(lc, axis=-2)
        diff = b[..., :, None, :] - b[..., None, :, :]
        decay = jnp.exp(jnp.where(lower[:, :, None], diff, -jnp.inf))
        scores = jnp.einsum('bhid,bhijd->bhij', qc, decay * kc[:, :, None, :, :])
        o = jnp.einsum('bhij,bhje->bhie', scores, vc) + jnp.einsum('bhid,bhde->bhie', qc * jnp.exp(b), state)
        b_last = b[..., -1:, :]
        state = state * jnp.exp(b_last)[:, :, 0, :, None] + jnp.einsum('bhjd,bhje->bhde', kc * jnp.exp(b_last - b), vc)
        return state, o

    state0 = jnp.zeros((B, H, DK, DV), jnp.float32)
    _, o = lax.scan(step, state0, (to_chunks(q), to_chunks(k), to_chunks(v), to_chunks(log_f)))
    return jnp.moveaxis(o, 0, 2).reshape(B, H, S, DV)


def hgrn2_group(q, f_fwd, f_bwd, i, g, lb_fwd, lb_bwd, o_gain):
    B, S, _ = q.shape
    dt = q.dtype

    def heads(t, d):
        return t.reshape(B, S, HG_HEADS, d).transpose(0, 2, 1, 3).astype(jnp.float32)

    qh = jax.nn.silu(heads(q, HG_DK))
    vh = heads(i, HG_DV)

    def gates(pre, lb):
        lb = lb.reshape(HG_HEADS, 1, HG_DK)
        f = lb + (1.0 - lb) * jax.nn.sigmoid(heads(pre, HG_DK))
        return 1.0 - f, jnp.log(f)

    k_f, lf_f = gates(f_fwd, lb_fwd)
    k_b, lf_b = gates(f_bwd, lb_bwd)
    o_fwd = chunk_gla_scan(qh, k_f, vh, lf_f)
    flip = lambda t: jnp.flip(t, axis=2)
    o_bwd = flip(chunk_gla_scan(flip(qh), flip(k_b), flip(vh), flip(lf_b)))
    o = rms_norm_f32(o_fwd + o_bwd, o_gain) * jax.nn.silu(heads(g, HG_DV))
    return o.transpose(0, 2, 1, 3).reshape(B, S, HG_WIDTH).astype(dt)


def cross_attention(h, mem, w_q, w_kv, w_o):
    B, S, D = h.shape
    M = mem.shape[1]
    q = (h @ w_q).reshape(B, S, X_HEADS, X_HEAD_DIM)
    kv = (mem @ w_kv).reshape(B, M, 2, X_HEADS, X_HEAD_DIM)
    k, v = kv[:, :, 0], kv[:, :, 1]
    s = jnp.einsum('bshd,bmhd->bhsm', q, k, preferred_element_type=jnp.float32) * X_HEAD_DIM ** -0.5
    p = jax.nn.softmax(s, axis=-1).astype(h.dtype)
    o = jnp.einsum('bhsm,bmhd->bshd', p, v).reshape(B, S, D)
    return o @ w_o


def moe_ffn(x, w_router, b_router, w_up, b_up, w_down, b_down):
    B, S, D = x.shape
    n_tok = B * S
    n_asg = n_tok * TOP_K
    xt = x.reshape(n_tok, D)
    logits = jnp.dot(xt, w_router, preferred_element_type=jnp.float32) + b_router.astype(jnp.float32)
    top_val, top_idx = lax.top_k(logits, TOP_K)
    gate = jax.nn.softmax(top_val, axis=-1)
    flat_e = top_idx.reshape(-1)
    order = jnp.argsort(flat_e)
    sorted_e = flat_e[order]
    tok = order // TOP_K
    counts = jnp.bincount(flat_e, length=N_EXPERTS)
    padded = (counts + MOE_BLOCK - 1) // MOE_BLOCK * MOE_BLOCK
    pad_end = jnp.cumsum(padded)
    pad_start = pad_end - padded
    grp_start = jnp.cumsum(counts) - counts
    dest = pad_start[sorted_e] + jnp.arange(n_asg) - grp_start[sorted_e]
    n_blk = (n_asg + N_EXPERTS * (MOE_BLOCK - 1) + MOE_BLOCK - 1) // MOE_BLOCK
    buf_tok = jnp.zeros((n_blk * MOE_BLOCK,), jnp.int32).at[dest].set(tok.astype(jnp.int32))
    blk_expert = jnp.minimum(jnp.searchsorted(pad_end, jnp.arange(n_blk) * MOE_BLOCK, side='right'), N_EXPERTS - 1)
    x_blocks = xt[buf_tok].reshape(n_blk, MOE_BLOCK, D)

    def expert_block(args):
        xb, e = args
        h = jnp.dot(xb, w_up[e]) + b_up[e]
        glu = jnp.minimum(h[:, :D_FF], SWIGLU_LIMIT)
        lin = jnp.clip(h[:, D_FF:], -SWIGLU_LIMIT, SWIGLU_LIMIT)
        a = glu * jax.nn.sigmoid(SWIGLU_ALPHA * glu) * (lin + 1.0)
        return jnp.dot(a, w_down[e]) + b_down[e]

    y_blocks = lax.map(expert_block, (x_blocks, blk_expert))
    y_asg = y_blocks.reshape(n_blk * MOE_BLOCK, D)[dest].astype(jnp.float32) * gate.reshape(-1)[order][:, None]
    out = jnp.zeros((n_tok, D), jnp.float32).at[tok].add(y_asg)
    return out.reshape(B, S, D).astype(x.dtype)


def trunk(x, mem, ln_in_g, ln_in_b, w_in, q_norm_g, k_norm_g, hg_lb_fwd, hg_lb_bwd, hg_norm_g, w_out,
          ln1_g, ln1_b, w_xq, w_xkv, w_xo, ln2_g, ln2_b, w_router, b_router, w_up, b_up, w_down, b_down,
          ln3_g, ln3_b):
    x = layer_norm(x, ln_in_g, ln_in_b)
    offsets = np.cumsum(IN_SPLITS)[:-1].tolist()
    for l in range(DEPTH):
        u = x @ w_in[l]
        a_q, a_k, a_v, h_q, h_ff, h_fb, h_i, h_g = jnp.split(u, offsets, axis=-1)
        att = attention_group(a_q, a_k, a_v, q_norm_g[l], k_norm_g[l])
        rec = hgrn2_group(h_q, h_ff, h_fb, h_i, h_g, hgrn_lower_bound(hg_lb_fwd, l),
                          hgrn_lower_bound(hg_lb_bwd, l), hg_norm_g[l])
        mix = jnp.concatenate([att, rec], axis=-1) @ w_out[l]
        x = layer_norm(DEEPNORM_ALPHA * x + mix, ln1_g[l], ln1_b[l])
        x = layer_norm(DEEPNORM_ALPHA * x + cross_attention(x, mem, w_xq[l], w_xkv[l], w_xo[l]), ln2_g[l], ln2_b[l])
        ffn = moe_ffn(x, w_router[l], b_router[l], w_up[l], b_up[l], w_down[l], b_down[l])
        x = layer_norm(DEEPNORM_ALPHA * x + ffn, ln3_g[l], ln3_b[l])
    return x


def setup_inputs(seed: int = 0) -> dict:
    key = jax.random.key(seed)
    ks = jax.random.split(key, 32)
    f32 = jnp.float32
    nrm = lambda k, shape, scale: jax.random.normal(k, shape, f32) * scale
    gain = lambda k, shape: 1.0 + 0.02 * jax.random.normal(k, shape, f32)
    return {
        'x_prompt': nrm(ks[0], (BATCH, SEQ, D_MODEL), 1.0),
        'x_sample': nrm(ks[1], (DEC_BATCH, DEC_SEQ, D_MODEL), 1.0),
        'mem_prompt': nrm(ks[2], (BATCH, N_MEM, D_MODEL), 1.0),
        'mem_sample': nrm(ks[3], (DEC_BATCH, N_MEM, D_MODEL), 1.0),
        'ln_in_g': gain(ks[4], (D_MODEL,)),
        'ln_in_b': nrm(ks[5], (D_MODEL,), 0.02),
        'w_in': nrm(ks[6], (DEPTH, D_MODEL, IN_WIDTH), D_MODEL ** -0.5),
        'q_norm_g': gain(ks[7], (DEPTH, ATT_HEAD_DIM)),
        'k_norm_g': gain(ks[8], (DEPTH, ATT_HEAD_DIM)),
        'hg_lb_fwd': nrm(ks[9], (DEPTH + 1, HG_KEY_WIDTH), 0.5),
        'hg_lb_bwd': nrm(ks[10], (DEPTH + 1, HG_KEY_WIDTH), 0.5),
        'hg_norm_g': gain(ks[11], (DEPTH, HG_DV)),
        'w_out': nrm(ks[12], (DEPTH, MIX_WIDTH, D_MODEL), MIX_WIDTH ** -0.5 * DEEPNORM_BETA),
        'ln1_g': gain(ks[13], (DEPTH, D_MODEL)),
        'ln1_b': nrm(ks[14], (DEPTH, D_MODEL), 0.02),
        'w_xq': nrm(ks[15], (DEPTH, D_MODEL, D_MODEL), D_MODEL ** -0.5),
        'w_xkv': nrm(ks[16], (DEPTH, D_MODEL, 2 * D_MODEL), D_MODEL ** -0.5),
        'w_xo': nrm(ks[17], (DEPTH, D_MODEL, D_MODEL), D_MODEL ** -0.5 * DEEPNORM_BETA),
        'ln2_g': gain(ks[18], (DEPTH, D_MODEL)),
        'ln2_b': nrm(ks[19], (DEPTH, D_MODEL), 0.02),
        'w_router': nrm(ks[20], (DEPTH, D_MODEL, N_EXPERTS), D_MODEL ** -0.5),
        'b_router': nrm(ks[21], (DEPTH, N_EXPERTS), 0.01),
        'w_up': nrm(ks[22], (DEPTH, N_EXPERTS, D_MODEL, 2 * D_FF), D_MODEL ** -0.5),
        'b_up': nrm(ks[23], (DEPTH, N_EXPERTS, 2 * D_FF), 0.02),
        'w_down': nrm(ks[24], (DEPTH, N_EXPERTS, D_FF, D_MODEL), D_FF ** -0.5 * DEEPNORM_BETA),
        'b_down': nrm(ks[25], (DEPTH, N_EXPERTS, D_MODEL), 0.02),
        'ln3_g': gain(ks[26], (DEPTH, D_MODEL)),
        'ln3_b': nrm(ks[27], (DEPTH, D_MODEL), 0.02),
    }


def reference(x_prompt, x_sample, mem_prompt, mem_sample, ln_in_g, ln_in_b, w_in, q_norm_g, k_norm_g,
              hg_lb_fwd, hg_lb_bwd, hg_norm_g, w_out, ln1_g, ln1_b, w_xq, w_xkv, w_xo, ln2_g, ln2_b,
              w_router, b_router, w_up, b_up, w_down, b_down, ln3_g, ln3_b):
    weights = (ln_in_g, ln_in_b, w_in, q_norm_g, k_norm_g, hg_lb_fwd, hg_lb_bwd, hg_norm_g, w_out,
               ln1_g, ln1_b, w_xq, w_xkv, w_xo, ln2_g, ln2_b, w_router, b_router, w_up, b_up,
               w_down, b_down, ln3_g, ln3_b)
    y_prompt = trunk(x_prompt, mem_prompt, *weights)
    y_sample = trunk(x_sample, mem_sample, *weights)
    return (y_prompt, y_sample)
```

```python
import functools

import jax
import jax.numpy as jnp
import numpy as np
from jax import lax
from jax.experimental import pallas as pl
from jax.experimental.pallas import tpu as pltpu

F32 = jnp.float32
BF16 = jnp.bfloat16
I32 = jnp.int32
U32 = jnp.uint32

GRID_W = 64
ATT_HEADS = 8
ATT_KV_HEADS = 2
ATT_HEAD_DIM = 64
ATT_WIDTH = ATT_HEADS * ATT_HEAD_DIM
ATT_KV_WIDTH = ATT_KV_HEADS * ATT_HEAD_DIM
ROPE_THETA = 10000.0
HG_HEADS = 4
HG_DK = 128
HG_WIDTH = HG_HEADS * HG_DK
X_HEADS = 4
N_EXPERTS = 32
TOP_K = 4
SWIGLU_LIMIT = 7.0
SWIGLU_ALPHA = 1.702
LN_EPS = 1e-5
RMS_EPS = 1e-6
DEPTH = 1
DEEPNORM_ALPHA = (2.0 * DEPTH) ** 0.25

LANES = 128
VMEM_LIMIT_BYTES = 56 * 1024 * 1024

TOKEN_TILE = 512
ATT_Q_TILE = 256
ATT_K_TILE = 512
HG_CHUNK = 64
HG_SUB = 16
EXPERT_TILE = 512
COMBINE_TILE = 256
DISPATCH_TILE = 512
NEG_INF = float("-inf")


def _cparams(sem):
    return pltpu.CompilerParams(dimension_semantics=sem, vmem_limit_bytes=VMEM_LIMIT_BYTES)


def _layer_norm(x, g, b):
    mu = jnp.mean(x, axis=-1, keepdims=True)
    xc = x - mu
    var = jnp.mean(xc * xc, axis=-1, keepdims=True)
    return xc * lax.rsqrt(var + LN_EPS) * g + b


def _sigmoid(x):
    return 1.0 / (1.0 + jnp.exp(-x))


def _dot(a, b):
    return jnp.dot(a, b, preferred_element_type=F32)


def _dot_nt(a, b):
    return lax.dot_general(a, b, (((1,), (1,)), ((), ())), preferred_element_type=F32)


def _dot_tn(a, b):
    return lax.dot_general(a, b, (((0,), (0,)), ((), ())), preferred_element_type=F32)


def _split3(x):
    h1 = x.astype(BF16)
    r1 = x - h1.astype(F32)
    h2 = r1.astype(BF16)
    h3 = (r1 - h2.astype(F32)).astype(BF16)
    return h1, h2, h3


def _pack_bf16_pair(lo, hi):
    lo_b = pltpu.bitcast(lo.astype(BF16).astype(F32), U32)
    hi_b = pltpu.bitcast(hi.astype(BF16).astype(F32), U32)
    return (hi_b & jnp.uint32(0xFFFF0000)) | (lo_b >> 16)


def _unpack_bf16_pair(p):
    lo = pltpu.bitcast(p << 16, F32)
    hi = pltpu.bitcast(p & jnp.uint32(0xFFFF0000), F32)
    return lo, hi


def _inproj_kernel(x_ref, g_ref, b_ref, w_ref, cos_ref, sin_ref, qg_ref, kg_ref, lbf_ref, lbb_ref,
                   q_out, k_out, v_out, hq_out, hi_out, hg_out, lff_out, lfb_out):
    xn = _layer_norm(x_ref[...], g_ref[...], b_ref[...])
    xb = xn.astype(BF16)
    cos = cos_ref[...]
    sin = sin_ref[...]
    tm = xb.shape[0]

    lane = lax.broadcasted_iota(I32, (tm, LANES), 1)
    first_half = (lane & (ATT_HEAD_DIM - 1)) < (ATT_HEAD_DIM // 2)
    r = lax.broadcasted_iota(I32, (LANES, LANES), 0)
    c = lax.broadcasted_iota(I32, (LANES, LANES), 1)
    head_ones = jnp.where((r >> 6) == (c >> 6), 1.0, 0.0).astype(BF16)

    def normed_rope(u, gain, scale):
        sq = u * u
        s_hi = sq.astype(BF16)
        s_lo = (sq - s_hi.astype(F32)).astype(BF16)
        ms = (_dot(s_hi, head_ones) + _dot(s_lo, head_ones)) * (1.0 / ATT_HEAD_DIM)
        un = u * lax.rsqrt(ms + RMS_EPS) * gain
        rot = jnp.where(first_half, pltpu.roll(un, LANES - 32, 1), pltpu.roll(un, 32, 1))
        return (un * cos + rot * sin) * scale

    col = 0
    for j in range(ATT_WIDTH // LANES):
        u = _dot(xb, w_ref[:, col:col + LANES])
        q_out[:, j * LANES:(j + 1) * LANES] = normed_rope(u, qg_ref[...], ATT_HEAD_DIM ** -0.5).astype(BF16)
        col += LANES
    u = _dot(xb, w_ref[:, col:col + LANES])
    k_out[...] = normed_rope(u, kg_ref[...], 1.0).astype(BF16)
    col += LANES
    v_out[...] = _dot(xb, w_ref[:, col:col + LANES]).astype(BF16)
    col += LANES

    u = _dot(xb, w_ref[:, col:col + HG_WIDTH])
    hq_out[...] = (u * _sigmoid(u)).astype(BF16)
    col += HG_WIDTH
    for lb_ref, lf_out in ((lbf_ref, lff_out), (lbb_ref, lfb_out)):
        u = _dot(xb, w_ref[:, col:col + HG_WIDTH])
        lb = lb_ref[...]
        lf_out[...] = jnp.log(lb + (1.0 - lb) * _sigmoid(u))
        col += HG_WIDTH
    hi_out[...] = _dot(xb, w_ref[:, col:col + HG_WIDTH]).astype(BF16)
    col += HG_WIDTH
    u = _dot(xb, w_ref[:, col:col + HG_WIDTH])
    hg_out[...] = (u * _sigmoid(u)).astype(BF16)


def _inproj(x, g, b, w, cos_t, sin_t, qg, kg, lbf, lbb, pos_block):
    T, D = x.shape
    tm = TOKEN_TILE
    n_in = w.shape[1]
    row = lambda i: (i, 0)
    const = lambda i: (0, 0)
    out_shape = (
        jax.ShapeDtypeStruct((T, ATT_WIDTH), BF16),
        jax.ShapeDtypeStruct((T, ATT_KV_WIDTH), BF16),
        jax.ShapeDtypeStruct((T, ATT_KV_WIDTH), BF16),
        jax.ShapeDtypeStruct((T, HG_WIDTH), BF16),
        jax.ShapeDtypeStruct((T, HG_WIDTH), BF16),
        jax.ShapeDtypeStruct((T, HG_WIDTH), BF16),
        jax.ShapeDtypeStruct((T, HG_WIDTH), F32),
        jax.ShapeDtypeStruct((T, HG_WIDTH), F32),
    )
    out_specs = (
        pl.BlockSpec((tm, ATT_WIDTH), row),
        pl.BlockSpec((tm, ATT_KV_WIDTH), row),
        pl.BlockSpec((tm, ATT_KV_WIDTH), row),
        pl.BlockSpec((tm, HG_WIDTH), row),
        pl.BlockSpec((tm, HG_WIDTH), row),
        pl.BlockSpec((tm, HG_WIDTH), row),
        pl.BlockSpec((tm, HG_WIDTH), row),
        pl.BlockSpec((tm, HG_WIDTH), row),
    )
    return pl.pallas_call(
        _inproj_kernel,
        out_shape=out_shape,
        grid=(T // tm,),
        in_specs=[
            pl.BlockSpec((tm, D), row),
            pl.BlockSpec((1, D), const),
            pl.BlockSpec((1, D), const),
            pl.BlockSpec((D, n_in), const),
            pl.BlockSpec((tm, LANES), lambda i: (pos_block(i), 0)),
            pl.BlockSpec((tm, LANES), lambda i: (pos_block(i), 0)),
            pl.BlockSpec((1, LANES), const),
            pl.BlockSpec((1, LANES), const),
            pl.BlockSpec((1, HG_WIDTH), const),
            pl.BlockSpec((1, HG_WIDTH), const),
        ],
        out_specs=out_specs,
        compiler_params=_cparams(("parallel",)),
        name="inproj",
    )(x, g, b, w, cos_t, sin_t, qg, kg, lbf, lbb)


class _Groups:
    def __init__(self, n0, s0, n1, s1):
        self.n0, self.s0, self.n1, self.s1 = n0, s0, n1, s1
        self.t0 = n0 * s0
        self.total = self.t0 + n1 * s1

    def split(self, i, tile):
        nt0 = self.t0 // tile
        per0, per1 = self.s0 // tile, self.s1 // tile
        in1 = i >= nt0
        j1 = jnp.maximum(i - nt0, 0)
        seq = jnp.where(in1, self.n0 + j1 // per1, i // per0)
        pos = jnp.where(in1, j1 % per1, i % per0)
        return in1, seq, pos, jnp.where(in1, per1, per0)

    def seq_of_tile(self, i, tile):
        return self.split(i, tile)[1]

    def pos_of_tile(self, i, tile):
        return self.split(i, tile)[2]

    def reversed_tile(self, i, tile):
        _, _, pos, per = self.split(i, tile)
        return i - pos + (per - 1 - pos)


def _attention_kernel(*refs, n_kv_blocks, tiles_group0):
    q_ref = refs[0]
    k_refs = refs[1:1 + n_kv_blocks]
    v_refs = refs[1 + n_kv_blocks:1 + 2 * n_kv_blocks]
    o_ref = refs[1 + 2 * n_kv_blocks]
    tq = q_ref.shape[0]
    kv_rows = k_refs[0].shape[0]
    tk = min(ATT_K_TILE, kv_rows)
    n_k = kv_rows // tk
    group = ATT_HEADS // ATT_KV_HEADS
    rows = group * tq
    lane = lax.broadcasted_iota(I32, (tq, LANES), 1)
    low_half = lane < ATT_HEAD_DIM
    extra_trips = jnp.where(pl.program_id(0) >= tiles_group0, n_k, 0)

    for kvh in range(ATT_KV_HEADS):
        keep = low_half if kvh == 0 else jnp.logical_not(low_half)
        parts = []
        for g in range(group):
            c0 = kvh * 2 * LANES + (g // 2) * LANES
            pair = q_ref[:, c0:c0 + LANES].astype(F32)
            if g % 2 != kvh:
                pair = pltpu.roll(pair, ATT_HEAD_DIM, 1)
            parts.append(jnp.where(keep, pair, 0.0).astype(BF16))
        qp = jnp.concatenate(parts, axis=0)

        def make_body(k_ref, v_ref):
            def body(kc, carry):
                m, l, acc = carry
                r0 = pl.multiple_of(kc * tk, tk)
                ks = k_ref[pl.ds(r0, tk), :]
                vs = v_ref[pl.ds(r0, tk), :]
                s = _dot_nt(qp, ks)
                m_new = jnp.maximum(m, jnp.max(s, axis=-1, keepdims=True))
                alpha = jnp.exp(m - m_new)
                p = jnp.exp(s - m_new)
                l_new = alpha * l + jnp.sum(p, axis=-1, keepdims=True)
                acc_new = alpha * acc + _dot(p.astype(BF16), vs)
                return m_new, l_new, acc_new
            return body

        carry = (jnp.full((rows, 1), NEG_INF, F32), jnp.zeros((rows, 1), F32), jnp.zeros((rows, LANES), F32))
        carry = lax.fori_loop(0, n_k, make_body(k_refs[0], v_refs[0]), carry)
        for kb in range(1, n_kv_blocks):
            carry = lax.fori_loop(0, extra_trips, make_body(k_refs[kb], v_refs[kb]), carry)
        m, l, acc = carry
        out = acc / l
        for j in range(group // 2):
            a = out[(2 * j) * tq:(2 * j + 1) * tq]
            b = out[(2 * j + 1) * tq:(2 * j + 2) * tq]
            if kvh == 1:
                a = pltpu.roll(a, ATT_HEAD_DIM, 1)
            else:
                b = pltpu.roll(b, ATT_HEAD_DIM, 1)
            c0 = kvh * 2 * LANES + j * LANES
            o_ref[:, c0:c0 + LANES] = jnp.where(low_half, a, b).astype(BF16)


def _attention(q, k, v, groups):
    T = q.shape[0]
    tq = ATT_Q_TILE
    kv_rows = groups.s0
    n_kv_blocks = groups.s1 // groups.s0

    def kv_map(j):
        def index_map(i):
            in1, seq, _, _ = groups.split(i, tq)
            first = jnp.where(in1, groups.t0 // kv_rows + (seq - groups.n0) * n_kv_blocks, seq)
            return (first + jnp.where(in1, j, 0), 0)
        return index_map

    kv_specs = [pl.BlockSpec((kv_rows, ATT_KV_WIDTH), kv_map(j)) for j in range(n_kv_blocks)]
    return pl.pallas_call(
        functools.partial(_attention_kernel, n_kv_blocks=n_kv_blocks, tiles_group0=groups.t0 // tq),
        out_shape=jax.ShapeDtypeStruct((T, ATT_WIDTH), BF16),
        grid=(T // tq,),
        in_specs=[pl.BlockSpec((tq, ATT_WIDTH), lambda i: (i, 0))] + kv_specs + kv_specs,
        out_specs=pl.BlockSpec((tq, ATT_WIDTH), lambda i: (i, 0)),
        compiler_params=_cparams(("parallel",)),
        name="attention",
    )(q, *([k] * n_kv_blocks), *([v] * n_kv_blocks))


def _hgrn_kernel(*refs, reverse, finalize, groups):
    if finalize:
        q_ref, v_ref, lf_ref, of_ref, g_ref, gain_ref, o_ref, st_ref = refs
    else:
        q_ref, v_ref, lf_ref, o_ref, st_ref = refs
    tb = q_ref.shape[0]
    C = min(HG_CHUNK, tb)
    n_chunk = tb // C
    n_sub = C // HG_SUB

    @pl.when(groups.pos_of_tile(pl.program_id(1), tb) == 0)
    def _():
        st_ref[...] = jnp.zeros_like(st_ref)

    ri = lax.broadcasted_iota(I32, (C, C), 0)
    ci = lax.broadcasted_iota(I32, (C, C), 1)
    sub_lo = (ri >> 4) << 4
    if reverse:
        inc = ci >= ri
        loc_m = inc & (ci < sub_lo + HG_SUB)
    else:
        inc = ci <= ri
        loc_m = inc & (ci >= sub_lo)
    cum_mat = jnp.where(inc, 1.0, 0.0).astype(BF16)
    loc_mat = jnp.where(loc_m, 1.0, 0.0).astype(BF16)
    row_id = lax.broadcasted_iota(I32, (C, 1), 0)
    sub_row = lax.broadcasted_iota(I32, (HG_SUB, 1), 0)
    lane_c = lax.broadcasted_iota(I32, (HG_SUB, C), 1)
    edge = 0 if reverse else C - 1

    def chunk(step, carry):
        cidx = (n_chunk - 1 - step) if reverse else step
        r0 = pl.multiple_of(cidx * C, C)
        lf = lf_ref[pl.ds(r0, C), :]
        q = q_ref[pl.ds(r0, C), :].astype(F32)
        v = v_ref[pl.ds(r0, C), :]
        h1, h2, h3 = _split3(lf)
        cum = _dot(cum_mat, h1) + _dot(cum_mat, h2) + _dot(cum_mat, h3)
        loc = _dot(loc_mat, h1) + _dot(loc_mat, h2) + _dot(loc_mat, h3)
        kk = 1.0 - jnp.exp(lf)
        total = cum[edge:edge + 1, :]
        st = st_ref[...]
        o_inter = _dot_nt((q * jnp.exp(cum)).astype(BF16), st.astype(BF16))
        k_state = (kk * jnp.exp(total - cum)).astype(BF16)
        st_ref[...] = st * jnp.exp(total) + _dot_tn(v, k_state)
        qh_b = (q * jnp.exp(loc)).astype(BF16)
        blocks = []
        for s in range(n_sub):
            lo = s * HG_SUB
            hi = lo + HG_SUB
            has_off = (s < n_sub - 1) if reverse else (s > 0)
            if has_off:
                if reverse:
                    expo = jnp.where(row_id >= hi, cum[hi:hi + 1, :] - cum, NEG_INF)
                else:
                    expo = jnp.where(row_id < lo, cum[lo - 1:lo, :] - cum, NEG_INF)
                k_off = (kk * jnp.exp(expo)).astype(BF16)
                a_s = _dot_nt(qh_b[lo:hi], k_off)
            else:
                a_s = jnp.zeros((HG_SUB, C), F32)
            loc_s = loc[lo:hi]
            q_s = q[lo:hi]
            for j in range(HG_SUB):
                jj = lo + j
                d = loc_s - loc[jj:jj + 1, :]
                msk = (sub_row <= j) if reverse else (sub_row >= j)
                e = jnp.exp(jnp.where(msk, d, NEG_INF))
                colv = jnp.sum(q_s * e * kk[jj:jj + 1, :], axis=-1, keepdims=True)
                a_s = jnp.where(lane_c == jj, colv, a_s)
            blocks.append(a_s)
        a = jnp.concatenate(blocks, axis=0)
        o = o_inter + _dot(a.astype(BF16), v)
        if finalize:
            tot = o + of_ref[pl.ds(r0, C), :]
            ms = jnp.mean(tot * tot, axis=-1, keepdims=True)
            res = tot * lax.rsqrt(ms + RMS_EPS) * gain_ref[...] * g_ref[pl.ds(r0, C), :].astype(F32)
            o_ref[pl.ds(r0, C), :] = res.astype(o_ref.dtype)
        else:
            o_ref[pl.ds(r0, C), :] = o
        return carry

    lax.fori_loop(0, n_chunk, chunk, 0)


def _hgrn_pass(q, v, lf, groups, *, reverse, finalize_args=None):
    T = q.shape[0]
    tb = TOKEN_TILE
    finalize = finalize_args is not None

    def tmap(h, i):
        return (groups.reversed_tile(i, tb) if reverse else i, h)

    spec = pl.BlockSpec((tb, HG_DK), tmap)
    in_specs = [spec, spec, spec]
    args = [q, v, lf]
    if finalize:
        o_fwd, gate, gain = finalize_args
        in_specs += [spec, spec, pl.BlockSpec((1, HG_DK), lambda h, i: (0, 0))]
        args += [o_fwd, gate, gain]
    return pl.pallas_call(
        functools.partial(_hgrn_kernel, reverse=reverse, finalize=finalize, groups=groups),
        out_shape=jax.ShapeDtypeStruct((T, HG_WIDTH), BF16 if finalize else F32),
        grid=(HG_HEADS, T // tb),
        in_specs=in_specs,
        out_specs=spec,
        scratch_shapes=[pltpu.VMEM((HG_DK, HG_DK), F32)],
        compiler_params=_cparams(("parallel", "arbitrary")),
        name="hgrn_bwd" if reverse else "hgrn_fwd",
    )(*args)


def _kvproj_kernel(m_ref, w_ref, o_ref):
    o_ref[...] = _dot(m_ref[...].astype(BF16), w_ref[...]).astype(BF16)


def _kvproj(mem, w):
    R, D = mem.shape
    n = w.shape[1]
    tr = 256
    return pl.pallas_call(
        _kvproj_kernel,
        out_shape=jax.ShapeDtypeStruct((R, n), BF16),
        grid=(R // tr,),
        in_specs=[pl.BlockSpec((tr, D), lambda i: (i, 0)), pl.BlockSpec((D, n), lambda i: (0, 0))],
        out_specs=pl.BlockSpec((tr, n), lambda i: (i, 0)),
        compiler_params=_cparams(("parallel",)),
        name="kvproj",
    )(mem, w)


def _mix_xattn_kernel(x_ref, att_ref, rec_ref, g0_ref, b0_ref, wo_ref, g1_ref, b1_ref, wq_ref, kv_ref, wxo_ref,
                      g2_ref, b2_ref, x2_ref, p_ref):
    D = x_ref.shape[1]
    dh = D // X_HEADS
    xn = _layer_norm(x_ref[...], g0_ref[...], b0_ref[...])
    mix = _dot(att_ref[...], wo_ref[:ATT_WIDTH, :]) + _dot(rec_ref[...], wo_ref[ATT_WIDTH:, :])
    x1 = _layer_norm(DEEPNORM_ALPHA * xn + mix, g1_ref[...], b1_ref[...])
    qx = (_dot(x1.astype(BF16), wq_ref[...]) * (dh ** -0.5)).astype(BF16)
    heads = []
    for h in range(X_HEADS):
        kh = kv_ref[0, :, h * dh:(h + 1) * dh]
        vh = kv_ref[0, :, D + h * dh:D + (h + 1) * dh]
        s = _dot_nt(qx[:, h * dh:(h + 1) * dh], kh)
        s = s - jnp.max(s, axis=-1, keepdims=True)
        p = jnp.exp(s)
        p = p / jnp.sum(p, axis=-1, keepdims=True)
        heads.append(_dot(p.astype(BF16), vh).astype(BF16))
    o = jnp.concatenate(heads, axis=-1)
    x2 = _layer_norm(DEEPNORM_ALPHA * x1 + _dot(o, wxo_ref[...]), g2_ref[...], b2_ref[...])
    x2_ref[...] = x2
    p_ref[...] = _pack_bf16_pair(x2[:, :D // 2], x2[:, D // 2:])


def _mix_xattn(x, att, rec, g0, b0, wo, g1, b1, wq, kv, wxo, g2, b2, seq_of_tile):
    T, D = x.shape
    tm = TOKEN_TILE
    n_mem = kv.shape[1]
    row = lambda i: (i, 0)
    const = lambda i: (0, 0)
    vec = pl.BlockSpec((1, D), const)
    return pl.pallas_call(
        _mix_xattn_kernel,
        out_shape=(jax.ShapeDtypeStruct((T, D), F32), jax.ShapeDtypeStruct((T, D // 2), U32)),
        grid=(T // tm,),
        in_specs=[
            pl.BlockSpec((tm, D), row),
            pl.BlockSpec((tm, ATT_WIDTH), row),
            pl.BlockSpec((tm, HG_WIDTH), row),
            vec, vec,
            pl.BlockSpec((ATT_WIDTH + HG_WIDTH, D), const),
            vec, vec,
            pl.BlockSpec((D, D), const),
            pl.BlockSpec((1, n_mem, 2 * D), lambda i: (seq_of_tile(i), 0, 0)),
            pl.BlockSpec((D, D), const),
            vec, vec,
        ],
        out_specs=(pl.BlockSpec((tm, D), row), pl.BlockSpec((tm, D // 2), row)),
        compiler_params=_cparams(("parallel",)),
        name="mix_xattn",
    )(x, att, rec, g0, b0, wo, g1, b1, wq, kv, wxo, g2, b2)


def _router_kernel(x_ref, wt_ref, b_ref, idx_ref, gate_ref, rank_ref, cnt_ref, run_ref):
    tm = x_ref.shape[0]

    @pl.when(pl.program_id(0) == 0)
    def _():
        run_ref[...] = jnp.zeros_like(run_ref)

    xh, xl, _ = _split3(x_ref[...])
    wh, wl, _ = _split3(wt_ref[...])
    logits = _dot_nt(wh, xh) + _dot_nt(wh, xl) + _dot_nt(wl, xh) + b_ref[...]
    e_id = lax.broadcasted_iota(I32, (N_EXPERTS, tm), 0)
    work = logits
    vals, idxs = [], []
    chosen = jnp.zeros((N_EXPERTS, tm), F32)
    for _ in range(TOP_K):
        m = jnp.max(work, axis=0, keepdims=True)
        idx = jnp.min(jnp.where(work == m, e_id, N_EXPERTS), axis=0, keepdims=True)
        hit = e_id == idx
        chosen = jnp.where(hit, 1.0, chosen)
        work = jnp.where(hit, NEG_INF, work)
        vals.append(m)
        idxs.append(idx)
    ex = [jnp.exp(v - vals[0]) for v in vals]
    den = ex[0] + ex[1] + ex[2] + ex[3]
    t_r = lax.broadcasted_iota(I32, (tm, tm), 0)
    t_c = lax.broadcasted_iota(I32, (tm, tm), 1)
    before = jnp.where(t_r < t_c, 1.0, 0.0).astype(BF16)
    rank_e = _dot(chosen.astype(BF16), before) + run_ref[...]
    run_ref[...] = run_ref[...] + jnp.sum(chosen, axis=1, keepdims=True)
    for k in range(TOP_K):
        idx_ref[k:k + 1, :] = idxs[k]
        gate_ref[k:k + 1, :] = ex[k] / den
        rank_ref[k:k + 1, :] = jnp.sum(jnp.where(e_id == idxs[k], rank_e, 0.0), axis=0, keepdims=True).astype(I32)
    cnt_ref[...] = run_ref[...].astype(I32)


def _router(x2, w_router_t, b_router):
    T, D = x2.shape
    tm = TOKEN_TILE
    tok = pl.BlockSpec((TOP_K, tm), lambda i: (0, i))
    return pl.pallas_call(
        _router_kernel,
        out_shape=(jax.ShapeDtypeStruct((TOP_K, T), I32), jax.ShapeDtypeStruct((TOP_K, T), F32),
                   jax.ShapeDtypeStruct((TOP_K, T), I32), jax.ShapeDtypeStruct((N_EXPERTS, 1), I32)),
        grid=(T // tm,),
        in_specs=[pl.BlockSpec((tm, D), lambda i: (i, 0)),
                  pl.BlockSpec((N_EXPERTS, D), lambda i: (0, 0)),
                  pl.BlockSpec((N_EXPERTS, 1), lambda i: (0, 0))],
        out_specs=(tok, tok, tok, pl.BlockSpec((N_EXPERTS, 1), lambda i: (0, 0))),
        scratch_shapes=[pltpu.VMEM((N_EXPERTS, 1), F32)],
        compiler_params=_cparams(("arbitrary",)),
        name="router",
    )(x2, w_router_t, b_router)


def _dispatch_kernel(dest_ref, p_hbm, zero_hbm, out_hbm, sem):
    del zero_hbm
    td = DISPATCH_TILE
    base = pl.program_id(0) * td

    def row_copy(t, k):
        d = dest_ref[0, 0, k * td + t]
        return pltpu.make_async_copy(p_hbm.at[pl.ds(base + t, 1)], out_hbm.at[pl.ds(d, 1)], sem)

    def start(t, c):
        for k in range(TOP_K):
            row_copy(t, k).start()
        return c

    def wait(t, c):
        for k in range(TOP_K):
            row_copy(t, k).wait()
        return c

    lax.fori_loop(0, td, start, 0)
    lax.fori_loop(0, td, wait, 0)


def _dispatch(dest_tiles, p, zeros):
    T = p.shape[0]
    td = DISPATCH_TILE
    return pl.pallas_call(
        _dispatch_kernel,
        out_shape=jax.ShapeDtypeStruct(zeros.shape, zeros.dtype),
        grid=(T // td,),
        in_specs=[pl.BlockSpec((1, 1, TOP_K * td), lambda i: (i, 0, 0), memory_space=pltpu.SMEM),
                  pl.BlockSpec(memory_space=pl.ANY),
                  pl.BlockSpec(memory_space=pl.ANY)],
        out_specs=pl.BlockSpec(memory_space=pl.ANY),
        scratch_shapes=[pltpu.SemaphoreType.DMA],
        input_output_aliases={2: 0},
        compiler_params=_cparams(("arbitrary",)),
        name="dispatch",
    )(dest_tiles, p, zeros)


def _expert_kernel(blk_e_ref, nvalid_ref, p_ref, wu_ref, bu_ref, wd_ref, bd_ref, y_ref):
    i = pl.program_id(0)
    dff = wd_ref.shape[1]
    half = p_ref.shape[1]

    @pl.when(i < nvalid_ref[0])
    def _():
        lo, hi = _unpack_bf16_pair(p_ref[...])
        lo = lo.astype(BF16)
        hi = hi.astype(BF16)

        def up(c0):
            return (_dot(lo, wu_ref[0, :half, c0:c0 + dff]) + _dot(hi, wu_ref[0, half:, c0:c0 + dff])
                    + bu_ref[0, :, c0:c0 + dff])

        glu = jnp.minimum(up(0), SWIGLU_LIMIT)
        lin = jnp.clip(up(dff), -SWIGLU_LIMIT, SWIGLU_LIMIT)
        a = glu * _sigmoid(SWIGLU_ALPHA * glu) * (lin + 1.0)
        y = _dot(a.astype(BF16), wd_ref[0]) + bd_ref[0]
        y_ref[...] = _pack_bf16_pair(y[:, :half], y[:, half:])

    @pl.when(i >= nvalid_ref[0])
    def _():
        y_ref[...] = jnp.zeros_like(y_ref)


def _experts(blk_expert, n_valid, ps, wu, bu, wd, bd):
    R, half = ps.shape
    te = EXPERT_TILE
    E, D, two_f = wu.shape
    dff = wd.shape[1]
    grid_spec = pltpu.PrefetchScalarGridSpec(
        num_scalar_prefetch=2,
        grid=(R // te,),
        in_specs=[
            pl.BlockSpec((te, half), lambda i, be, nv: (i, 0)),
            pl.BlockSpec((1, D, two_f), lambda i, be, nv: (be[i], 0, 0)),
            pl.BlockSpec((1, 1, two_f), lambda i, be, nv: (be[i], 0, 0)),
            pl.BlockSpec((1, dff, D), lambda i, be, nv: (be[i], 0, 0)),
            pl.BlockSpec((1, 1, D), lambda i, be, nv: (be[i], 0, 0)),
        ],
        out_specs=pl.BlockSpec((te, half), lambda i, be, nv: (i, 0)),
    )
    return pl.pallas_call(
        _expert_kernel,
        out_shape=jax.ShapeDtypeStruct((R, half), U32),
        grid_spec=grid_spec,
        compiler_params=_cparams(("arbitrary",)),
        name="experts",
    )(blk_expert, n_valid, ps, wu, bu, wd, bd)


def _combine_kernel(dest_ref, ys_hbm, x2_ref, gate_ref, g_ref, b_ref, o_ref, buf, sem):
    tc = x2_ref.shape[0]
    half = buf.shape[2]

    def row_copy(t, k):
        d = dest_ref[0, 0, k * tc + t]
        return pltpu.make_async_copy(ys_hbm.at[pl.ds(d, 1)], buf.at[k, pl.ds(t, 1)], sem)

    def start(t, c):
        for k in range(TOP_K):
            row_copy(t, k).start()
        return c

    def wait(t, c):
        for k in range(TOP_K):
            row_copy(t, k).wait()
        return c

    lax.fori_loop(0, tc, start, 0)
    lax.fori_loop(0, tc, wait, 0)

    gates = gate_ref[...]
    acc_lo = jnp.zeros((tc, half), F32)
    acc_hi = jnp.zeros((tc, half), F32)
    for k in range(TOP_K):
        lo, hi = _unpack_bf16_pair(buf[k])
        gk = gates[:, k:k + 1]
        acc_lo = acc_lo + gk * lo
        acc_hi = acc_hi + gk * hi
    ffn = jnp.concatenate([acc_lo, acc_hi], axis=-1)
    o_ref[...] = _layer_norm(DEEPNORM_ALPHA * x2_ref[...] + ffn, g_ref[...], b_ref[...])


def _combine(dest_tiles, ys, x2, gates, g, b, *, tile0, n_tiles):
    D = x2.shape[1]
    tc = COMBINE_TILE
    return pl.pallas_call(
        _combine_kernel,
        out_shape=jax.ShapeDtypeStruct((n_tiles * tc, D), F32),
        grid=(n_tiles,),
        in_specs=[pl.BlockSpec((1, 1, TOP_K * tc), lambda i: (tile0 + i, 0, 0), memory_space=pltpu.SMEM),
                  pl.BlockSpec(memory_space=pl.ANY),
                  pl.BlockSpec((tc, D), lambda i: (tile0 + i, 0)),
                  pl.BlockSpec((tc, TOP_K), lambda i: (tile0 + i, 0)),
                  pl.BlockSpec((1, D), lambda i: (0, 0)),
                  pl.BlockSpec((1, D), lambda i: (0, 0))],
        out_specs=pl.BlockSpec((tc, D), lambda i: (i, 0)),
        scratch_shapes=[pltpu.VMEM((TOP_K, tc, D // 2), U32), pltpu.SemaphoreType.DMA],
        compiler_params=_cparams(("arbitrary",)),
        name="combine",
    )(dest_tiles, ys, x2, gates, g, b)


def _rope_tables(seq_len):
    rows = seq_len // GRID_W
    row = jnp.repeat(jnp.arange(rows, dtype=F32), GRID_W)
    colp = (jnp.arange(seq_len) % GRID_W).astype(F32)
    axis_dim = ATT_HEAD_DIM // 2
    inv_freq = ROPE_THETA ** (-jnp.arange(0, axis_dim, 2, dtype=F32) / axis_dim)
    ang = jnp.concatenate([row[:, None] * inv_freq, colp[:, None] * inv_freq], axis=-1)
    cos, sin = jnp.cos(ang), jnp.sin(ang)
    cos_h = jnp.concatenate([cos, cos], axis=-1)
    sin_h = jnp.concatenate([-sin, sin], axis=-1)
    return jnp.tile(cos_h, (1, LANES // ATT_HEAD_DIM)), jnp.tile(sin_h, (1, LANES // ATT_HEAD_DIM))


def _tiles_of(idx_t, tile):
    K, T = idx_t.shape
    return idx_t.reshape(K, T // tile, tile).transpose(1, 0, 2).reshape(T // tile, 1, K * tile)


def kernel(x_prompt, x_sample, mem_prompt, mem_sample, ln_in_g, ln_in_b, w_in, q_norm_g, k_norm_g, hg_lb_fwd, hg_lb_bwd, hg_norm_g, w_out, ln1_g, ln1_b, w_xq, w_xkv, w_xo, ln2_g, ln2_b, w_router, b_router, w_up, b_up, w_down, b_down, ln3_g, ln3_b):
    Bp, Sp, D = x_prompt.shape
    Bs, Ss, _ = x_sample.shape
    Tp, Ts = Bp * Sp, Bs * Ss
    T = Tp + Ts
    tm = TOKEN_TILE
    assert Sp % tm == 0 and Ss % Sp == 0
    n_mem = mem_prompt.shape[1]
    layer = 0

    x = jnp.concatenate([x_prompt.reshape(Tp, D), x_sample.reshape(Ts, D)], axis=0)
    mem = jnp.concatenate([mem_prompt.reshape(Bp * n_mem, D), mem_sample.reshape(Bs * n_mem, D)], axis=0)

    perm = np.concatenate([np.arange(0, ATT_HEAD_DIM, 2), np.arange(1, ATT_HEAD_DIM, 2)])
    qk_cols = np.concatenate([h * ATT_HEAD_DIM + perm for h in range(ATT_HEADS + ATT_KV_HEADS)])
    cols = np.concatenate([qk_cols, np.arange(ATT_WIDTH + ATT_KV_WIDTH, w_in.shape[2])])
    w_in_b = w_in[layer][:, cols].astype(BF16)
    pair = LANES // ATT_HEAD_DIM
    qg = jnp.tile(q_norm_g[layer][perm], pair).reshape(1, LANES)
    kg = jnp.tile(k_norm_g[layer][perm], pair).reshape(1, LANES)

    def lower_bound(logits):
        sm = jax.nn.softmax(logits.astype(F32), axis=0)
        return (jnp.cumsum(sm, axis=0)[layer + 1] - sm[0]).reshape(1, HG_WIDTH)

    lbf, lbb = lower_bound(hg_lb_fwd), lower_bound(hg_lb_bwd)
    cos_t, sin_t = _rope_tables(max(Sp, Ss))
    vec = lambda a: a.reshape(1, -1)

    groups = _Groups(Bp, Sp, Bs, Ss)
    pos_block = lambda i: groups.pos_of_tile(i, tm)
    seq_of_tile = lambda i: groups.seq_of_tile(i, tm)

    q, k, v, hq, hi, hg, lff, lfb = _inproj(x, vec(ln_in_g), vec(ln_in_b), w_in_b, cos_t, sin_t, qg, kg, lbf, lbb,
                                           pos_block)

    att = _attention(q, k, v, groups)
    o_fwd = _hgrn_pass(hq, hi, lff, groups, reverse=False)
    gain = hg_norm_g[layer].reshape(1, HG_DK)
    rec = _hgrn_pass(hq, hi, lfb, groups, reverse=True, finalize_args=(o_fwd, hg, gain))

    kv = _kvproj(mem, w_xkv[layer].astype(BF16)).reshape(Bp + Bs, n_mem, 2 * D)
    x2, packed = _mix_xattn(x, att, rec, vec(ln_in_g), vec(ln_in_b), w_out[layer].astype(BF16),
                            vec(ln1_g[layer]), vec(ln1_b[layer]), w_xq[layer].astype(BF16), kv,
                            w_xo[layer].astype(BF16), vec(ln2_g[layer]), vec(ln2_b[layer]), seq_of_tile)

    idx_t, gate_t, rank_t, counts = _router(x2, w_router[layer].T, b_router[layer].reshape(N_EXPERTS, 1))

    te = EXPERT_TILE
    counts = counts.reshape(N_EXPERTS)
    padded = (counts + te - 1) // te * te
    pad_end = jnp.cumsum(padded)
    pad_start = pad_end - padded
    n_blk = (T * TOP_K + N_EXPERTS * (te - 1) + te - 1) // te
    dest_t = pad_start[idx_t] + rank_t
    blk_expert = jnp.minimum(jnp.searchsorted(pad_end, jnp.arange(n_blk, dtype=I32) * te, side='right'),
                             N_EXPERTS - 1).astype(I32)
    n_valid = (pad_end[-1] // te).astype(I32).reshape(1)

    sorted_rows = _dispatch(_tiles_of(dest_t, DISPATCH_TILE), packed, jnp.zeros((n_blk * te, D // 2), U32))
    ys = _experts(blk_expert, n_valid, sorted_rows, w_up[layer].astype(BF16),
                  b_up[layer].reshape(N_EXPERTS, 1, -1), w_down[layer].astype(BF16),
                  b_down[layer].reshape(N_EXPERTS, 1, -1))

    dest_c = _tiles_of(dest_t, COMBINE_TILE)
    gates = gate_t.T
    tc = COMBINE_TILE
    y_p = _combine(dest_c, ys, x2, gates, vec(ln3_g[layer]), vec(ln3_b[layer]), tile0=0, n_tiles=Tp // tc)
    y_s = _combine(dest_c, ys, x2, gates, vec(ln3_g[layer]), vec(ln3_b[layer]), tile0=Tp // tc, n_tiles=Ts // tc)
    return y_p.reshape(Bp, Sp, D), y_s.reshape(Bs, Ss, D)
```

```python
import functools

import jax
import jax.numpy as jnp
import numpy as np
from jax import lax
from jax.experimental import pallas as pl
from jax.experimental.pallas import tpu as pltpu

F32 = jnp.float32
BF16 = jnp.bfloat16
I32 = jnp.int32
U32 = jnp.uint32

GRID_W = 64
ATT_HEADS = 8
ATT_KV_HEADS = 2
ATT_HEAD_DIM = 64
ATT_WIDTH = ATT_HEADS * ATT_HEAD_DIM
ATT_KV_WIDTH = ATT_KV_HEADS * ATT_HEAD_DIM
ROPE_THETA = 10000.0
HG_HEADS = 4
HG_DK = 128
HG_WIDTH = HG_HEADS * HG_DK
X_HEADS = 4
N_EXPERTS = 32
TOP_K = 4
SWIGLU_LIMIT = 7.0
SWIGLU_ALPHA = 1.702
LN_EPS = 1e-5
RMS_EPS = 1e-6
DEPTH = 1
DEEPNORM_ALPHA = (2.0 * DEPTH) ** 0.25

LANES = 128
VMEM_LIMIT_BYTES = 56 * 1024 * 1024

TOKEN_TILE = 512
ATT_Q_TILE = 256
ATT_K_TILE = 512
HG_CHUNK = 64
HG_SUB = 16

ATT_SAFE_SCORE = 40.0
ATT_NORM_SLACK = 1.01
HG_SAFE_LOG = 5.0
EXPERT_TILE = 512
COMBINE_TILE = 256
DISPATCH_TILE = 512
NEG_INF = float("-inf")


def _cparams(sem):
    return pltpu.CompilerParams(dimension_semantics=sem, vmem_limit_bytes=VMEM_LIMIT_BYTES)


def _layer_norm(x, g, b):
    mu = jnp.mean(x, axis=-1, keepdims=True)
    xc = x - mu
    var = jnp.mean(xc * xc, axis=-1, keepdims=True)
    return xc * lax.rsqrt(var + LN_EPS) * g + b


def _sigmoid(x):
    return 1.0 / (1.0 + jnp.exp(-x))


def _dot(a, b):
    return jnp.dot(a, b, preferred_element_type=F32)


def _dot_nt(a, b):
    return lax.dot_general(a, b, (((1,), (1,)), ((), ())), preferred_element_type=F32)


def _dot_tn(a, b):
    return lax.dot_general(a, b, (((0,), (0,)), ((), ())), preferred_element_type=F32)


def _split3(x):
    h1 = x.astype(BF16)
    r1 = x - h1.astype(F32)
    h2 = r1.astype(BF16)
    h3 = (r1 - h2.astype(F32)).astype(BF16)
    return h1, h2, h3


def _pack_bf16_pair(lo, hi):
    lo_b = pltpu.bitcast(lo.astype(BF16).astype(F32), U32)
    hi_b = pltpu.bitcast(hi.astype(BF16).astype(F32), U32)
    return (hi_b & jnp.uint32(0xFFFF0000)) | (lo_b >> 16)


def _unpack_bf16_pair(p):
    lo = pltpu.bitcast(p << 16, F32)
    hi = pltpu.bitcast(p & jnp.uint32(0xFFFF0000), F32)
    return lo, hi


def _inproj_kernel(x_ref, g_ref, b_ref, w_ref, cos_ref, sin_ref, qg_ref, kg_ref, lbf_ref, lbb_ref,
                   q_out, k_out, v_out, hq_out, hi_out, hg_out, lff_out, lfb_out):
    xn = _layer_norm(x_ref[...], g_ref[...], b_ref[...])
    xb = xn.astype(BF16)
    cos = cos_ref[...]
    sin = sin_ref[...]
    tm = xb.shape[0]

    lane = lax.broadcasted_iota(I32, (tm, LANES), 1)
    first_half = (lane & (ATT_HEAD_DIM - 1)) < (ATT_HEAD_DIM // 2)
    r = lax.broadcasted_iota(I32, (LANES, LANES), 0)
    c = lax.broadcasted_iota(I32, (LANES, LANES), 1)
    head_ones = jnp.where((r >> 6) == (c >> 6), 1.0, 0.0).astype(BF16)

    def normed_rope(u, gain, scale):
        sq = u * u
        s_hi = sq.astype(BF16)
        s_lo = (sq - s_hi.astype(F32)).astype(BF16)
        ms = (_dot(s_hi, head_ones) + _dot(s_lo, head_ones)) * (1.0 / ATT_HEAD_DIM)
        un = u * lax.rsqrt(ms + RMS_EPS) * gain
        rot = jnp.where(first_half, pltpu.roll(un, LANES - 32, 1), pltpu.roll(un, 32, 1))
        return (un * cos + rot * sin) * scale

    col = 0
    for j in range(ATT_WIDTH // LANES):
        u = _dot(xb, w_ref[:, col:col + LANES])
        q_out[:, j * LANES:(j + 1) * LANES] = normed_rope(u, qg_ref[...], ATT_HEAD_DIM ** -0.5).astype(BF16)
        col += LANES
    u = _dot(xb, w_ref[:, col:col + LANES])
    k_out[...] = normed_rope(u, kg_ref[...], 1.0).astype(BF16)
    col += LANES
    v_out[...] = _dot(xb, w_ref[:, col:col + LANES]).astype(BF16)
    col += LANES

    u = _dot(xb, w_ref[:, col:col + HG_WIDTH])
    hq_out[...] = (u * _sigmoid(u)).astype(BF16)
    col += HG_WIDTH
    for lb_ref, lf_out in ((lbf_ref, lff_out), (lbb_ref, lfb_out)):
        u = _dot(xb, w_ref[:, col:col + HG_WIDTH])
        lb = lb_ref[...]
        lf_out[...] = jnp.log(lb + (1.0 - lb) * _sigmoid(u))
        col += HG_WIDTH
    hi_out[...] = _dot(xb, w_ref[:, col:col + HG_WIDTH]).astype(BF16)
    col += HG_WIDTH
    u = _dot(xb, w_ref[:, col:col + HG_WIDTH])
    hg_out[...] = (u * _sigmoid(u)).astype(BF16)


def _inproj(x, g, b, w, cos_t, sin_t, qg, kg, lbf, lbb, pos_block):
    T, D = x.shape
    tm = TOKEN_TILE
    n_in = w.shape[1]
    row = lambda i: (i, 0)
    const = lambda i: (0, 0)
    out_shape = (
        jax.ShapeDtypeStruct((T, ATT_WIDTH), BF16),
        jax.ShapeDtypeStruct((T, ATT_KV_WIDTH), BF16),
        jax.ShapeDtypeStruct((T, ATT_KV_WIDTH), BF16),
        jax.ShapeDtypeStruct((T, HG_WIDTH), BF16),
        jax.ShapeDtypeStruct((T, HG_WIDTH), BF16),
        jax.ShapeDtypeStruct((T, HG_WIDTH), BF16),
        jax.ShapeDtypeStruct((T, HG_WIDTH), F32),
        jax.ShapeDtypeStruct((T, HG_WIDTH), F32),
    )
    out_specs = (
        pl.BlockSpec((tm, ATT_WIDTH), row),
        pl.BlockSpec((tm, ATT_KV_WIDTH), row),
        pl.BlockSpec((tm, ATT_KV_WIDTH), row),
        pl.BlockSpec((tm, HG_WIDTH), row),
        pl.BlockSpec((tm, HG_WIDTH), row),
        pl.BlockSpec((tm, HG_WIDTH), row),
        pl.BlockSpec((tm, HG_WIDTH), row),
        pl.BlockSpec((tm, HG_WIDTH), row),
    )
    return pl.pallas_call(
        _inproj_kernel,
        out_shape=out_shape,
        grid=(T // tm,),
        in_specs=[
            pl.BlockSpec((tm, D), row),
            pl.BlockSpec((1, D), const),
            pl.BlockSpec((1, D), const),
            pl.BlockSpec((D, n_in), const),
            pl.BlockSpec((tm, LANES), lambda i: (pos_block(i), 0)),
            pl.BlockSpec((tm, LANES), lambda i: (pos_block(i), 0)),
            pl.BlockSpec((1, LANES), const),
            pl.BlockSpec((1, LANES), const),
            pl.BlockSpec((1, HG_WIDTH), const),
            pl.BlockSpec((1, HG_WIDTH), const),
        ],
        out_specs=out_specs,
        compiler_params=_cparams(("parallel",)),
        name="inproj",
    )(x, g, b, w, cos_t, sin_t, qg, kg, lbf, lbb)


class _Groups:
    def __init__(self, n0, s0, n1, s1):
        self.n0, self.s0, self.n1, self.s1 = n0, s0, n1, s1
        self.t0 = n0 * s0
        self.total = self.t0 + n1 * s1

    def split(self, i, tile):
        nt0 = self.t0 // tile
        per0, per1 = self.s0 // tile, self.s1 // tile
        in1 = i >= nt0
        j1 = jnp.maximum(i - nt0, 0)
        seq = jnp.where(in1, self.n0 + j1 // per1, i // per0)
        pos = jnp.where(in1, j1 % per1, i % per0)
        return in1, seq, pos, jnp.where(in1, per1, per0)

    def seq_of_tile(self, i, tile):
        return self.split(i, tile)[1]

    def pos_of_tile(self, i, tile):
        return self.split(i, tile)[2]

    def reversed_tile(self, i, tile):
        _, _, pos, per = self.split(i, tile)
        return i - pos + (per - 1 - pos)


def _attention_kernel(*refs, n_kv_blocks, groups):
    q_ref = refs[0]
    k_refs = refs[1:1 + n_kv_blocks]
    v_refs = refs[1 + n_kv_blocks:1 + 2 * n_kv_blocks]
    o_ref, kmax_ref, out_ref = refs[1 + 2 * n_kv_blocks:]
    tq = q_ref.shape[0]
    kv_rows = k_refs[0].shape[0]
    tk = min(ATT_K_TILE, kv_rows)
    n_k = kv_rows // tk
    group = ATT_HEADS // ATT_KV_HEADS
    rows = group * tq
    lane = lax.broadcasted_iota(I32, (tq, LANES), 1)
    low_half = lane < ATT_HEAD_DIM
    in_group1, _, pos, _ = groups.split(pl.program_id(0), tq)
    extra_trips = jnp.where(in_group1, n_k, 0)

    @pl.when(pos == 0)
    def _():
        r = lax.broadcasted_iota(I32, (LANES, LANES), 0)
        c = lax.broadcasted_iota(I32, (LANES, LANES), 1)
        head_ones = jnp.where((r >> 6) == (c >> 6), 1.0, 0.0).astype(BF16)
        best = jnp.zeros((1, LANES), F32)
        for k_ref in k_refs:
            kf = k_ref[...].astype(F32)
            best = jnp.maximum(best, jnp.max(_dot((kf * kf).astype(BF16), head_ones), axis=0, keepdims=True))
        kmax_ref[...] = best * ATT_NORM_SLACK

    for kvh in range(ATT_KV_HEADS):
        keep = low_half if kvh == 0 else jnp.logical_not(low_half)
        parts = []
        for g in range(group):
            c0 = kvh * 2 * LANES + (g // 2) * LANES
            pair = q_ref[:, c0:c0 + LANES].astype(F32)
            if g % 2 != kvh:
                pair = pltpu.roll(pair, ATT_HEAD_DIM, 1)
            parts.append(jnp.where(keep, pair, 0.0).astype(BF16))
        qp = jnp.concatenate(parts, axis=0)

        qf = qp.astype(F32)
        q_sq = jnp.max(jnp.sum(qf * qf, axis=-1, keepdims=True))
        k_sq = jnp.max(jnp.where(keep[:1], kmax_ref[...], 0.0))
        unshifted_ok = q_sq * k_sq <= ATT_SAFE_SCORE * ATT_SAFE_SCORE

        def run(make_body, carry):
            carry = lax.fori_loop(0, n_k, make_body(k_refs[0], v_refs[0]), carry)
            for kb in range(1, n_kv_blocks):
                carry = lax.fori_loop(0, extra_trips, make_body(k_refs[kb], v_refs[kb]), carry)
            return carry

        @pl.when(unshifted_ok)
        def _():
            def make_body(k_ref, v_ref):
                def body(kc, carry):
                    l, acc = carry
                    r0 = pl.multiple_of(kc * tk, tk)
                    p = jnp.exp(_dot_nt(qp, k_ref[pl.ds(r0, tk), :]))
                    part = p[:, :LANES]
                    for t in range(1, tk // LANES):
                        part = part + p[:, t * LANES:(t + 1) * LANES]
                    return l + part, acc + _dot(p.astype(BF16), v_ref[pl.ds(r0, tk), :])
                return body

            l, acc = run(make_body, (jnp.zeros((rows, LANES), F32), jnp.zeros((rows, LANES), F32)))
            out_ref[...] = acc / jnp.sum(l, axis=-1, keepdims=True)

        @pl.when(jnp.logical_not(unshifted_ok))
        def _():
            def make_body(k_ref, v_ref):
                def body(kc, carry):
                    m, l, acc = carry
                    r0 = pl.multiple_of(kc * tk, tk)
                    s = _dot_nt(qp, k_ref[pl.ds(r0, tk), :])
                    m_new = jnp.maximum(m, jnp.max(s, axis=-1, keepdims=True))
                    alpha = jnp.exp(m - m_new)
                    p = jnp.exp(s - m_new)
                    l_new = alpha * l + jnp.sum(p, axis=-1, keepdims=True)
                    acc_new = alpha * acc + _dot(p.astype(BF16), v_ref[pl.ds(r0, tk), :])
                    return m_new, l_new, acc_new
                return body

            init = (jnp.full((rows, 1), NEG_INF, F32), jnp.zeros((rows, 1), F32), jnp.zeros((rows, LANES), F32))
            m, l, acc = run(make_body, init)
            out_ref[...] = acc / l

        out = out_ref[...]
        for j in range(group // 2):
            a = out[(2 * j) * tq:(2 * j + 1) * tq]
            b = out[(2 * j + 1) * tq:(2 * j + 2) * tq]
            if kvh == 1:
                a = pltpu.roll(a, ATT_HEAD_DIM, 1)
            else:
                b = pltpu.roll(b, ATT_HEAD_DIM, 1)
            c0 = kvh * 2 * LANES + j * LANES
            o_ref[:, c0:c0 + LANES] = jnp.where(low_half, a, b).astype(BF16)


def _attention(q, k, v, groups):
    T = q.shape[0]
    tq = ATT_Q_TILE
    kv_rows = groups.s0
    n_kv_blocks = groups.s1 // groups.s0

    def kv_map(j):
        def index_map(i):
            in1, seq, _, _ = groups.split(i, tq)
            first = jnp.where(in1, groups.t0 // kv_rows + (seq - groups.n0) * n_kv_blocks, seq)
            return (first + jnp.where(in1, j, 0), 0)
        return index_map

    kv_specs = [pl.BlockSpec((kv_rows, ATT_KV_WIDTH), kv_map(j)) for j in range(n_kv_blocks)]
    return pl.pallas_call(
        functools.partial(_attention_kernel, n_kv_blocks=n_kv_blocks, groups=groups),
        out_shape=jax.ShapeDtypeStruct((T, ATT_WIDTH), BF16),
        grid=(T // tq,),
        in_specs=[pl.BlockSpec((tq, ATT_WIDTH), lambda i: (i, 0))] + kv_specs + kv_specs,
        out_specs=pl.BlockSpec((tq, ATT_WIDTH), lambda i: (i, 0)),
        scratch_shapes=[pltpu.VMEM((1, LANES), F32),
                        pltpu.VMEM((ATT_HEADS // ATT_KV_HEADS * tq, LANES), F32)],
        compiler_params=_cparams(("arbitrary",)),
        name="attention",
    )(q, *([k] * n_kv_blocks), *([v] * n_kv_blocks))


def _hgrn_kernel(*refs, reverse, finalize, groups):
    if finalize:
        q_ref, v_ref, lf_ref, of_ref, g_ref, gain_ref, o_ref, st_ref = refs
    else:
        q_ref, v_ref, lf_ref, o_ref, st_ref = refs
    tb = q_ref.shape[0]
    C = min(HG_CHUNK, tb)
    n_chunk = tb // C
    n_sub = C // HG_SUB

    @pl.when(groups.pos_of_tile(pl.program_id(1), tb) == 0)
    def _():
        st_ref[...] = jnp.zeros_like(st_ref)

    ri = lax.broadcasted_iota(I32, (C, C), 0)
    ci = lax.broadcasted_iota(I32, (C, C), 1)
    sub_lo = (ri >> 4) << 4
    if reverse:
        inc = ci >= ri
        loc_m = inc & (ci < sub_lo + HG_SUB)
    else:
        inc = ci <= ri
        loc_m = inc & (ci >= sub_lo)
    cum_mat = jnp.where(inc, 1.0, 0.0).astype(BF16)
    loc_mat = jnp.where(loc_m, 1.0, 0.0).astype(BF16)
    row_id = lax.broadcasted_iota(I32, (C, 1), 0)
    sub_row = lax.broadcasted_iota(I32, (HG_SUB, 1), 0)
    lane_c = lax.broadcasted_iota(I32, (HG_SUB, C), 1)
    edge = 0 if reverse else C - 1

    def chunk(step, carry, *, pairwise):
        cidx = (n_chunk - 1 - step) if reverse else step
        r0 = cidx * C if isinstance(cidx, int) else pl.multiple_of(cidx * C, C)
        lf = lf_ref[pl.ds(r0, C), :]
        q = q_ref[pl.ds(r0, C), :].astype(F32)
        v = v_ref[pl.ds(r0, C), :]
        h1, h2, h3 = _split3(lf)
        cum = _dot(cum_mat, h1) + _dot(cum_mat, h2) + _dot(cum_mat, h3)
        loc = _dot(loc_mat, h1) + _dot(loc_mat, h2) + _dot(loc_mat, h3)
        kk = 1.0 - jnp.exp(lf)
        total = cum[edge:edge + 1, :]
        st = st_ref[...]
        o_inter = _dot_nt((q * jnp.exp(cum)).astype(BF16), st.astype(BF16))
        k_state = (kk * jnp.exp(total - cum)).astype(BF16)
        st_ref[...] = st * jnp.exp(total) + _dot_tn(v, k_state)
        qh_b = (q * jnp.exp(loc)).astype(BF16)
        blocks = []
        for s in range(n_sub):
            lo = s * HG_SUB
            hi = lo + HG_SUB
            if pairwise:
                has_off = (s < n_sub - 1) if reverse else (s > 0)
                if has_off:
                    if reverse:
                        expo = jnp.where(row_id >= hi, cum[hi:hi + 1, :] - cum, NEG_INF)
                    else:
                        expo = jnp.where(row_id < lo, cum[lo - 1:lo, :] - cum, NEG_INF)
                    k_off = (kk * jnp.exp(expo)).astype(BF16)
                    a_s = _dot_nt(qh_b[lo:hi], k_off)
                else:
                    a_s = jnp.zeros((HG_SUB, C), F32)
                loc_s = loc[lo:hi]
                q_s = q[lo:hi]
                for j in range(HG_SUB):
                    jj = lo + j
                    d = loc_s - loc[jj:jj + 1, :]
                    msk = (sub_row <= j) if reverse else (sub_row >= j)
                    e = jnp.exp(jnp.where(msk, d, NEG_INF))
                    colv = jnp.sum(q_s * e * kk[jj:jj + 1, :], axis=-1, keepdims=True)
                    a_s = jnp.where(lane_c == jj, colv, a_s)
            else:
                if reverse:
                    edge_row = cum[hi:hi + 1, :] if s < n_sub - 1 else jnp.zeros((1, HG_DK), F32)
                    expo = jnp.where(row_id >= lo, edge_row - cum, NEG_INF)
                    tri = lane_c >= lo + sub_row
                else:
                    edge_row = cum[lo - 1:lo, :] if s > 0 else jnp.zeros((1, HG_DK), F32)
                    expo = jnp.where(row_id < hi, edge_row - cum, NEG_INF)
                    tri = lane_c <= lo + sub_row
                k_rel = (kk * jnp.exp(expo)).astype(BF16)
                a_s = jnp.where(tri, _dot_nt(qh_b[lo:hi], k_rel), 0.0)
            blocks.append(a_s)
        a = jnp.concatenate(blocks, axis=0)
        o = o_inter + _dot(a.astype(BF16), v)
        if finalize:
            tot = o + of_ref[pl.ds(r0, C), :]
            ms = jnp.mean(tot * tot, axis=-1, keepdims=True)
            res = tot * lax.rsqrt(ms + RMS_EPS) * gain_ref[...] * g_ref[pl.ds(r0, C), :].astype(F32)
            o_ref[pl.ds(r0, C), :] = res.astype(o_ref.dtype)
        else:
            o_ref[pl.ds(r0, C), :] = o
        return carry

    mild_decay = jnp.min(lf_ref[...]) >= -HG_SAFE_LOG

    @pl.when(mild_decay)
    def _():
        for step in range(n_chunk):
            chunk(step, 0, pairwise=False)

    @pl.when(jnp.logical_not(mild_decay))
    def _():
        lax.fori_loop(0, n_chunk, functools.partial(chunk, pairwise=True), 0)


def _hgrn_pass(q, v, lf, groups, *, reverse, finalize_args=None):
    T = q.shape[0]
    tb = TOKEN_TILE
    finalize = finalize_args is not None

    def tmap(h, i):
        return (groups.reversed_tile(i, tb) if reverse else i, h)

    spec = pl.BlockSpec((tb, HG_DK), tmap)
    in_specs = [spec, spec, spec]
    args = [q, v, lf]
    if finalize:
        o_fwd, gate, gain = finalize_args
        in_specs += [spec, spec, pl.BlockSpec((1, HG_DK), lambda h, i: (0, 0))]
        args += [o_fwd, gate, gain]
    return pl.pallas_call(
        functools.partial(_hgrn_kernel, reverse=reverse, finalize=finalize, groups=groups),
        out_shape=jax.ShapeDtypeStruct((T, HG_WIDTH), BF16 if finalize else F32),
        grid=(HG_HEADS, T // tb),
        in_specs=in_specs,
        out_specs=spec,
        scratch_shapes=[pltpu.VMEM((HG_DK, HG_DK), F32)],
        compiler_params=_cparams(("parallel", "arbitrary")),
        name="hgrn_bwd" if reverse else "hgrn_fwd",
    )(*args)


def _kvproj_kernel(m_ref, w_ref, o_ref):
    o_ref[...] = _dot(m_ref[...].astype(BF16), w_ref[...]).astype(BF16)


def _kvproj(mem, w):
    R, D = mem.shape
    n = w.shape[1]
    tr = 256
    return pl.pallas_call(
        _kvproj_kernel,
        out_shape=jax.ShapeDtypeStruct((R, n), BF16),
        grid=(R // tr,),
        in_specs=[pl.BlockSpec((tr, D), lambda i: (i, 0)), pl.BlockSpec((D, n), lambda i: (0, 0))],
        out_specs=pl.BlockSpec((tr, n), lambda i: (i, 0)),
        compiler_params=_cparams(("parallel",)),
        name="kvproj",
    )(mem, w)


def _mix_xattn_kernel(x_ref, att_ref, rec_ref, g0_ref, b0_ref, wo_ref, g1_ref, b1_ref, wq_ref, kv_ref, wxo_ref,
                      g2_ref, b2_ref, x2_ref, p_ref):
    D = x_ref.shape[1]
    dh = D // X_HEADS
    xn = _layer_norm(x_ref[...], g0_ref[...], b0_ref[...])
    mix = _dot(att_ref[...], wo_ref[:ATT_WIDTH, :]) + _dot(rec_ref[...], wo_ref[ATT_WIDTH:, :])
    x1 = _layer_norm(DEEPNORM_ALPHA * xn + mix, g1_ref[...], b1_ref[...])
    qx = (_dot(x1.astype(BF16), wq_ref[...]) * (dh ** -0.5)).astype(BF16)
    heads = []
    for h in range(X_HEADS):
        kh = kv_ref[0, :, h * dh:(h + 1) * dh]
        vh = kv_ref[0, :, D + h * dh:D + (h + 1) * dh]
        s = _dot_nt(qx[:, h * dh:(h + 1) * dh], kh)
        s = s - jnp.max(s, axis=-1, keepdims=True)
        p = jnp.exp(s)
        p = p / jnp.sum(p, axis=-1, keepdims=True)
        heads.append(_dot(p.astype(BF16), vh).astype(BF16))
    o = jnp.concatenate(heads, axis=-1)
    x2 = _layer_norm(DEEPNORM_ALPHA * x1 + _dot(o, wxo_ref[...]), g2_ref[...], b2_ref[...])
    x2_ref[...] = x2
    p_ref[...] = _pack_bf16_pair(x2[:, :D // 2], x2[:, D // 2:])


def _mix_xattn(x, att, rec, g0, b0, wo, g1, b1, wq, kv, wxo, g2, b2, seq_of_tile):
    T, D = x.shape
    tm = TOKEN_TILE
    n_mem = kv.shape[1]
    row = lambda i: (i, 0)
    const = lambda i: (0, 0)
    vec = pl.BlockSpec((1, D), const)
    return pl.pallas_call(
        _mix_xattn_kernel,
        out_shape=(jax.ShapeDtypeStruct((T, D), F32), jax.ShapeDtypeStruct((T, D // 2), U32)),
        grid=(T // tm,),
        in_specs=[
            pl.BlockSpec((tm, D), row),
            pl.BlockSpec((tm, ATT_WIDTH), row),
            pl.BlockSpec((tm, HG_WIDTH), row),
            vec, vec,
            pl.BlockSpec((ATT_WIDTH + HG_WIDTH, D), const),
            vec, vec,
            pl.BlockSpec((D, D), const),
            pl.BlockSpec((1, n_mem, 2 * D), lambda i: (seq_of_tile(i), 0, 0)),
            pl.BlockSpec((D, D), const),
            vec, vec,
        ],
        out_specs=(pl.BlockSpec((tm, D), row), pl.BlockSpec((tm, D // 2), row)),
        compiler_params=_cparams(("parallel",)),
        name="mix_xattn",
    )(x, att, rec, g0, b0, wo, g1, b1, wq, kv, wxo, g2, b2)


def _router_kernel(x_ref, wt_ref, b_ref, idx_ref, gate_ref, rank_ref, cnt_ref, run_ref):
    tm = x_ref.shape[0]

    @pl.when(pl.program_id(0) == 0)
    def _():
        run_ref[...] = jnp.zeros_like(run_ref)

    xh, xl, _ = _split3(x_ref[...])
    wh, wl, _ = _split3(wt_ref[...])
    logits = _dot_nt(wh, xh) + _dot_nt(wh, xl) + _dot_nt(wl, xh) + b_ref[...]
    e_id = lax.broadcasted_iota(I32, (N_EXPERTS, tm), 0)
    work = logits
    vals, idxs = [], []
    chosen = jnp.zeros((N_EXPERTS, tm), F32)
    for _ in range(TOP_K):
        m = jnp.max(work, axis=0, keepdims=True)
        idx = jnp.min(jnp.where(work == m, e_id, N_EXPERTS), axis=0, keepdims=True)
        hit = e_id == idx
        chosen = jnp.where(hit, 1.0, chosen)
        work = jnp.where(hit, NEG_INF, work)
        vals.append(m)
        idxs.append(idx)
    ex = [jnp.exp(v - vals[0]) for v in vals]
    den = ex[0] + ex[1] + ex[2] + ex[3]
    t_r = lax.broadcasted_iota(I32, (tm, tm), 0)
    t_c = lax.broadcasted_iota(I32, (tm, tm), 1)
    before = jnp.where(t_r < t_c, 1.0, 0.0).astype(BF16)
    rank_e = _dot(chosen.astype(BF16), before) + run_ref[...]
    run_ref[...] = run_ref[...] + jnp.sum(chosen, axis=1, keepdims=True)
    for k in range(TOP_K):
        idx_ref[k:k + 1, :] = idxs[k]
        gate_ref[k:k + 1, :] = ex[k] / den
        rank_ref[k:k + 1, :] = jnp.sum(jnp.where(e_id == idxs[k], rank_e, 0.0), axis=0, keepdims=True).astype(I32)
    cnt_ref[...] = run_ref[...].astype(I32)


def _router(x2, w_router_t, b_router):
    T, D = x2.shape
    tm = TOKEN_TILE
    tok = pl.BlockSpec((TOP_K, tm), lambda i: (0, i))
    return pl.pallas_call(
        _router_kernel,
        out_shape=(jax.ShapeDtypeStruct((TOP_K, T), I32), jax.ShapeDtypeStruct((TOP_K, T), F32),
                   jax.ShapeDtypeStruct((TOP_K, T), I32), jax.ShapeDtypeStruct((N_EXPERTS, 1), I32)),
        grid=(T // tm,),
        in_specs=[pl.BlockSpec((tm, D), lambda i: (i, 0)),
                  pl.BlockSpec((N_EXPERTS, D), lambda i: (0, 0)),
                  pl.BlockSpec((N_EXPERTS, 1), lambda i: (0, 0))],
        out_specs=(tok, tok, tok, pl.BlockSpec((N_EXPERTS, 1), lambda i: (0, 0))),
        scratch_shapes=[pltpu.VMEM((N_EXPERTS, 1), F32)],
        compiler_params=_cparams(("arbitrary",)),
        name="router",
    )(x2, w_router_t, b_router)


def _dispatch_kernel(dest_ref, p_ref, zero_hbm, out_hbm, sem):
    del zero_hbm
    td = DISPATCH_TILE

    def row_copy(t, k):
        d = dest_ref[0, 0, k * td + t]
        return pltpu.make_async_copy(p_ref.at[pl.ds(t, 1)], out_hbm.at[pl.ds(d, 1)], sem)

    def start(t, c):
        for k in range(TOP_K):
            row_copy(t, k).start()
        return c

    def wait(t, c):
        for k in range(TOP_K):
            row_copy(t, k).wait()
        return c

    lax.fori_loop(0, td, start, 0)
    lax.fori_loop(0, td, wait, 0)


def _dispatch(dest_tiles, p, zeros):
    T = p.shape[0]
    td = DISPATCH_TILE
    return pl.pallas_call(
        _dispatch_kernel,
        out_shape=jax.ShapeDtypeStruct(zeros.shape, zeros.dtype),
        grid=(T // td,),
        in_specs=[pl.BlockSpec((1, 1, TOP_K * td), lambda i: (i, 0, 0), memory_space=pltpu.SMEM),
                  pl.BlockSpec((td, p.shape[1]), lambda i: (i, 0)),
                  pl.BlockSpec(memory_space=pl.ANY)],
        out_specs=pl.BlockSpec(memory_space=pl.ANY),
        scratch_shapes=[pltpu.SemaphoreType.DMA],
        input_output_aliases={2: 0},
        compiler_params=_cparams(("arbitrary",)),
        name="dispatch",
    )(dest_tiles, p, zeros)


def _expert_kernel(blk_e_ref, nvalid_ref, p_ref, wu_ref, bu_ref, wd_ref, bd_ref, y_ref):
    i = pl.program_id(0)
    dff = wd_ref.shape[1]
    half = p_ref.shape[1]

    @pl.when(i < nvalid_ref[0])
    def _():
        lo, hi = _unpack_bf16_pair(p_ref[...])
        lo = lo.astype(BF16)
        hi = hi.astype(BF16)

        def up(c0):
            return (_dot(lo, wu_ref[0, :half, c0:c0 + dff]) + _dot(hi, wu_ref[0, half:, c0:c0 + dff])
                    + bu_ref[0, :, c0:c0 + dff])

        glu = jnp.minimum(up(0), SWIGLU_LIMIT)
        lin = jnp.clip(up(dff), -SWIGLU_LIMIT, SWIGLU_LIMIT)
        a = glu * _sigmoid(SWIGLU_ALPHA * glu) * (lin + 1.0)
        y = _dot(a.astype(BF16), wd_ref[0]) + bd_ref[0]
        y_ref[...] = _pack_bf16_pair(y[:, :half], y[:, half:])

    @pl.when(i >= nvalid_ref[0])
    def _():
        y_ref[...] = jnp.zeros_like(y_ref)


def _experts(blk_expert, n_valid, ps, wu, bu, wd, bd):
    R, half = ps.shape
    te = EXPERT_TILE
    E, D, two_f = wu.shape
    dff = wd.shape[1]
    grid_spec = pltpu.PrefetchScalarGridSpec(
        num_scalar_prefetch=2,
        grid=(R // te,),
        in_specs=[
            pl.BlockSpec((te, half), lambda i, be, nv: (i, 0)),
            pl.BlockSpec((1, D, two_f), lambda i, be, nv: (be[i], 0, 0)),
            pl.BlockSpec((1, 1, two_f), lambda i, be, nv: (be[i], 0, 0)),
            pl.BlockSpec((1, dff, D), lambda i, be, nv: (be[i], 0, 0)),
            pl.BlockSpec((1, 1, D), lambda i, be, nv: (be[i], 0, 0)),
        ],
        out_specs=pl.BlockSpec((te, half), lambda i, be, nv: (i, 0)),
    )
    return pl.pallas_call(
        _expert_kernel,
        out_shape=jax.ShapeDtypeStruct((R, half), U32),
        grid_spec=grid_spec,
        compiler_params=_cparams(("arbitrary",)),
        name="experts",
    )(blk_expert, n_valid, ps, wu, bu, wd, bd)


def _combine_kernel(dest_ref, ys_hbm, x2_ref, gate_ref, g_ref, b_ref, o_ref, buf, sem):
    tc = x2_ref.shape[0]
    half = buf.shape[2]

    def row_copy(t, k):
        d = dest_ref[0, 0, k * tc + t]
        return pltpu.make_async_copy(ys_hbm.at[pl.ds(d, 1)], buf.at[k, pl.ds(t, 1)], sem)

    def start(t, c):
        for k in range(TOP_K):
            row_copy(t, k).start()
        return c

    def wait(t, c):
        for k in range(TOP_K):
            row_copy(t, k).wait()
        return c

    lax.fori_loop(0, tc, start, 0)
    lax.fori_loop(0, tc, wait, 0)

    gates = gate_ref[...]
    acc_lo = jnp.zeros((tc, half), F32)
    acc_hi = jnp.zeros((tc, half), F32)
    for k in range(TOP_K):
        lo, hi = _unpack_bf16_pair(buf[k])
        gk = gates[:, k:k + 1]
        acc_lo = acc_lo + gk * lo
        acc_hi = acc_hi + gk * hi
    ffn = jnp.concatenate([acc_lo, acc_hi], axis=-1)
    o_ref[...] = _layer_norm(DEEPNORM_ALPHA * x2_ref[...] + ffn, g_ref[...], b_ref[...])


def _combine(dest_tiles, ys, x2, gates, g, b, *, tile0, n_tiles):
    D = x2.shape[1]
    tc = COMBINE_TILE
    return pl.pallas_call(
        _combine_kernel,
        out_shape=jax.ShapeDtypeStruct((n_tiles * tc, D), F32),
        grid=(n_tiles,),
        in_specs=[pl.BlockSpec((1, 1, TOP_K * tc), lambda i: (tile0 + i, 0, 0), memory_space=pltpu.SMEM),
                  pl.BlockSpec(memory_space=pl.ANY),
                  pl.BlockSpec((tc, D), lambda i: (tile0 + i, 0)),
                  pl.BlockSpec((tc, TOP_K), lambda i: (tile0 + i, 0)),
                  pl.BlockSpec((1, D), lambda i: (0, 0)),
                  pl.BlockSpec((1, D), lambda i: (0, 0))],
        out_specs=pl.BlockSpec((tc, D), lambda i: (i, 0)),
        scratch_shapes=[pltpu.VMEM((TOP_K, tc, D // 2), U32), pltpu.SemaphoreType.DMA],
        compiler_params=_cparams(("arbitrary",)),
        name="combine",
    )(dest_tiles, ys, x2, gates, g, b)


def _rope_tables(seq_len):
    rows = seq_len // GRID_W
    row = jnp.repeat(jnp.arange(rows, dtype=F32), GRID_W)
    colp = (jnp.arange(seq_len) % GRID_W).astype(F32)
    axis_dim = ATT_HEAD_DIM // 2
    inv_freq = ROPE_THETA ** (-jnp.arange(0, axis_dim, 2, dtype=F32) / axis_dim)
    ang = jnp.concatenate([row[:, None] * inv_freq, colp[:, None] * inv_freq], axis=-1)
    cos, sin = jnp.cos(ang), jnp.sin(ang)
    cos_h = jnp.concatenate([cos, cos], axis=-1)
    sin_h = jnp.concatenate([-sin, sin], axis=-1)
    return jnp.tile(cos_h, (1, LANES // ATT_HEAD_DIM)), jnp.tile(sin_h, (1, LANES // ATT_HEAD_DIM))


def _tiles_of(idx_t, tile):
    K, T = idx_t.shape
    return idx_t.reshape(K, T // tile, tile).transpose(1, 0, 2).reshape(T // tile, 1, K * tile)


def kernel(x_prompt, x_sample, mem_prompt, mem_sample, ln_in_g, ln_in_b, w_in, q_norm_g, k_norm_g, hg_lb_fwd, hg_lb_bwd, hg_norm_g, w_out, ln1_g, ln1_b, w_xq, w_xkv, w_xo, ln2_g, ln2_b, w_router, b_router, w_up, b_up, w_down, b_down, ln3_g, ln3_b):
    Bp, Sp, D = x_prompt.shape
    Bs, Ss, _ = x_sample.shape
    Tp, Ts = Bp * Sp, Bs * Ss
    T = Tp + Ts
    tm = TOKEN_TILE
    assert Sp % tm == 0 and Ss % Sp == 0
    n_mem = mem_prompt.shape[1]
    layer = 0

    x = jnp.concatenate([x_prompt.reshape(Tp, D), x_sample.reshape(Ts, D)], axis=0)
    mem = jnp.concatenate([mem_prompt.reshape(Bp * n_mem, D), mem_sample.reshape(Bs * n_mem, D)], axis=0)

    def deinterleave(a):
        lead = a.shape[:-1]
        a = a.reshape(*lead, -1, ATT_HEAD_DIM // 2, 2)
        return jnp.swapaxes(a, -1, -2).reshape(*lead, -1)

    n_qk = ATT_WIDTH + ATT_KV_WIDTH
    w_in_l = w_in[layer]
    w_in_b = jnp.concatenate([deinterleave(w_in_l[:, :n_qk]), w_in_l[:, n_qk:]], axis=1).astype(BF16)
    pair = LANES // ATT_HEAD_DIM
    qg = jnp.tile(deinterleave(q_norm_g[layer]), pair).reshape(1, LANES)
    kg = jnp.tile(deinterleave(k_norm_g[layer]), pair).reshape(1, LANES)

    def lower_bound(logits):
        sm = jax.nn.softmax(logits.astype(F32), axis=0)
        return (jnp.cumsum(sm, axis=0)[layer + 1] - sm[0]).reshape(1, HG_WIDTH)

    lbf, lbb = lower_bound(hg_lb_fwd), lower_bound(hg_lb_bwd)
    cos_t, sin_t = _rope_tables(max(Sp, Ss))
    vec = lambda a: a.reshape(1, -1)

    groups = _Groups(Bp, Sp, Bs, Ss)
    pos_block = lambda i: groups.pos_of_tile(i, tm)
    seq_of_tile = lambda i: groups.seq_of_tile(i, tm)

    q, k, v, hq, hi, hg, lff, lfb = _inproj(x, vec(ln_in_g), vec(ln_in_b), w_in_b, cos_t, sin_t, qg, kg, lbf, lbb,
                                           pos_block)

    att = _attention(q, k, v, groups)
    o_fwd = _hgrn_pass(hq, hi, lff, groups, reverse=False)
    gain = hg_norm_g[layer].reshape(1, HG_DK)
    rec = _hgrn_pass(hq, hi, lfb, groups, reverse=True, finalize_args=(o_fwd, hg, gain))

    kv = _kvproj(mem, w_xkv[layer].astype(BF16)).reshape(Bp + Bs, n_mem, 2 * D)
    x2, packed = _mix_xattn(x, att, rec, vec(ln_in_g), vec(ln_in_b), w_out[layer].astype(BF16),
                            vec(ln1_g[layer]), vec(ln1_b[layer]), w_xq[layer].astype(BF16), kv,
                            w_xo[layer].astype(BF16), vec(ln2_g[layer]), vec(ln2_b[layer]), seq_of_tile)

    idx_t, gate_t, rank_t, counts = _router(x2, w_router[layer].T, b_router[layer].reshape(N_EXPERTS, 1))

    te = EXPERT_TILE
    counts = counts.reshape(N_EXPERTS)
    padded = (counts + te - 1) // te * te
    pad_end = jnp.cumsum(padded)
    pad_start = pad_end - padded
    n_blk = (T * TOP_K + N_EXPERTS * (te - 1) + te - 1) // te
    e_ids = jnp.arange(N_EXPERTS, dtype=I32)
    start_of = jnp.sum(jnp.where(idx_t[None] == e_ids[:, None, None], pad_start[:, None, None], 0), axis=0)
    dest_t = start_of + rank_t
    blk_row0 = jnp.arange(n_blk, dtype=I32) * te
    blk_expert = jnp.minimum(jnp.sum((pad_end[None, :] <= blk_row0[:, None]).astype(I32), axis=1), N_EXPERTS - 1)
    n_valid = (pad_end[-1] // te).astype(I32).reshape(1)

    sorted_rows = _dispatch(_tiles_of(dest_t, DISPATCH_TILE), packed, jnp.zeros((n_blk * te, D // 2), U32))
    ys = _experts(blk_expert, n_valid, sorted_rows, w_up[layer].astype(BF16),
                  b_up[layer].reshape(N_EXPERTS, 1, -1), w_down[layer].astype(BF16),
                  b_down[layer].reshape(N_EXPERTS, 1, -1))

    dest_c = _tiles_of(dest_t, COMBINE_TILE)
    gates = gate_t.T
    tc = COMBINE_TILE
    y_p = _combine(dest_c, ys, x2, gates, vec(ln3_g[layer]), vec(ln3_b[layer]), tile0=0, n_tiles=Tp // tc)
    y_s = _combine(dest_c, ys, x2, gates, vec(ln3_g[layer]), vec(ln3_b[layer]), tile0=Tp // tc, n_tiles=Ts // tc)
    return y_p.reshape(Bp, Sp, D), y_s.reshape(Bs, Ss, D)
```

```python
import functools

import jax
import jax.numpy as jnp
import numpy as np
from jax import lax
from jax.experimental import pallas as pl
from jax.experimental.pallas import tpu as pltpu

F32 = jnp.float32
BF16 = jnp.bfloat16
I32 = jnp.int32
U32 = jnp.uint32

GRID_W = 64
ATT_HEADS = 8
ATT_KV_HEADS = 2
ATT_HEAD_DIM = 64
ATT_WIDTH = ATT_HEADS * ATT_HEAD_DIM
ATT_KV_WIDTH = ATT_KV_HEADS * ATT_HEAD_DIM
ROPE_THETA = 10000.0
HG_HEADS = 4
HG_DK = 128
HG_WIDTH = HG_HEADS * HG_DK
X_HEADS = 4
N_EXPERTS = 32
TOP_K = 4
SWIGLU_LIMIT = 7.0
SWIGLU_ALPHA = 1.702
LN_EPS = 1e-5
RMS_EPS = 1e-6
DEPTH = 1
DEEPNORM_ALPHA = (2.0 * DEPTH) ** 0.25

LANES = 128
VMEM_LIMIT_BYTES = 56 * 1024 * 1024
DMA_PRIORITIES = 2

TOKEN_TILE = 512
ATT_Q_TILE = 256
ATT_K_TILE = 4096
HG_CHUNK = 64
HG_SUB = 16

ATT_SAFE_SCORE = 40.0
ATT_NORM_SLACK = 1.01
HG_SAFE_LOG = 5.0
EXPERT_TILE = 512
COMBINE_TILE = 256
DISPATCH_TILE = 512
NEG_INF = float("-inf")


def _cparams(sem):
    return pltpu.CompilerParams(dimension_semantics=sem, vmem_limit_bytes=VMEM_LIMIT_BYTES)


def _layer_norm(x, g, b):
    mu = jnp.mean(x, axis=-1, keepdims=True)
    xc = x - mu
    var = jnp.mean(xc * xc, axis=-1, keepdims=True)
    return xc * lax.rsqrt(var + LN_EPS) * g + b


def _sigmoid(x):
    return 1.0 / (1.0 + jnp.exp(-x))


def _dot(a, b):
    return jnp.dot(a, b, preferred_element_type=F32)


def _dot_nt(a, b):
    return lax.dot_general(a, b, (((1,), (1,)), ((), ())), preferred_element_type=F32)


def _dot_tn(a, b):
    return lax.dot_general(a, b, (((0,), (0,)), ((), ())), preferred_element_type=F32)


def _split3(x):
    h1 = x.astype(BF16)
    r1 = x - h1.astype(F32)
    h2 = r1.astype(BF16)
    h3 = (r1 - h2.astype(F32)).astype(BF16)
    return h1, h2, h3


def _pack_bf16_pair(lo, hi):
    lo_b = pltpu.bitcast(lo.astype(BF16).astype(F32), U32)
    hi_b = pltpu.bitcast(hi.astype(BF16).astype(F32), U32)
    return (hi_b & jnp.uint32(0xFFFF0000)) | (lo_b >> 16)


def _unpack_bf16_pair(p):
    lo = pltpu.bitcast(p << 16, F32)
    hi = pltpu.bitcast(p & jnp.uint32(0xFFFF0000), F32)
    return lo, hi


def _group_rows(x0_ref, x1_ref, tiles_group0):
    return jnp.where(pl.program_id(0) < tiles_group0, x0_ref[...], x1_ref[...])


def _group_specs(tile, width, tiles_group0):
    return [pl.BlockSpec((tile, width), lambda i: (jnp.minimum(i, tiles_group0 - 1), 0)),
            pl.BlockSpec((tile, width), lambda i: (jnp.maximum(i - tiles_group0, 0), 0))]


def _inproj_kernel(x0_ref, x1_ref, g_ref, b_ref, w_ref, cos_ref, sin_ref, qg_ref, kg_ref, lbf_ref, lbb_ref,
                   q_out, k_out, v_out, hq_out, hi_out, hg_out, lff_out, lfb_out, *, tiles_group0):
    xn = _layer_norm(_group_rows(x0_ref, x1_ref, tiles_group0), g_ref[...], b_ref[...])
    xb = xn.astype(BF16)
    cos = cos_ref[...]
    sin = sin_ref[...]
    tm = xb.shape[0]

    lane = lax.broadcasted_iota(I32, (tm, LANES), 1)
    first_half = (lane & (ATT_HEAD_DIM - 1)) < (ATT_HEAD_DIM // 2)
    r = lax.broadcasted_iota(I32, (LANES, LANES), 0)
    c = lax.broadcasted_iota(I32, (LANES, LANES), 1)
    head_ones = jnp.where((r >> 6) == (c >> 6), 1.0, 0.0).astype(BF16)

    def normed_rope(u, gain, scale):
        sq = u * u
        s_hi = sq.astype(BF16)
        s_lo = (sq - s_hi.astype(F32)).astype(BF16)
        ms = (_dot(s_hi, head_ones) + _dot(s_lo, head_ones)) * (1.0 / ATT_HEAD_DIM)
        un = u * lax.rsqrt(ms + RMS_EPS) * gain
        rot = jnp.where(first_half, pltpu.roll(un, LANES - 32, 1), pltpu.roll(un, 32, 1))
        return (un * cos + rot * sin) * scale

    col = 0
    for j in range(ATT_WIDTH // LANES):
        u = _dot(xb, w_ref[:, col:col + LANES])
        q_out[:, j * LANES:(j + 1) * LANES] = normed_rope(u, qg_ref[...], ATT_HEAD_DIM ** -0.5).astype(BF16)
        col += LANES
    u = _dot(xb, w_ref[:, col:col + LANES])
    k_out[...] = normed_rope(u, kg_ref[...], 1.0).astype(BF16)
    col += LANES
    v_out[...] = _dot(xb, w_ref[:, col:col + LANES]).astype(BF16)
    col += LANES

    u = _dot(xb, w_ref[:, col:col + HG_WIDTH])
    hq_out[...] = (u * _sigmoid(u)).astype(BF16)
    col += HG_WIDTH
    for lb_ref, lf_out in ((lbf_ref, lff_out), (lbb_ref, lfb_out)):
        u = _dot(xb, w_ref[:, col:col + HG_WIDTH])
        lb = lb_ref[...]
        lf_out[...] = jnp.log(lb + (1.0 - lb) * _sigmoid(u))
        col += HG_WIDTH
    hi_out[...] = _dot(xb, w_ref[:, col:col + HG_WIDTH]).astype(BF16)
    col += HG_WIDTH
    u = _dot(xb, w_ref[:, col:col + HG_WIDTH])
    hg_out[...] = (u * _sigmoid(u)).astype(BF16)


def _inproj(x0, x1, g, b, w, cos_t, sin_t, qg, kg, lbf, lbb, pos_block):
    D = x0.shape[1]
    T = x0.shape[0] + x1.shape[0]
    tm = TOKEN_TILE
    tiles_group0 = x0.shape[0] // tm
    n_in = w.shape[1]
    row = lambda i: (i, 0)
    const = lambda i: (0, 0)
    out_shape = (
        jax.ShapeDtypeStruct((T, ATT_WIDTH), BF16),
        jax.ShapeDtypeStruct((T, ATT_KV_WIDTH), BF16),
        jax.ShapeDtypeStruct((T, ATT_KV_WIDTH), BF16),
        jax.ShapeDtypeStruct((T, HG_WIDTH), BF16),
        jax.ShapeDtypeStruct((T, HG_WIDTH), BF16),
        jax.ShapeDtypeStruct((T, HG_WIDTH), BF16),
        jax.ShapeDtypeStruct((T, HG_WIDTH), F32),
        jax.ShapeDtypeStruct((T, HG_WIDTH), F32),
    )
    out_specs = (
        pl.BlockSpec((tm, ATT_WIDTH), row),
        pl.BlockSpec((tm, ATT_KV_WIDTH), row),
        pl.BlockSpec((tm, ATT_KV_WIDTH), row),
        pl.BlockSpec((tm, HG_WIDTH), row),
        pl.BlockSpec((tm, HG_WIDTH), row),
        pl.BlockSpec((tm, HG_WIDTH), row),
        pl.BlockSpec((tm, HG_WIDTH), row),
        pl.BlockSpec((tm, HG_WIDTH), row),
    )
    return pl.pallas_call(
        functools.partial(_inproj_kernel, tiles_group0=tiles_group0),
        out_shape=out_shape,
        grid=(T // tm,),
        in_specs=_group_specs(tm, D, tiles_group0) + [
            pl.BlockSpec((1, D), const),
            pl.BlockSpec((1, D), const),
            pl.BlockSpec((D, n_in), const),
            pl.BlockSpec((tm, LANES), lambda i: (pos_block(i), 0)),
            pl.BlockSpec((tm, LANES), lambda i: (pos_block(i), 0)),
            pl.BlockSpec((1, LANES), const),
            pl.BlockSpec((1, LANES), const),
            pl.BlockSpec((1, HG_WIDTH), const),
            pl.BlockSpec((1, HG_WIDTH), const),
        ],
        out_specs=out_specs,
        compiler_params=_cparams(("parallel",)),
        name="inproj",
    )(x0, x1, g, b, w, cos_t, sin_t, qg, kg, lbf, lbb)


class _Groups:
    def __init__(self, n0, s0, n1, s1):
        self.n0, self.s0, self.n1, self.s1 = n0, s0, n1, s1
        self.t0 = n0 * s0
        self.total = self.t0 + n1 * s1

    def split(self, i, tile):
        nt0 = self.t0 // tile
        per0, per1 = self.s0 // tile, self.s1 // tile
        in1 = i >= nt0
        j1 = jnp.maximum(i - nt0, 0)
        seq = jnp.where(in1, self.n0 + j1 // per1, i // per0)
        pos = jnp.where(in1, j1 % per1, i % per0)
        return in1, seq, pos, jnp.where(in1, per1, per0)

    def seq_of_tile(self, i, tile):
        return self.split(i, tile)[1]

    def pos_of_tile(self, i, tile):
        return self.split(i, tile)[2]

    def reversed_tile(self, i, tile):
        _, _, pos, per = self.split(i, tile)
        return i - pos + (per - 1 - pos)


def _attention_kernel(*refs, n_kv_blocks, groups):
    q_ref = refs[0]
    k_refs = refs[1:1 + n_kv_blocks]
    v_refs = refs[1 + n_kv_blocks:1 + 2 * n_kv_blocks]
    o_ref, kmax_ref, out_ref = refs[1 + 2 * n_kv_blocks:]
    tq = q_ref.shape[0]
    kv_rows = k_refs[0].shape[0]
    tk = min(ATT_K_TILE, kv_rows)
    n_k = kv_rows // tk
    group = ATT_HEADS // ATT_KV_HEADS
    rows = group * tq
    lane = lax.broadcasted_iota(I32, (tq, LANES), 1)
    low_half = lane < ATT_HEAD_DIM
    in_group1, _, pos, _ = groups.split(pl.program_id(0), tq)
    extra_trips = jnp.where(in_group1, n_k, 0)

    @pl.when(pos == 0)
    def _():
        r = lax.broadcasted_iota(I32, (LANES, LANES), 0)
        c = lax.broadcasted_iota(I32, (LANES, LANES), 1)
        head_ones = jnp.where((r >> 6) == (c >> 6), 1.0, 0.0).astype(BF16)
        best = jnp.zeros((1, LANES), F32)
        for k_ref in k_refs:
            kf = k_ref[...].astype(F32)
            best = jnp.maximum(best, jnp.max(_dot((kf * kf).astype(BF16), head_ones), axis=0, keepdims=True))
        kmax_ref[...] = best * ATT_NORM_SLACK

    for kvh in range(ATT_KV_HEADS):
        keep = low_half if kvh == 0 else jnp.logical_not(low_half)
        parts = []
        for g in range(group):
            c0 = kvh * 2 * LANES + (g // 2) * LANES
            pair = q_ref[:, c0:c0 + LANES].astype(F32)
            if g % 2 != kvh:
                pair = pltpu.roll(pair, ATT_HEAD_DIM, 1)
            parts.append(jnp.where(keep, pair, 0.0).astype(BF16))
        qp = jnp.concatenate(parts, axis=0)

        qf = qp.astype(F32)
        q_sq = jnp.max(jnp.sum(qf * qf, axis=-1, keepdims=True))
        k_sq = jnp.max(jnp.where(keep[:1], kmax_ref[...], 0.0))
        unshifted_ok = q_sq * k_sq <= ATT_SAFE_SCORE * ATT_SAFE_SCORE

        def run(make_body, carry):
            carry = lax.fori_loop(0, n_k, make_body(k_refs[0], v_refs[0]), carry)
            for kb in range(1, n_kv_blocks):
                carry = lax.fori_loop(0, extra_trips, make_body(k_refs[kb], v_refs[kb]), carry)
            return carry

        @pl.when(unshifted_ok)
        def _():
            def make_body(k_ref, v_ref):
                def body(kc, carry):
                    l, acc = carry
                    r0 = pl.multiple_of(kc * tk, tk)
                    p = jnp.exp(_dot_nt(qp, k_ref[pl.ds(r0, tk), :]))
                    part = p[:, :LANES]
                    for t in range(1, tk // LANES):
                        part = part + p[:, t * LANES:(t + 1) * LANES]
                    return l + part, acc + _dot(p.astype(BF16), v_ref[pl.ds(r0, tk), :])
                return body

            l, acc = run(make_body, (jnp.zeros((rows, LANES), F32), jnp.zeros((rows, LANES), F32)))
            out_ref[...] = acc / jnp.sum(l, axis=-1, keepdims=True)

        @pl.when(jnp.logical_not(unshifted_ok))
        def _():
            def make_body(k_ref, v_ref):
                def body(kc, carry):
                    m, l, acc = carry
                    r0 = pl.multiple_of(kc * tk, tk)
                    s = _dot_nt(qp, k_ref[pl.ds(r0, tk), :])
                    m_new = jnp.maximum(m, jnp.max(s, axis=-1, keepdims=True))
                    alpha = jnp.exp(m - m_new)
                    p = jnp.exp(s - m_new)
                    l_new = alpha * l + jnp.sum(p, axis=-1, keepdims=True)
                    acc_new = alpha * acc + _dot(p.astype(BF16), v_ref[pl.ds(r0, tk), :])
                    return m_new, l_new, acc_new
                return body

            init = (jnp.full((rows, 1), NEG_INF, F32), jnp.zeros((rows, 1), F32), jnp.zeros((rows, LANES), F32))
            m, l, acc = run(make_body, init)
            out_ref[...] = acc / l

        out = out_ref[...]
        for j in range(group // 2):
            a = out[(2 * j) * tq:(2 * j + 1) * tq]
            b = out[(2 * j + 1) * tq:(2 * j + 2) * tq]
            if kvh == 1:
                a = pltpu.roll(a, ATT_HEAD_DIM, 1)
            else:
                b = pltpu.roll(b, ATT_HEAD_DIM, 1)
            c0 = kvh * 2 * LANES + j * LANES
            o_ref[:, c0:c0 + LANES] = jnp.where(low_half, a, b).astype(BF16)


def _attention(q, k, v, groups):
    T = q.shape[0]
    tq = ATT_Q_TILE
    kv_rows = groups.s0
    n_kv_blocks = groups.s1 // groups.s0

    def kv_map(j):
        def index_map(i):
            in1, seq, _, _ = groups.split(i, tq)
            first = jnp.where(in1, groups.t0 // kv_rows + (seq - groups.n0) * n_kv_blocks, seq)
            return (first + jnp.where(in1, j, 0), 0)
        return index_map

    kv_specs = [pl.BlockSpec((kv_rows, ATT_KV_WIDTH), kv_map(j)) for j in range(n_kv_blocks)]
    return pl.pallas_call(
        functools.partial(_attention_kernel, n_kv_blocks=n_kv_blocks, groups=groups),
        out_shape=jax.ShapeDtypeStruct((T, ATT_WIDTH), BF16),
        grid=(T // tq,),
        in_specs=[pl.BlockSpec((tq, ATT_WIDTH), lambda i: (i, 0))] + kv_specs + kv_specs,
        out_specs=pl.BlockSpec((tq, ATT_WIDTH), lambda i: (i, 0)),
        scratch_shapes=[pltpu.VMEM((1, LANES), F32),
                        pltpu.VMEM((ATT_HEADS // ATT_KV_HEADS * tq, LANES), F32)],
        compiler_params=_cparams(("arbitrary",)),
        name="attention",
    )(q, *([k] * n_kv_blocks), *([v] * n_kv_blocks))


def _hgrn_kernel(*refs, reverse, finalize, groups):
    if finalize:
        q_ref, v_ref, lf_ref, of_ref, g_ref, gain_ref, o_ref, st_ref = refs
    else:
        q_ref, v_ref, lf_ref, o_ref, st_ref = refs
    tb = q_ref.shape[0]
    C = min(HG_CHUNK, tb)
    n_chunk = tb // C
    n_sub = C // HG_SUB

    @pl.when(groups.pos_of_tile(pl.program_id(1), tb) == 0)
    def _():
        st_ref[...] = jnp.zeros_like(st_ref)

    ri = lax.broadcasted_iota(I32, (C, C), 0)
    ci = lax.broadcasted_iota(I32, (C, C), 1)
    sub_lo = (ri >> 4) << 4
    if reverse:
        inc = ci >= ri
        loc_m = inc & (ci < sub_lo + HG_SUB)
    else:
        inc = ci <= ri
        loc_m = inc & (ci >= sub_lo)
    cum_mat = jnp.where(inc, 1.0, 0.0).astype(BF16)
    loc_mat = jnp.where(loc_m, 1.0, 0.0).astype(BF16)
    row_id = lax.broadcasted_iota(I32, (C, 1), 0)
    sub_row = lax.broadcasted_iota(I32, (HG_SUB, 1), 0)
    lane_c = lax.broadcasted_iota(I32, (HG_SUB, C), 1)
    edge = 0 if reverse else C - 1

    def cumulative(lf):
        h1, h2, h3 = _split3(lf)
        cum = _dot(cum_mat, h1) + _dot(cum_mat, h2) + _dot(cum_mat, h3)
        loc = _dot(loc_mat, h1) + _dot(loc_mat, h2) + _dot(loc_mat, h3)
        return cum, loc

    def emit(r0, o):
        if finalize:
            tot = o + of_ref[pl.ds(r0, C), :]
            ms = jnp.mean(tot * tot, axis=-1, keepdims=True)
            res = tot * lax.rsqrt(ms + RMS_EPS) * gain_ref[...] * g_ref[pl.ds(r0, C), :].astype(F32)
            o_ref[pl.ds(r0, C), :] = res.astype(o_ref.dtype)
        else:
            o_ref[pl.ds(r0, C), :] = o

    def pairwise_chunk(step, carry):
        cidx = (n_chunk - 1 - step) if reverse else step
        r0 = pl.multiple_of(cidx * C, C)
        lf = lf_ref[pl.ds(r0, C), :]
        q = q_ref[pl.ds(r0, C), :].astype(F32)
        v = v_ref[pl.ds(r0, C), :]
        cum, loc = cumulative(lf)
        kk = 1.0 - jnp.exp(lf)
        total = cum[edge:edge + 1, :]
        st = st_ref[...]
        o_inter = _dot_nt((q * jnp.exp(cum)).astype(BF16), st.astype(BF16))
        k_state = (kk * jnp.exp(total - cum)).astype(BF16)
        st_ref[...] = st * jnp.exp(total) + _dot_tn(v, k_state)
        qh_b = (q * jnp.exp(loc)).astype(BF16)
        blocks = []
        for s in range(n_sub):
            lo = s * HG_SUB
            hi = lo + HG_SUB
            has_off = (s < n_sub - 1) if reverse else (s > 0)
            if has_off:
                if reverse:
                    expo = jnp.where(row_id >= hi, cum[hi:hi + 1, :] - cum, NEG_INF)
                else:
                    expo = jnp.where(row_id < lo, cum[lo - 1:lo, :] - cum, NEG_INF)
                k_off = (kk * jnp.exp(expo)).astype(BF16)
                a_s = _dot_nt(qh_b[lo:hi], k_off)
            else:
                a_s = jnp.zeros((HG_SUB, C), F32)
            loc_s = loc[lo:hi]
            q_s = q[lo:hi]
            for j in range(HG_SUB):
                jj = lo + j
                d = loc_s - loc[jj:jj + 1, :]
                msk = (sub_row <= j) if reverse else (sub_row >= j)
                e = jnp.exp(jnp.where(msk, d, NEG_INF))
                colv = jnp.sum(q_s * e * kk[jj:jj + 1, :], axis=-1, keepdims=True)
                a_s = jnp.where(lane_c == jj, colv, a_s)
            blocks.append(a_s)
        a = jnp.concatenate(blocks, axis=0)
        emit(r0, o_inter + _dot(a.astype(BF16), v))
        return carry

    def staged_tile():
        starts = [((n_chunk - 1 - s) if reverse else s) * C for s in range(n_chunk)]
        lfs = [lf_ref[r0:r0 + C, :] for r0 in starts]
        qs = [q_ref[r0:r0 + C, :].astype(F32) for r0 in starts]
        vs = [v_ref[r0:r0 + C, :] for r0 in starts]
        cums, locs = zip(*[cumulative(lf) for lf in lfs])
        kks = [1.0 - jnp.exp(lf) for lf in lfs]
        totals = [cum[edge:edge + 1, :] for cum in cums]
        a_mats = []
        for q, kk, cum, loc in zip(qs, kks, cums, locs):
            qh_b = (q * jnp.exp(loc)).astype(BF16)
            blocks = []
            for s in range(n_sub):
                lo = s * HG_SUB
                hi = lo + HG_SUB
                if reverse:
                    edge_row = cum[hi:hi + 1, :] if s < n_sub - 1 else jnp.zeros((1, HG_DK), F32)
                    expo = jnp.where(row_id >= lo, edge_row - cum, NEG_INF)
                    tri = lane_c >= lo + sub_row
                else:
                    edge_row = cum[lo - 1:lo, :] if s > 0 else jnp.zeros((1, HG_DK), F32)
                    expo = jnp.where(row_id < hi, edge_row - cum, NEG_INF)
                    tri = lane_c <= lo + sub_row
                k_rel = (kk * jnp.exp(expo)).astype(BF16)
                blocks.append(jnp.where(tri, _dot_nt(qh_b[lo:hi], k_rel), 0.0))
            a_mats.append(jnp.concatenate(blocks, axis=0).astype(BF16))
        intras = [_dot(a, v) for a, v in zip(a_mats, vs)]
        updates = [_dot_tn(v, (kk * jnp.exp(tot - cum)).astype(BF16))
                   for v, kk, tot, cum in zip(vs, kks, totals, cums)]
        q_ins = [(q * jnp.exp(cum)).astype(BF16) for q, cum in zip(qs, cums)]
        st = st_ref[...]
        for r0, q_in, intra, tot, upd in zip(starts, q_ins, intras, totals, updates):
            emit(r0, _dot_nt(q_in, st.astype(BF16)) + intra)
            st = st * jnp.exp(tot) + upd
        st_ref[...] = st

    mild_decay = jnp.min(lf_ref[...]) >= -HG_SAFE_LOG

    @pl.when(mild_decay)
    def _():
        staged_tile()

    @pl.when(jnp.logical_not(mild_decay))
    def _():
        lax.fori_loop(0, n_chunk, pairwise_chunk, 0)


def _hgrn_pass(q, v, lf, groups, *, reverse, finalize_args=None):
    T = q.shape[0]
    tb = TOKEN_TILE
    finalize = finalize_args is not None

    def tmap(h, i):
        return (groups.reversed_tile(i, tb) if reverse else i, h)

    spec = pl.BlockSpec((tb, HG_DK), tmap)
    in_specs = [spec, spec, spec]
    args = [q, v, lf]
    if finalize:
        o_fwd, gate, gain = finalize_args
        in_specs += [spec, spec, pl.BlockSpec((1, HG_DK), lambda h, i: (0, 0))]
        args += [o_fwd, gate, gain]
    return pl.pallas_call(
        functools.partial(_hgrn_kernel, reverse=reverse, finalize=finalize, groups=groups),
        out_shape=jax.ShapeDtypeStruct((T, HG_WIDTH), BF16 if finalize else F32),
        grid=(HG_HEADS, T // tb),
        in_specs=in_specs,
        out_specs=spec,
        scratch_shapes=[pltpu.VMEM((HG_DK, HG_DK), F32)],
        compiler_params=_cparams(("parallel", "arbitrary")),
        name="hgrn_bwd" if reverse else "hgrn_fwd",
    )(*args)


def _kvproj_kernel(m_ref, w_ref, o_ref):
    o_ref[...] = _dot(m_ref[...].astype(BF16), w_ref[...]).astype(BF16)


def _kvproj(mem, w):
    R, D = mem.shape
    n = w.shape[1]
    tr = 256
    return pl.pallas_call(
        _kvproj_kernel,
        out_shape=jax.ShapeDtypeStruct((R, n), BF16),
        grid=(R // tr,),
        in_specs=[pl.BlockSpec((tr, D), lambda i: (i, 0)), pl.BlockSpec((D, n), lambda i: (0, 0))],
        out_specs=pl.BlockSpec((tr, n), lambda i: (i, 0)),
        compiler_params=_cparams(("parallel",)),
        name="kvproj",
    )(mem, w)


def _mix_xattn_kernel(x0_ref, x1_ref, att_ref, rec_ref, g0_ref, b0_ref, wo_ref, g1_ref, b1_ref, wq_ref, kv_ref,
                      wxo_ref, g2_ref, b2_ref, x2_ref, p_ref, *, tiles_group0):
    D = x0_ref.shape[1]
    dh = D // X_HEADS
    xn = _layer_norm(_group_rows(x0_ref, x1_ref, tiles_group0), g0_ref[...], b0_ref[...])
    mix = _dot(att_ref[...], wo_ref[:ATT_WIDTH, :]) + _dot(rec_ref[...], wo_ref[ATT_WIDTH:, :])
    x1 = _layer_norm(DEEPNORM_ALPHA * xn + mix, g1_ref[...], b1_ref[...])
    qx = (_dot(x1.astype(BF16), wq_ref[...]) * (dh ** -0.5)).astype(BF16)
    heads = []
    for h in range(X_HEADS):
        kh = kv_ref[0, :, h * dh:(h + 1) * dh]
        vh = kv_ref[0, :, D + h * dh:D + (h + 1) * dh]
        s = _dot_nt(qx[:, h * dh:(h + 1) * dh], kh)
        s = s - jnp.max(s, axis=-1, keepdims=True)
        p = jnp.exp(s)
        p = p / jnp.sum(p, axis=-1, keepdims=True)
        heads.append(_dot(p.astype(BF16), vh).astype(BF16))
    o = jnp.concatenate(heads, axis=-1)
    x2 = _layer_norm(DEEPNORM_ALPHA * x1 + _dot(o, wxo_ref[...]), g2_ref[...], b2_ref[...])
    x2_ref[...] = x2
    p_ref[...] = _pack_bf16_pair(x2[:, :D // 2], x2[:, D // 2:])


def _mix_xattn(x0, x1, att, rec, g0, b0, wo, g1, b1, wq, kv, wxo, g2, b2, seq_of_tile):
    D = x0.shape[1]
    T = x0.shape[0] + x1.shape[0]
    tm = TOKEN_TILE
    tiles_group0 = x0.shape[0] // tm
    n_mem = kv.shape[1]
    row = lambda i: (i, 0)
    const = lambda i: (0, 0)
    vec = pl.BlockSpec((1, D), const)
    return pl.pallas_call(
        functools.partial(_mix_xattn_kernel, tiles_group0=tiles_group0),
        out_shape=(jax.ShapeDtypeStruct((T, D), F32), jax.ShapeDtypeStruct((T, D // 2), U32)),
        grid=(T // tm,),
        in_specs=_group_specs(tm, D, tiles_group0) + [
            pl.BlockSpec((tm, ATT_WIDTH), row),
            pl.BlockSpec((tm, HG_WIDTH), row),
            vec, vec,
            pl.BlockSpec((ATT_WIDTH + HG_WIDTH, D), const),
            vec, vec,
            pl.BlockSpec((D, D), const),
            pl.BlockSpec((1, n_mem, 2 * D), lambda i: (seq_of_tile(i), 0, 0)),
            pl.BlockSpec((D, D), const),
            vec, vec,
        ],
        out_specs=(pl.BlockSpec((tm, D), row), pl.BlockSpec((tm, D // 2), row)),
        compiler_params=_cparams(("parallel",)),
        name="mix_xattn",
    )(x0, x1, att, rec, g0, b0, wo, g1, b1, wq, kv, wxo, g2, b2)


def _router_kernel(x_ref, wt_ref, b_ref, idx_ref, gate_ref, rank_ref, cnt_ref, run_ref):
    tm = x_ref.shape[0]

    @pl.when(pl.program_id(0) == 0)
    def _():
        run_ref[...] = jnp.zeros_like(run_ref)

    xh, xl, _ = _split3(x_ref[...])
    wh, wl, _ = _split3(wt_ref[...])
    logits = _dot_nt(wh, xh) + _dot_nt(wh, xl) + _dot_nt(wl, xh) + b_ref[...]
    e_id = lax.broadcasted_iota(I32, (N_EXPERTS, tm), 0)
    work = logits
    vals, idxs = [], []
    chosen = jnp.zeros((N_EXPERTS, tm), F32)
    for _ in range(TOP_K):
        m = jnp.max(work, axis=0, keepdims=True)
        idx = jnp.min(jnp.where(work == m, e_id, N_EXPERTS), axis=0, keepdims=True)
        hit = e_id == idx
        chosen = jnp.where(hit, 1.0, chosen)
        work = jnp.where(hit, NEG_INF, work)
        vals.append(m)
        idxs.append(idx)
    ex = [jnp.exp(v - vals[0]) for v in vals]
    den = ex[0] + ex[1] + ex[2] + ex[3]
    t_r = lax.broadcasted_iota(I32, (tm, tm), 0)
    t_c = lax.broadcasted_iota(I32, (tm, tm), 1)
    before = jnp.where(t_r < t_c, 1.0, 0.0).astype(BF16)
    rank_e = _dot(chosen.astype(BF16), before) + run_ref[...]
    run_ref[...] = run_ref[...] + jnp.sum(chosen, axis=1, keepdims=True)
    for k in range(TOP_K):
        idx_ref[k:k + 1, :] = idxs[k]
        gate_ref[k:k + 1, :] = ex[k] / den
        rank_ref[k:k + 1, :] = jnp.sum(jnp.where(e_id == idxs[k], rank_e, 0.0), axis=0, keepdims=True).astype(I32)
    cnt_ref[...] = run_ref[...].astype(I32)


def _router(x2, w_router_t, b_router):
    T, D = x2.shape
    tm = TOKEN_TILE
    tok = pl.BlockSpec((TOP_K, tm), lambda i: (0, i))
    return pl.pallas_call(
        _router_kernel,
        out_shape=(jax.ShapeDtypeStruct((TOP_K, T), I32), jax.ShapeDtypeStruct((TOP_K, T), F32),
                   jax.ShapeDtypeStruct((TOP_K, T), I32), jax.ShapeDtypeStruct((N_EXPERTS, 1), I32)),
        grid=(T // tm,),
        in_specs=[pl.BlockSpec((tm, D), lambda i: (i, 0)),
                  pl.BlockSpec((N_EXPERTS, D), lambda i: (0, 0)),
                  pl.BlockSpec((N_EXPERTS, 1), lambda i: (0, 0))],
        out_specs=(tok, tok, tok, pl.BlockSpec((N_EXPERTS, 1), lambda i: (0, 0))),
        scratch_shapes=[pltpu.VMEM((N_EXPERTS, 1), F32)],
        compiler_params=_cparams(("arbitrary",)),
        name="router",
    )(x2, w_router_t, b_router)


def _dispatch_kernel(dest_ref, p_ref, zero_hbm, out_hbm, sem):
    del zero_hbm
    td = DISPATCH_TILE

    def row_copy(t, k):
        d = dest_ref[0, 0, k * td + t]
        return pltpu.make_async_copy(p_ref.at[pl.ds(t, 1)], out_hbm.at[pl.ds(d, 1)], sem)

    def start(t, c):
        for k in range(TOP_K):
            row_copy(t, k).start(priority=k % DMA_PRIORITIES)
        return c

    def wait(t, c):
        for k in range(TOP_K):
            row_copy(t, k).wait()
        return c

    lax.fori_loop(0, td, start, 0)
    lax.fori_loop(0, td, wait, 0)


def _dispatch(dest_tiles, p, zeros):
    T = p.shape[0]
    td = DISPATCH_TILE
    return pl.pallas_call(
        _dispatch_kernel,
        out_shape=jax.ShapeDtypeStruct(zeros.shape, zeros.dtype),
        grid=(T // td,),
        in_specs=[pl.BlockSpec((1, 1, TOP_K * td), lambda i: (i, 0, 0), memory_space=pltpu.SMEM),
                  pl.BlockSpec((td, p.shape[1]), lambda i: (i, 0)),
                  pl.BlockSpec(memory_space=pl.ANY)],
        out_specs=pl.BlockSpec(memory_space=pl.ANY),
        scratch_shapes=[pltpu.SemaphoreType.DMA],
        input_output_aliases={2: 0},
        compiler_params=_cparams(("arbitrary",)),
        name="dispatch",
    )(dest_tiles, p, zeros)


def _expert_kernel(blk_e_ref, nvalid_ref, p_ref, wu_ref, bu_ref, wd_ref, bd_ref, y_ref, wu_b, wd_b):
    i = pl.program_id(0)
    dff = wd_ref.shape[1]
    half = p_ref.shape[1]
    live = i < nvalid_ref[0]
    new_expert = (i == 0) | (blk_e_ref[i] != blk_e_ref[jnp.maximum(i - 1, 0)])

    @pl.when(live & new_expert)
    def _():
        wu_b[...] = wu_ref[0].astype(BF16)
        wd_b[...] = wd_ref[0].astype(BF16)

    @pl.when(live)
    def _():
        lo, hi = _unpack_bf16_pair(p_ref[...])
        lo = lo.astype(BF16)
        hi = hi.astype(BF16)

        def up(c0):
            return (_dot(lo, wu_b[:half, c0:c0 + dff]) + _dot(hi, wu_b[half:, c0:c0 + dff])
                    + bu_ref[0, :, c0:c0 + dff])

        glu = jnp.minimum(up(0), SWIGLU_LIMIT)
        lin = jnp.clip(up(dff), -SWIGLU_LIMIT, SWIGLU_LIMIT)
        a = glu * _sigmoid(SWIGLU_ALPHA * glu) * (lin + 1.0)
        y = _dot(a.astype(BF16), wd_b[...]) + bd_ref[0]
        y_ref[...] = _pack_bf16_pair(y[:, :half], y[:, half:])

    @pl.when(i >= nvalid_ref[0])
    def _():
        y_ref[...] = jnp.zeros_like(y_ref)


def _experts(blk_expert, n_valid, ps, wu, bu, wd, bd):
    R, half = ps.shape
    te = EXPERT_TILE
    E, D, two_f = wu.shape
    dff = wd.shape[1]
    grid_spec = pltpu.PrefetchScalarGridSpec(
        num_scalar_prefetch=2,
        grid=(R // te,),
        in_specs=[
            pl.BlockSpec((te, half), lambda i, be, nv: (i, 0)),
            pl.BlockSpec((1, D, two_f), lambda i, be, nv: (be[i], 0, 0)),
            pl.BlockSpec((1, 1, two_f), lambda i, be, nv: (be[i], 0, 0)),
            pl.BlockSpec((1, dff, D), lambda i, be, nv: (be[i], 0, 0)),
            pl.BlockSpec((1, 1, D), lambda i, be, nv: (be[i], 0, 0)),
        ],
        out_specs=pl.BlockSpec((te, half), lambda i, be, nv: (i, 0)),
        scratch_shapes=[pltpu.VMEM((D, two_f), BF16), pltpu.VMEM((dff, D), BF16)],
    )
    return pl.pallas_call(
        _expert_kernel,
        out_shape=jax.ShapeDtypeStruct((R, half), U32),
        grid_spec=grid_spec,
        compiler_params=_cparams(("arbitrary",)),
        name="experts",
    )(blk_expert, n_valid, ps, wu, bu, wd, bd)


def _combine_kernel(dest_ref, dest_next_ref, ys_hbm, x2_ref, gate_ref, g_ref, b_ref, o_ref, buf, sems):
    tc = x2_ref.shape[0]
    half = buf.shape[3]
    i = pl.program_id(0)
    slot = i % 2

    def row_copy(idx_ref, s, t, k):
        d = idx_ref[0, 0, k * tc + t]
        return pltpu.make_async_copy(ys_hbm.at[pl.ds(d, 1)], buf.at[s, k, pl.ds(t, 1)], sems.at[s])

    def start_tile(idx_ref, s):
        def body(t, c):
            for k in range(TOP_K):
                row_copy(idx_ref, s, t, k).start(priority=k % DMA_PRIORITIES)
            return c
        lax.fori_loop(0, tc, body, 0)

    @pl.when(i == 0)
    def _():
        start_tile(dest_ref, 0)

    @pl.when(i + 1 < pl.num_programs(0))
    def _():
        start_tile(dest_next_ref, 1 - slot)

    def wait(t, c):
        for k in range(TOP_K):
            row_copy(dest_ref, slot, t, k).wait()
        return c

    lax.fori_loop(0, tc, wait, 0)

    gates = gate_ref[...]
    acc_lo = jnp.zeros((tc, half), F32)
    acc_hi = jnp.zeros((tc, half), F32)
    for k in range(TOP_K):
        lo, hi = _unpack_bf16_pair(buf[slot, k])
        gk = gates[:, k:k + 1]
        acc_lo = acc_lo + gk * lo
        acc_hi = acc_hi + gk * hi
    ffn = jnp.concatenate([acc_lo, acc_hi], axis=-1)
    o_ref[...] = _layer_norm(DEEPNORM_ALPHA * x2_ref[...] + ffn, g_ref[...], b_ref[...])


def _combine(dest_tiles, ys, x2, gates, g, b, *, tile0, n_tiles):
    D = x2.shape[1]
    tc = COMBINE_TILE
    return pl.pallas_call(
        _combine_kernel,
        out_shape=jax.ShapeDtypeStruct((n_tiles * tc, D), F32),
        grid=(n_tiles,),
        in_specs=[pl.BlockSpec((1, 1, TOP_K * tc), lambda i: (tile0 + i, 0, 0), memory_space=pltpu.SMEM),
                  pl.BlockSpec((1, 1, TOP_K * tc), lambda i: (tile0 + jnp.minimum(i + 1, n_tiles - 1), 0, 0),
                               memory_space=pltpu.SMEM),
                  pl.BlockSpec(memory_space=pl.ANY),
                  pl.BlockSpec((tc, D), lambda i: (tile0 + i, 0)),
                  pl.BlockSpec((tc, TOP_K), lambda i: (tile0 + i, 0)),
                  pl.BlockSpec((1, D), lambda i: (0, 0)),
                  pl.BlockSpec((1, D), lambda i: (0, 0))],
        out_specs=pl.BlockSpec((tc, D), lambda i: (i, 0)),
        scratch_shapes=[pltpu.VMEM((2, TOP_K, tc, D // 2), U32), pltpu.SemaphoreType.DMA((2,))],
        compiler_params=_cparams(("arbitrary",)),
        name="combine",
    )(dest_tiles, dest_tiles, ys, x2, gates, g, b)


def _rope_tables(seq_len):
    rows = seq_len // GRID_W
    row = jnp.repeat(jnp.arange(rows, dtype=F32), GRID_W)
    colp = (jnp.arange(seq_len) % GRID_W).astype(F32)
    axis_dim = ATT_HEAD_DIM // 2
    inv_freq = ROPE_THETA ** (-jnp.arange(0, axis_dim, 2, dtype=F32) / axis_dim)
    ang = jnp.concatenate([row[:, None] * inv_freq, colp[:, None] * inv_freq], axis=-1)
    cos, sin = jnp.cos(ang), jnp.sin(ang)
    cos_h = jnp.concatenate([cos, cos], axis=-1)
    sin_h = jnp.concatenate([-sin, sin], axis=-1)
    return jnp.tile(cos_h, (1, LANES // ATT_HEAD_DIM)), jnp.tile(sin_h, (1, LANES // ATT_HEAD_DIM))


def _tiles_of(idx_t, tile):
    K, T = idx_t.shape
    return idx_t.reshape(K, T // tile, tile).transpose(1, 0, 2).reshape(T // tile, 1, K * tile)


def kernel(x_prompt, x_sample, mem_prompt, mem_sample, ln_in_g, ln_in_b, w_in, q_norm_g, k_norm_g, hg_lb_fwd, hg_lb_bwd, hg_norm_g, w_out, ln1_g, ln1_b, w_xq, w_xkv, w_xo, ln2_g, ln2_b, w_router, b_router, w_up, b_up, w_down, b_down, ln3_g, ln3_b):
    Bp, Sp, D = x_prompt.shape
    Bs, Ss, _ = x_sample.shape
    Tp, Ts = Bp * Sp, Bs * Ss
    T = Tp + Ts
    tm = TOKEN_TILE
    assert Sp % tm == 0 and Ss % Sp == 0
    n_mem = mem_prompt.shape[1]
    layer = 0

    x0, x1 = x_prompt.reshape(Tp, D), x_sample.reshape(Ts, D)
    mem = jnp.concatenate([mem_prompt.reshape(Bp * n_mem, D), mem_sample.reshape(Bs * n_mem, D)], axis=0)

    def deinterleave(a):
        lead = a.shape[:-1]
        a = a.reshape(*lead, -1, ATT_HEAD_DIM // 2, 2)
        return jnp.swapaxes(a, -1, -2).reshape(*lead, -1)

    n_qk = ATT_WIDTH + ATT_KV_WIDTH
    w_in_l = w_in[layer]
    w_in_b = jnp.concatenate([deinterleave(w_in_l[:, :n_qk]), w_in_l[:, n_qk:]], axis=1).astype(BF16)
    pair = LANES // ATT_HEAD_DIM
    qg = jnp.tile(deinterleave(q_norm_g[layer]), pair).reshape(1, LANES)
    kg = jnp.tile(deinterleave(k_norm_g[layer]), pair).reshape(1, LANES)

    def lower_bound(logits):
        sm = jax.nn.softmax(logits.astype(F32), axis=0)
        return (jnp.cumsum(sm, axis=0)[layer + 1] - sm[0]).reshape(1, HG_WIDTH)

    lbf, lbb = lower_bound(hg_lb_fwd), lower_bound(hg_lb_bwd)
    cos_t, sin_t = _rope_tables(max(Sp, Ss))
    vec = lambda a: a.reshape(1, -1)

    groups = _Groups(Bp, Sp, Bs, Ss)
    pos_block = lambda i: groups.pos_of_tile(i, tm)
    seq_of_tile = lambda i: groups.seq_of_tile(i, tm)

    q, k, v, hq, hi, hg, lff, lfb = _inproj(x0, x1, vec(ln_in_g), vec(ln_in_b), w_in_b, cos_t, sin_t, qg, kg,
                                           lbf, lbb, pos_block)

    att = _attention(q, k, v, groups)
    o_fwd = _hgrn_pass(hq, hi, lff, groups, reverse=False)
    gain = hg_norm_g[layer].reshape(1, HG_DK)
    rec = _hgrn_pass(hq, hi, lfb, groups, reverse=True, finalize_args=(o_fwd, hg, gain))

    kv = _kvproj(mem, w_xkv[layer].astype(BF16)).reshape(Bp + Bs, n_mem, 2 * D)
    x2, packed = _mix_xattn(x0, x1, att, rec, vec(ln_in_g), vec(ln_in_b), w_out[layer].astype(BF16),
                            vec(ln1_g[layer]), vec(ln1_b[layer]), w_xq[layer].astype(BF16), kv,
                            w_xo[layer].astype(BF16), vec(ln2_g[layer]), vec(ln2_b[layer]), seq_of_tile)

    idx_t, gate_t, rank_t, counts = _router(x2, w_router[layer].T, b_router[layer].reshape(N_EXPERTS, 1))

    te = EXPERT_TILE
    counts = counts.reshape(N_EXPERTS)
    padded = (counts + te - 1) // te * te
    pad_end = jnp.cumsum(padded)
    pad_start = pad_end - padded
    n_blk = (T * TOP_K + N_EXPERTS * (te - 1) + te - 1) // te
    e_ids = jnp.arange(N_EXPERTS, dtype=I32)
    start_of = jnp.sum(jnp.where(idx_t[None] == e_ids[:, None, None], pad_start[:, None, None], 0), axis=0)
    dest_t = start_of + rank_t
    blk_row0 = jnp.arange(n_blk, dtype=I32) * te
    blk_expert = jnp.minimum(jnp.sum((pad_end[None, :] <= blk_row0[:, None]).astype(I32), axis=1), N_EXPERTS - 1)
    n_valid = (pad_end[-1] // te).astype(I32).reshape(1)

    sorted_rows = _dispatch(_tiles_of(dest_t, DISPATCH_TILE), packed, jnp.zeros((n_blk * te, D // 2), U32))
    ys = _experts(blk_expert, n_valid, sorted_rows, w_up[layer], b_up[layer].reshape(N_EXPERTS, 1, -1),
                  w_down[layer], b_down[layer].reshape(N_EXPERTS, 1, -1))

    dest_c = _tiles_of(dest_t, COMBINE_TILE)
    gates = gate_t.T
    tc = COMBINE_TILE
    y_p = _combine(dest_c, ys, x2, gates, vec(ln3_g[layer]), vec(ln3_b[layer]), tile0=0, n_tiles=Tp // tc)
    y_s = _combine(dest_c, ys, x2, gates, vec(ln3_g[layer]), vec(ln3_b[layer]), tile0=Tp // tc, n_tiles=Ts // tc)
    return y_p.reshape(Bp, Sp, D), y_s.reshape(Bs, Ss, D)
```

```python
import functools

import jax
import jax.numpy as jnp
import numpy as np
from jax import lax
from jax.experimental import pallas as pl
from jax.experimental.pallas import tpu as pltpu

F32 = jnp.float32
BF16 = jnp.bfloat16
I32 = jnp.int32
U32 = jnp.uint32

GRID_W = 64
ATT_HEADS = 8
ATT_KV_HEADS = 2
ATT_HEAD_DIM = 64
ATT_WIDTH = ATT_HEADS * ATT_HEAD_DIM
ATT_KV_WIDTH = ATT_KV_HEADS * ATT_HEAD_DIM
ROPE_THETA = 10000.0
HG_HEADS = 4
HG_DK = 128
HG_WIDTH = HG_HEADS * HG_DK
X_HEADS = 4
N_EXPERTS = 32
TOP_K = 4
SWIGLU_LIMIT = 7.0
SWIGLU_ALPHA = 1.702
LN_EPS = 1e-5
RMS_EPS = 1e-6
DEPTH = 1
DEEPNORM_ALPHA = (2.0 * DEPTH) ** 0.25

LANES = 128
VMEM_LIMIT_BYTES = 56 * 1024 * 1024
DMA_PRIORITIES = 2

TOKEN_TILE = 512
ATT_Q_TILE = 256
ATT_K_TILE = 4096
HG_CHUNK = 64
HG_SUB = 16

ATT_SAFE_SCORE = 60.0
ATT_NORM_SLACK = 1.01
HG_SAFE_LOG = 5.0
EXPERT_TILE = 512
COMBINE_TILE = 256
DISPATCH_TILE = 512
NEG_INF = float("-inf")


def _cparams(sem):
    return pltpu.CompilerParams(dimension_semantics=sem, vmem_limit_bytes=VMEM_LIMIT_BYTES)


def _layer_norm(x, g, b):
    mu = jnp.mean(x, axis=-1, keepdims=True)
    xc = x - mu
    var = jnp.mean(xc * xc, axis=-1, keepdims=True)
    return xc * lax.rsqrt(var + LN_EPS) * g + b


def _sigmoid(x):
    return 1.0 / (1.0 + jnp.exp(-x))


def _dot(a, b):
    return jnp.dot(a, b, preferred_element_type=F32)


def _dot_nt(a, b):
    return lax.dot_general(a, b, (((1,), (1,)), ((), ())), preferred_element_type=F32)


def _dot_tn(a, b):
    return lax.dot_general(a, b, (((0,), (0,)), ((), ())), preferred_element_type=F32)


def _split3(x):
    h1 = x.astype(BF16)
    r1 = x - h1.astype(F32)
    h2 = r1.astype(BF16)
    h3 = (r1 - h2.astype(F32)).astype(BF16)
    return h1, h2, h3


def _pack_bf16_pair(lo, hi):
    lo_b = pltpu.bitcast(lo.astype(BF16).astype(F32), U32)
    hi_b = pltpu.bitcast(hi.astype(BF16).astype(F32), U32)
    return (hi_b & jnp.uint32(0xFFFF0000)) | (lo_b >> 16)


def _unpack_bf16_pair(p):
    lo = pltpu.bitcast(p << 16, F32)
    hi = pltpu.bitcast(p & jnp.uint32(0xFFFF0000), F32)
    return lo, hi


SLAB = 4


def _store_slabs(ref, value):
    rows = value.shape[0]
    for s in range(SLAB):
        ref[pl.ds(s, rows, stride=SLAB), :] = value[:, s * LANES:(s + 1) * LANES]


def _load_slabs(ref):
    rows = ref.shape[0] // SLAB
    return jnp.concatenate([ref[pl.ds(s, rows, stride=SLAB), :] for s in range(SLAB)], axis=-1)


def _group_rows(x0_ref, x1_ref, tiles_group0):
    return jnp.where(pl.program_id(0) < tiles_group0, x0_ref[...], x1_ref[...])


def _group_specs(tile, width, tiles_group0):
    return [pl.BlockSpec((tile, width), lambda i: (jnp.minimum(i, tiles_group0 - 1), 0)),
            pl.BlockSpec((tile, width), lambda i: (jnp.maximum(i - tiles_group0, 0), 0))]


def _inproj_kernel(x0_ref, x1_ref, g_ref, b_ref, w_ref, cos_ref, sin_ref, qg_ref, kg_ref, lbf_ref, lbb_ref,
                   q_out, k_out, v_out, hq_out, hi_out, hg_out, lff_out, lfb_out, *, tiles_group0):
    xn = _layer_norm(_group_rows(x0_ref, x1_ref, tiles_group0), g_ref[...], b_ref[...])
    xb = xn.astype(BF16)
    cos = cos_ref[...]
    sin = sin_ref[...]
    tm = xb.shape[0]

    lane = lax.broadcasted_iota(I32, (tm, LANES), 1)
    first_half = (lane & (ATT_HEAD_DIM - 1)) < (ATT_HEAD_DIM // 2)
    r = lax.broadcasted_iota(I32, (LANES, LANES), 0)
    c = lax.broadcasted_iota(I32, (LANES, LANES), 1)
    head_ones = jnp.where((r >> 6) == (c >> 6), 1.0, 0.0).astype(BF16)

    def normed_rope(u, gain, scale):
        sq = u * u
        s_hi = sq.astype(BF16)
        s_lo = (sq - s_hi.astype(F32)).astype(BF16)
        ms = (_dot(s_hi, head_ones) + _dot(s_lo, head_ones)) * (1.0 / ATT_HEAD_DIM)
        un = u * lax.rsqrt(ms + RMS_EPS) * gain
        rot = jnp.where(first_half, pltpu.roll(un, LANES - 32, 1), pltpu.roll(un, 32, 1))
        return (un * cos + rot * sin) * scale

    col = 0
    for j in range(ATT_WIDTH // (2 * LANES)):
        u = _dot(xb, w_ref[:, col:col + 2 * LANES])
        for h in range(2):
            c0 = (2 * j + h) * LANES
            q_out[:, c0:c0 + LANES] = normed_rope(u[:, h * LANES:(h + 1) * LANES], qg_ref[...],
                                                  ATT_HEAD_DIM ** -0.5).astype(BF16)
        col += 2 * LANES
    u = _dot(xb, w_ref[:, col:col + 2 * LANES])
    k_out[...] = normed_rope(u[:, :LANES], kg_ref[...], 1.0).astype(BF16)
    v_out[:, :LANES] = u[:, LANES:].astype(BF16)
    v_out[:, LANES:] = jnp.ones((tm, LANES), BF16)
    col += 2 * LANES

    u = _dot(xb, w_ref[:, col:col + HG_WIDTH])
    hq_out[...] = (u * _sigmoid(u)).astype(BF16)
    col += HG_WIDTH
    for lb_ref, lf_out in ((lbf_ref, lff_out), (lbb_ref, lfb_out)):
        u = _dot(xb, w_ref[:, col:col + HG_WIDTH])
        lb = lb_ref[...]
        lf_out[...] = jnp.log(lb + (1.0 - lb) * _sigmoid(u))
        col += HG_WIDTH
    hi_out[...] = _dot(xb, w_ref[:, col:col + HG_WIDTH]).astype(BF16)
    col += HG_WIDTH
    u = _dot(xb, w_ref[:, col:col + HG_WIDTH])
    hg_out[...] = (u * _sigmoid(u)).astype(BF16)


def _inproj(x0, x1, g, b, w, cos_t, sin_t, qg, kg, lbf, lbb, pos_block):
    D = x0.shape[1]
    T = x0.shape[0] + x1.shape[0]
    tm = TOKEN_TILE
    tiles_group0 = x0.shape[0] // tm
    n_in = w.shape[1]
    row = lambda i: (i, 0)
    const = lambda i: (0, 0)
    out_shape = (
        jax.ShapeDtypeStruct((T, ATT_WIDTH), BF16),
        jax.ShapeDtypeStruct((T, ATT_KV_WIDTH), BF16),
        jax.ShapeDtypeStruct((T, 2 * ATT_KV_WIDTH), BF16),
        jax.ShapeDtypeStruct((T, HG_WIDTH), BF16),
        jax.ShapeDtypeStruct((T, HG_WIDTH), BF16),
        jax.ShapeDtypeStruct((T, HG_WIDTH), BF16),
        jax.ShapeDtypeStruct((T, HG_WIDTH), F32),
        jax.ShapeDtypeStruct((T, HG_WIDTH), F32),
    )
    out_specs = (
        pl.BlockSpec((tm, ATT_WIDTH), row),
        pl.BlockSpec((tm, ATT_KV_WIDTH), row),
        pl.BlockSpec((tm, 2 * ATT_KV_WIDTH), row),
        pl.BlockSpec((tm, HG_WIDTH), row),
        pl.BlockSpec((tm, HG_WIDTH), row),
        pl.BlockSpec((tm, HG_WIDTH), row),
        pl.BlockSpec((tm, HG_WIDTH), row),
        pl.BlockSpec((tm, HG_WIDTH), row),
    )
    return pl.pallas_call(
        functools.partial(_inproj_kernel, tiles_group0=tiles_group0),
        out_shape=out_shape,
        grid=(T // tm,),
        in_specs=_group_specs(tm, D, tiles_group0) + [
            pl.BlockSpec((1, D), const),
            pl.BlockSpec((1, D), const),
            pl.BlockSpec((D, n_in), const),
            pl.BlockSpec((tm, LANES), lambda i: (pos_block(i), 0)),
            pl.BlockSpec((tm, LANES), lambda i: (pos_block(i), 0)),
            pl.BlockSpec((1, LANES), const),
            pl.BlockSpec((1, LANES), const),
            pl.BlockSpec((1, HG_WIDTH), const),
            pl.BlockSpec((1, HG_WIDTH), const),
        ],
        out_specs=out_specs,
        compiler_params=_cparams(("parallel",)),
        name="inproj",
    )(x0, x1, g, b, w, cos_t, sin_t, qg, kg, lbf, lbb)


class _Groups:
    def __init__(self, n0, s0, n1, s1):
        self.n0, self.s0, self.n1, self.s1 = n0, s0, n1, s1
        self.t0 = n0 * s0
        self.total = self.t0 + n1 * s1

    def split(self, i, tile):
        nt0 = self.t0 // tile
        per0, per1 = self.s0 // tile, self.s1 // tile
        in1 = i >= nt0
        j1 = jnp.maximum(i - nt0, 0)
        seq = jnp.where(in1, self.n0 + j1 // per1, i // per0)
        pos = jnp.where(in1, j1 % per1, i % per0)
        return in1, seq, pos, jnp.where(in1, per1, per0)

    def seq_of_tile(self, i, tile):
        return self.split(i, tile)[1]

    def pos_of_tile(self, i, tile):
        return self.split(i, tile)[2]

    def reversed_tile(self, i, tile):
        _, _, pos, per = self.split(i, tile)
        return i - pos + (per - 1 - pos)


def _attention_kernel(*refs, n_kv_blocks, groups):
    q_ref = refs[0]
    k_refs = refs[1:1 + n_kv_blocks]
    v_refs = refs[1 + n_kv_blocks:1 + 2 * n_kv_blocks]
    o_ref, kmax_ref, out_ref = refs[1 + 2 * n_kv_blocks:]
    tq = q_ref.shape[0]
    kv_rows = k_refs[0].shape[0]
    tk = min(ATT_K_TILE, kv_rows)
    n_k = kv_rows // tk
    group = ATT_HEADS // ATT_KV_HEADS
    rows = group * tq
    lane = lax.broadcasted_iota(I32, (tq, LANES), 1)
    low_half = lane < ATT_HEAD_DIM
    in_group1, _, pos, _ = groups.split(pl.program_id(0), tq)
    extra_trips = jnp.where(in_group1, n_k, 0)

    @pl.when(pos == 0)
    def _():
        r = lax.broadcasted_iota(I32, (LANES, LANES), 0)
        c = lax.broadcasted_iota(I32, (LANES, LANES), 1)
        head_ones = jnp.where((r >> 6) == (c >> 6), 1.0, 0.0).astype(BF16)
        best = jnp.zeros((1, LANES), F32)
        for k_ref in k_refs:
            kf = k_ref[...].astype(F32)
            best = jnp.maximum(best, jnp.max(_dot((kf * kf).astype(BF16), head_ones), axis=0, keepdims=True))
        kmax_ref[...] = best * ATT_NORM_SLACK

    for kvh in range(ATT_KV_HEADS):
        keep = low_half if kvh == 0 else jnp.logical_not(low_half)
        parts = []
        for g in range(group):
            c0 = kvh * 2 * LANES + (g // 2) * LANES
            pair = q_ref[:, c0:c0 + LANES].astype(F32)
            if g % 2 != kvh:
                pair = pltpu.roll(pair, ATT_HEAD_DIM, 1)
            parts.append(jnp.where(keep, pair, 0.0).astype(BF16))
        qp = jnp.concatenate(parts, axis=0)

        qf = qp.astype(F32)
        q_sq = jnp.sum(jnp.max(qf * qf, axis=0, keepdims=True))
        k_sq = jnp.max(jnp.where(keep[:1], kmax_ref[...], 0.0))
        unshifted_ok = q_sq * k_sq <= ATT_SAFE_SCORE * ATT_SAFE_SCORE

        def run(make_body, carry):
            carry = lax.fori_loop(0, n_k, make_body(k_refs[0], v_refs[0]), carry)
            for kb in range(1, n_kv_blocks):
                carry = lax.fori_loop(0, extra_trips, make_body(k_refs[kb], v_refs[kb]), carry)
            return carry

        @pl.when(unshifted_ok)
        def _():
            def make_body(k_ref, v_ref):
                def body(kc, acc):
                    r0 = pl.multiple_of(kc * tk, tk)
                    p = jnp.exp(_dot_nt(qp, k_ref[pl.ds(r0, tk), :]))
                    return acc + _dot(p.astype(BF16), v_ref[pl.ds(r0, tk), :])
                return body

            acc = run(make_body, jnp.zeros((rows, 2 * LANES), F32))
            out_ref[...] = acc[:, :LANES] / acc[:, LANES:]

        @pl.when(jnp.logical_not(unshifted_ok))
        def _():
            def make_body(k_ref, v_ref):
                def body(kc, carry):
                    m, l, acc = carry
                    r0 = pl.multiple_of(kc * tk, tk)
                    s = _dot_nt(qp, k_ref[pl.ds(r0, tk), :])
                    m_new = jnp.maximum(m, jnp.max(s, axis=-1, keepdims=True))
                    alpha = jnp.exp(m - m_new)
                    p = jnp.exp(s - m_new)
                    l_new = alpha * l + jnp.sum(p, axis=-1, keepdims=True)
                    acc_new = alpha * acc + _dot(p.astype(BF16), v_ref[pl.ds(r0, tk), :LANES])
                    return m_new, l_new, acc_new
                return body

            init = (jnp.full((rows, 1), NEG_INF, F32), jnp.zeros((rows, 1), F32), jnp.zeros((rows, LANES), F32))
            m, l, acc = run(make_body, init)
            out_ref[...] = acc / l

        out = out_ref[...]
        for j in range(group // 2):
            a = out[(2 * j) * tq:(2 * j + 1) * tq]
            b = out[(2 * j + 1) * tq:(2 * j + 2) * tq]
            if kvh == 1:
                a = pltpu.roll(a, ATT_HEAD_DIM, 1)
            else:
                b = pltpu.roll(b, ATT_HEAD_DIM, 1)
            c0 = kvh * 2 * LANES + j * LANES
            o_ref[:, c0:c0 + LANES] = jnp.where(low_half, a, b).astype(BF16)


def _attention(q, k, v, groups):
    T = q.shape[0]
    tq = ATT_Q_TILE
    kv_rows = groups.s0
    n_kv_blocks = groups.s1 // groups.s0

    def kv_map(j):
        def index_map(i):
            in1, seq, _, _ = groups.split(i, tq)
            first = jnp.where(in1, groups.t0 // kv_rows + (seq - groups.n0) * n_kv_blocks, seq)
            return (first + jnp.where(in1, j, 0), 0)
        return index_map

    k_specs = [pl.BlockSpec((kv_rows, ATT_KV_WIDTH), kv_map(j)) for j in range(n_kv_blocks)]
    v_specs = [pl.BlockSpec((kv_rows, 2 * ATT_KV_WIDTH), kv_map(j)) for j in range(n_kv_blocks)]
    return pl.pallas_call(
        functools.partial(_attention_kernel, n_kv_blocks=n_kv_blocks, groups=groups),
        out_shape=jax.ShapeDtypeStruct((T, ATT_WIDTH), BF16),
        grid=(T // tq,),
        in_specs=[pl.BlockSpec((tq, ATT_WIDTH), lambda i: (i, 0))] + k_specs + v_specs,
        out_specs=pl.BlockSpec((tq, ATT_WIDTH), lambda i: (i, 0)),
        scratch_shapes=[pltpu.VMEM((1, LANES), F32),
                        pltpu.VMEM((ATT_HEADS // ATT_KV_HEADS * tq, LANES), F32)],
        compiler_params=_cparams(("arbitrary",)),
        name="attention",
    )(q, *([k] * n_kv_blocks), *([v] * n_kv_blocks))


def _hgrn_kernel(*refs, reverse, finalize, groups):
    if finalize:
        q_ref, v_ref, lf_ref, of_ref, g_ref, gain_ref, o_ref, st_ref = refs
    else:
        q_ref, v_ref, lf_ref, o_ref, st_ref = refs
    tb = q_ref.shape[0]
    C = min(HG_CHUNK, tb)
    n_chunk = tb // C
    n_sub = C // HG_SUB

    @pl.when(groups.pos_of_tile(pl.program_id(1), tb) == 0)
    def _():
        st_ref[...] = jnp.zeros_like(st_ref)

    ri = lax.broadcasted_iota(I32, (C, C), 0)
    ci = lax.broadcasted_iota(I32, (C, C), 1)
    sub_lo = (ri >> 4) << 4
    if reverse:
        inc = ci >= ri
        loc_m = inc & (ci < sub_lo + HG_SUB)
    else:
        inc = ci <= ri
        loc_m = inc & (ci >= sub_lo)
    cum_mat = jnp.where(inc, 1.0, 0.0).astype(BF16)
    loc_mat = jnp.where(loc_m, 1.0, 0.0).astype(BF16)
    row_id = lax.broadcasted_iota(I32, (C, 1), 0)
    sub_row = lax.broadcasted_iota(I32, (HG_SUB, 1), 0)
    lane_c = lax.broadcasted_iota(I32, (HG_SUB, C), 1)
    edge = 0 if reverse else C - 1

    def cumulative(lf):
        h1, h2, h3 = _split3(lf)
        cum = _dot(cum_mat, h1) + _dot(cum_mat, h2) + _dot(cum_mat, h3)
        loc = _dot(loc_mat, h1) + _dot(loc_mat, h2) + _dot(loc_mat, h3)
        return cum, loc

    def emit(r0, o):
        if finalize:
            tot = o + of_ref[pl.ds(r0, C), :]
            ms = jnp.mean(tot * tot, axis=-1, keepdims=True)
            res = tot * lax.rsqrt(ms + RMS_EPS) * gain_ref[...] * g_ref[pl.ds(r0, C), :].astype(F32)
            o_ref[pl.ds(r0, C), :] = res.astype(o_ref.dtype)
        else:
            o_ref[pl.ds(r0, C), :] = o

    def pairwise_chunk(step, carry):
        cidx = (n_chunk - 1 - step) if reverse else step
        r0 = pl.multiple_of(cidx * C, C)
        lf = lf_ref[pl.ds(r0, C), :]
        q = q_ref[pl.ds(r0, C), :].astype(F32)
        v = v_ref[pl.ds(r0, C), :]
        cum, loc = cumulative(lf)
        kk = 1.0 - jnp.exp(lf)
        total = cum[edge:edge + 1, :]
        st = st_ref[...]
        o_inter = _dot_nt((q * jnp.exp(cum)).astype(BF16), st.astype(BF16))
        k_state = (kk * jnp.exp(total - cum)).astype(BF16)
        st_ref[...] = st * jnp.exp(total) + _dot_tn(v, k_state)
        qh_b = (q * jnp.exp(loc)).astype(BF16)
        blocks = []
        for s in range(n_sub):
            lo = s * HG_SUB
            hi = lo + HG_SUB
            has_off = (s < n_sub - 1) if reverse else (s > 0)
            if has_off:
                if reverse:
                    expo = jnp.where(row_id >= hi, cum[hi:hi + 1, :] - cum, NEG_INF)
                else:
                    expo = jnp.where(row_id < lo, cum[lo - 1:lo, :] - cum, NEG_INF)
                k_off = (kk * jnp.exp(expo)).astype(BF16)
                a_s = _dot_nt(qh_b[lo:hi], k_off)
            else:
                a_s = jnp.zeros((HG_SUB, C), F32)
            loc_s = loc[lo:hi]
            q_s = q[lo:hi]
            for j in range(HG_SUB):
                jj = lo + j
                d = loc_s - loc[jj:jj + 1, :]
                msk = (sub_row <= j) if reverse else (sub_row >= j)
                e = jnp.exp(jnp.where(msk, d, NEG_INF))
                colv = jnp.sum(q_s * e * kk[jj:jj + 1, :], axis=-1, keepdims=True)
                a_s = jnp.where(lane_c == jj, colv, a_s)
            blocks.append(a_s)
        a = jnp.concatenate(blocks, axis=0)
        emit(r0, o_inter + _dot(a.astype(BF16), v))
        return carry

    def staged_tile():
        starts = [((n_chunk - 1 - s) if reverse else s) * C for s in range(n_chunk)]
        lfs = [lf_ref[r0:r0 + C, :] for r0 in starts]
        qs = [q_ref[r0:r0 + C, :].astype(F32) for r0 in starts]
        vs = [v_ref[r0:r0 + C, :] for r0 in starts]
        cums, locs = zip(*[cumulative(lf) for lf in lfs])
        kks = [1.0 - jnp.exp(lf) for lf in lfs]
        totals = [cum[edge:edge + 1, :] for cum in cums]
        a_mats = []
        for q, kk, cum, loc in zip(qs, kks, cums, locs):
            qh_b = (q * jnp.exp(loc)).astype(BF16)
            blocks = []
            for s in range(n_sub):
                lo = s * HG_SUB
                hi = lo + HG_SUB
                if reverse:
                    edge_row = cum[hi:hi + 1, :] if s < n_sub - 1 else jnp.zeros((1, HG_DK), F32)
                    expo = jnp.where(row_id >= lo, edge_row - cum, NEG_INF)
                    tri = lane_c >= lo + sub_row
                else:
                    edge_row = cum[lo - 1:lo, :] if s > 0 else jnp.zeros((1, HG_DK), F32)
                    expo = jnp.where(row_id < hi, edge_row - cum, NEG_INF)
                    tri = lane_c <= lo + sub_row
                k_rel = (kk * jnp.exp(expo)).astype(BF16)
                blocks.append(jnp.where(tri, _dot_nt(qh_b[lo:hi], k_rel), 0.0))
            a_mats.append(jnp.concatenate(blocks, axis=0).astype(BF16))
        intras = [_dot(a, v) for a, v in zip(a_mats, vs)]
        updates = [_dot_tn(v, (kk * jnp.exp(tot - cum)).astype(BF16))
                   for v, kk, tot, cum in zip(vs, kks, totals, cums)]
        q_ins = [(q * jnp.exp(cum)).astype(BF16) for q, cum in zip(qs, cums)]
        st = st_ref[...]
        for r0, q_in, intra, tot, upd in zip(starts, q_ins, intras, totals, updates):
            emit(r0, _dot_nt(q_in, st.astype(BF16)) + intra)
            st = st * jnp.exp(tot) + upd
        st_ref[...] = st

    mild_decay = jnp.min(lf_ref[...]) >= -HG_SAFE_LOG

    @pl.when(mild_decay)
    def _():
        staged_tile()

    @pl.when(jnp.logical_not(mild_decay))
    def _():
        lax.fori_loop(0, n_chunk, pairwise_chunk, 0)


def _hgrn_pass(q, v, lf, groups, *, reverse, finalize_args=None):
    T = q.shape[0]
    tb = TOKEN_TILE
    finalize = finalize_args is not None

    def tmap(h, i):
        return (groups.reversed_tile(i, tb) if reverse else i, h)

    spec = pl.BlockSpec((tb, HG_DK), tmap)
    in_specs = [spec, spec, spec]
    args = [q, v, lf]
    if finalize:
        o_fwd, gate, gain = finalize_args
        in_specs += [spec, spec, pl.BlockSpec((1, HG_DK), lambda h, i: (0, 0))]
        args += [o_fwd, gate, gain]
    return pl.pallas_call(
        functools.partial(_hgrn_kernel, reverse=reverse, finalize=finalize, groups=groups),
        out_shape=jax.ShapeDtypeStruct((T, HG_WIDTH), BF16 if finalize else F32),
        grid=(HG_HEADS, T // tb),
        in_specs=in_specs,
        out_specs=spec,
        scratch_shapes=[pltpu.VMEM((HG_DK, HG_DK), F32)],
        compiler_params=_cparams(("parallel", "arbitrary")),
        name="hgrn_bwd" if reverse else "hgrn_fwd",
    )(*args)


def _kvproj_kernel(m_ref, w_ref, o_ref):
    o_ref[...] = _dot(m_ref[...].astype(BF16), w_ref[...]).astype(BF16)


def _kvproj(mem, w):
    R, D = mem.shape
    n = w.shape[1]
    tr = 256
    return pl.pallas_call(
        _kvproj_kernel,
        out_shape=jax.ShapeDtypeStruct((R, n), BF16),
        grid=(R // tr,),
        in_specs=[pl.BlockSpec((tr, D), lambda i: (i, 0)), pl.BlockSpec((D, n), lambda i: (0, 0))],
        out_specs=pl.BlockSpec((tr, n), lambda i: (i, 0)),
        compiler_params=_cparams(("parallel",)),
        name="kvproj",
    )(mem, w)


def _mix_xattn_kernel(x0_ref, x1_ref, att_ref, rec_ref, g0_ref, b0_ref, wo_ref, g1_ref, b1_ref, wq_ref, kv_ref,
                      wxo_ref, g2_ref, b2_ref, x2_ref, p_ref, *, tiles_group0):
    D = x0_ref.shape[1]
    dh = D // X_HEADS
    xn = _layer_norm(_group_rows(x0_ref, x1_ref, tiles_group0), g0_ref[...], b0_ref[...])
    mix = _dot(att_ref[...], wo_ref[:ATT_WIDTH, :]) + _dot(rec_ref[...], wo_ref[ATT_WIDTH:, :])
    x1 = _layer_norm(DEEPNORM_ALPHA * xn + mix, g1_ref[...], b1_ref[...])
    qx = (_dot(x1.astype(BF16), wq_ref[...]) * (dh ** -0.5)).astype(BF16)
    heads = []
    for h in range(X_HEADS):
        kh = kv_ref[0, :, h * dh:(h + 1) * dh]
        vh = kv_ref[0, :, D + h * dh:D + (h + 1) * dh]
        s = _dot_nt(qx[:, h * dh:(h + 1) * dh], kh)
        s = s - jnp.max(s, axis=-1, keepdims=True)
        p = jnp.exp(s)
        p = p / jnp.sum(p, axis=-1, keepdims=True)
        heads.append(_dot(p.astype(BF16), vh).astype(BF16))
    o = jnp.concatenate(heads, axis=-1)
    x2 = _layer_norm(DEEPNORM_ALPHA * x1 + _dot(o, wxo_ref[...]), g2_ref[...], b2_ref[...])
    x2_ref[...] = x2
    _store_slabs(p_ref, _pack_bf16_pair(x2[:, :D // 2], x2[:, D // 2:]))


def _mix_xattn(x0, x1, att, rec, g0, b0, wo, g1, b1, wq, kv, wxo, g2, b2, seq_of_tile):
    D = x0.shape[1]
    T = x0.shape[0] + x1.shape[0]
    tm = TOKEN_TILE
    tiles_group0 = x0.shape[0] // tm
    n_mem = kv.shape[1]
    row = lambda i: (i, 0)
    const = lambda i: (0, 0)
    vec = pl.BlockSpec((1, D), const)
    return pl.pallas_call(
        functools.partial(_mix_xattn_kernel, tiles_group0=tiles_group0),
        out_shape=(jax.ShapeDtypeStruct((T, D), F32), jax.ShapeDtypeStruct((T * SLAB, LANES), U32)),
        grid=(T // tm,),
        in_specs=_group_specs(tm, D, tiles_group0) + [
            pl.BlockSpec((tm, ATT_WIDTH), row),
            pl.BlockSpec((tm, HG_WIDTH), row),
            vec, vec,
            pl.BlockSpec((ATT_WIDTH + HG_WIDTH, D), const),
            vec, vec,
            pl.BlockSpec((D, D), const),
            pl.BlockSpec((1, n_mem, 2 * D), lambda i: (seq_of_tile(i), 0, 0)),
            pl.BlockSpec((D, D), const),
            vec, vec,
        ],
        out_specs=(pl.BlockSpec((tm, D), row), pl.BlockSpec((tm * SLAB, LANES), row)),
        compiler_params=_cparams(("parallel",)),
        name="mix_xattn",
    )(x0, x1, att, rec, g0, b0, wo, g1, b1, wq, kv, wxo, g2, b2)


def _router_kernel(x_ref, wt_ref, b_ref, idx_ref, gate_ref, rank_ref, cnt_ref, run_ref):
    tm = x_ref.shape[0]

    @pl.when(pl.program_id(0) == 0)
    def _():
        run_ref[...] = jnp.zeros_like(run_ref)

    xh, xl, _ = _split3(x_ref[...])
    wh, wl, _ = _split3(wt_ref[...])
    logits = _dot_nt(wh, xh) + _dot_nt(wh, xl) + _dot_nt(wl, xh) + b_ref[...]
    e_id = lax.broadcasted_iota(I32, (N_EXPERTS, tm), 0)
    work = logits
    vals, idxs = [], []
    chosen = jnp.zeros((N_EXPERTS, tm), F32)
    for _ in range(TOP_K):
        m = jnp.max(work, axis=0, keepdims=True)
        idx = jnp.min(jnp.where(work == m, e_id, N_EXPERTS), axis=0, keepdims=True)
        hit = e_id == idx
        chosen = jnp.where(hit, 1.0, chosen)
        work = jnp.where(hit, NEG_INF, work)
        vals.append(m)
        idxs.append(idx)
    ex = [jnp.exp(v - vals[0]) for v in vals]
    den = ex[0] + ex[1] + ex[2] + ex[3]
    t_r = lax.broadcasted_iota(I32, (tm, tm), 0)
    t_c = lax.broadcasted_iota(I32, (tm, tm), 1)
    before = jnp.where(t_r < t_c, 1.0, 0.0).astype(BF16)
    rank_e = _dot(chosen.astype(BF16), before) + run_ref[...]
    run_ref[...] = run_ref[...] + jnp.sum(chosen, axis=1, keepdims=True)
    for k in range(TOP_K):
        idx_ref[k:k + 1, :] = idxs[k]
        gate_ref[k:k + 1, :] = ex[k] / den
        rank_ref[k:k + 1, :] = jnp.sum(jnp.where(e_id == idxs[k], rank_e, 0.0), axis=0, keepdims=True).astype(I32)
    cnt_ref[...] = run_ref[...].astype(I32)


def _router(x2, w_router_t, b_router):
    T, D = x2.shape
    tm = TOKEN_TILE
    tok = pl.BlockSpec((TOP_K, tm), lambda i: (0, i))
    return pl.pallas_call(
        _router_kernel,
        out_shape=(jax.ShapeDtypeStruct((TOP_K, T), I32), jax.ShapeDtypeStruct((TOP_K, T), F32),
                   jax.ShapeDtypeStruct((TOP_K, T), I32), jax.ShapeDtypeStruct((N_EXPERTS, 1), I32)),
        grid=(T // tm,),
        in_specs=[pl.BlockSpec((tm, D), lambda i: (i, 0)),
                  pl.BlockSpec((N_EXPERTS, D), lambda i: (0, 0)),
                  pl.BlockSpec((N_EXPERTS, 1), lambda i: (0, 0))],
        out_specs=(tok, tok, tok, pl.BlockSpec((N_EXPERTS, 1), lambda i: (0, 0))),
        scratch_shapes=[pltpu.VMEM((N_EXPERTS, 1), F32)],
        compiler_params=_cparams(("arbitrary",)),
        name="router",
    )(x2, w_router_t, b_router)


def _dispatch_kernel(dest_ref, p_ref, zero_hbm, out_hbm, sem):
    del zero_hbm
    td = DISPATCH_TILE

    def start(t, c):
        for k in range(TOP_K):
            d = dest_ref[0, 0, k * td + t]
            pltpu.make_async_copy(p_ref.at[t], out_hbm.at[d], sem).start(priority=k % DMA_PRIORITIES)
        return c

    lax.fori_loop(0, td, start, 0)
    everything = out_hbm.at[pl.ds(0, TOP_K * td)]
    pltpu.make_async_copy(everything, everything, sem).wait()


def _dispatch(dest_tiles, p, zeros):
    T = p.shape[0]
    td = DISPATCH_TILE
    return pl.pallas_call(
        _dispatch_kernel,
        out_shape=jax.ShapeDtypeStruct(zeros.shape, zeros.dtype),
        grid=(T // td,),
        in_specs=[pl.BlockSpec((1, 1, TOP_K * td), lambda i: (i, 0, 0), memory_space=pltpu.SMEM),
                  pl.BlockSpec((td,) + p.shape[1:], lambda i: (i, 0, 0)),
                  pl.BlockSpec(memory_space=pl.ANY)],
        out_specs=pl.BlockSpec(memory_space=pl.ANY),
        scratch_shapes=[pltpu.SemaphoreType.DMA],
        input_output_aliases={2: 0},
        compiler_params=_cparams(("arbitrary",)),
        name="dispatch",
    )(dest_tiles, p, zeros)


def _expert_kernel(blk_e_ref, nvalid_ref, p_ref, wu_ref, bu_ref, wd_ref, bd_ref, y_ref, wu_b, wd_b):
    i = pl.program_id(0)
    dff = wd_ref.shape[1]
    half = SLAB * LANES
    live = i < nvalid_ref[0]
    new_expert = (i == 0) | (blk_e_ref[i] != blk_e_ref[jnp.maximum(i - 1, 0)])

    @pl.when(live & new_expert)
    def _():
        wu_b[...] = wu_ref[0].astype(BF16)
        wd_b[...] = wd_ref[0].astype(BF16)

    @pl.when(live)
    def _():
        lo, hi = _unpack_bf16_pair(_load_slabs(p_ref))
        lo = lo.astype(BF16)
        hi = hi.astype(BF16)

        def up(c0):
            return (_dot(lo, wu_b[:half, c0:c0 + dff]) + _dot(hi, wu_b[half:, c0:c0 + dff])
                    + bu_ref[0, :, c0:c0 + dff])

        glu = jnp.minimum(up(0), SWIGLU_LIMIT)
        lin = jnp.clip(up(dff), -SWIGLU_LIMIT, SWIGLU_LIMIT)
        a = glu * _sigmoid(SWIGLU_ALPHA * glu) * (lin + 1.0)
        y = _dot(a.astype(BF16), wd_b[...]) + bd_ref[0]
        _store_slabs(y_ref, _pack_bf16_pair(y[:, :half], y[:, half:]))

    @pl.when(i >= nvalid_ref[0])
    def _():
        y_ref[...] = jnp.zeros_like(y_ref)


def _experts(blk_expert, n_valid, ps, wu, bu, wd, bd):
    R = ps.shape[0] // SLAB
    te = EXPERT_TILE
    E, D, two_f = wu.shape
    dff = wd.shape[1]
    grid_spec = pltpu.PrefetchScalarGridSpec(
        num_scalar_prefetch=2,
        grid=(R // te,),
        in_specs=[
            pl.BlockSpec((te * SLAB, LANES), lambda i, be, nv: (i, 0)),
            pl.BlockSpec((1, D, two_f), lambda i, be, nv: (be[i], 0, 0)),
            pl.BlockSpec((1, 1, two_f), lambda i, be, nv: (be[i], 0, 0)),
            pl.BlockSpec((1, dff, D), lambda i, be, nv: (be[i], 0, 0)),
            pl.BlockSpec((1, 1, D), lambda i, be, nv: (be[i], 0, 0)),
        ],
        out_specs=pl.BlockSpec((te * SLAB, LANES), lambda i, be, nv: (i, 0)),
        scratch_shapes=[pltpu.VMEM((D, two_f), BF16), pltpu.VMEM((dff, D), BF16)],
    )
    return pl.pallas_call(
        _expert_kernel,
        out_shape=jax.ShapeDtypeStruct((R * SLAB, LANES), U32),
        grid_spec=grid_spec,
        compiler_params=_cparams(("arbitrary",)),
        name="experts",
    )(blk_expert, n_valid, ps, wu, bu, wd, bd)


def _combine_kernel(dest_ref, dest_next_ref, ys_hbm, x2_ref, gate_ref, g_ref, b_ref, o_ref, buf, sems):
    tc = x2_ref.shape[0]
    half = SLAB * LANES
    i = pl.program_id(0)
    slot = i % 2

    def start_tile(idx_ref, s):
        def body(t, c):
            rows = pl.ds(pl.multiple_of(t * SLAB, SLAB), SLAB)
            for k in range(TOP_K):
                d = idx_ref[0, 0, k * tc + t]
                pltpu.make_async_copy(ys_hbm.at[d], buf.at[s, k, rows], sems.at[s]).start(
                    priority=k % DMA_PRIORITIES)
            return c
        lax.fori_loop(0, tc, body, 0)

    @pl.when(i == 0)
    def _():
        start_tile(dest_ref, 0)

    for s in range(2):
        @pl.when((i + 1 < pl.num_programs(0)) & (slot == s))
        def _():
            start_tile(dest_next_ref, 1 - s)

    for s in range(2):
        @pl.when(slot == s)
        def _():
            pltpu.make_async_copy(buf.at[s], buf.at[s], sems.at[s]).wait()

    gates = gate_ref[...]
    acc_lo = jnp.zeros((tc, half), F32)
    acc_hi = jnp.zeros((tc, half), F32)
    for k in range(TOP_K):
        lo, hi = _unpack_bf16_pair(_load_slabs(buf.at[slot, k]))
        gk = gates[:, k:k + 1]
        acc_lo = acc_lo + gk * lo
        acc_hi = acc_hi + gk * hi
    ffn = jnp.concatenate([acc_lo, acc_hi], axis=-1)
    o_ref[...] = _layer_norm(DEEPNORM_ALPHA * x2_ref[...] + ffn, g_ref[...], b_ref[...])


def _combine(dest_tiles, ys, x2, gates, g, b, *, tile0, n_tiles):
    D = x2.shape[1]
    tc = COMBINE_TILE
    return pl.pallas_call(
        _combine_kernel,
        out_shape=jax.ShapeDtypeStruct((n_tiles * tc, D), F32),
        grid=(n_tiles,),
        in_specs=[pl.BlockSpec((1, 1, TOP_K * tc), lambda i: (tile0 + i, 0, 0), memory_space=pltpu.SMEM),
                  pl.BlockSpec((1, 1, TOP_K * tc), lambda i: (tile0 + jnp.minimum(i + 1, n_tiles - 1), 0, 0),
                               memory_space=pltpu.SMEM),
                  pl.BlockSpec(memory_space=pl.ANY),
                  pl.BlockSpec((tc, D), lambda i: (tile0 + i, 0)),
                  pl.BlockSpec((tc, TOP_K), lambda i: (tile0 + i, 0)),
                  pl.BlockSpec((1, D), lambda i: (0, 0)),
                  pl.BlockSpec((1, D), lambda i: (0, 0))],
        out_specs=pl.BlockSpec((tc, D), lambda i: (i, 0)),
        scratch_shapes=[pltpu.VMEM((2, TOP_K, tc * SLAB, LANES), U32), pltpu.SemaphoreType.DMA((2,))],
        compiler_params=_cparams(("arbitrary",)),
        name="combine",
    )(dest_tiles, dest_tiles, ys, x2, gates, g, b)


def _rope_tables(seq_len):
    rows = seq_len // GRID_W
    row = jnp.repeat(jnp.arange(rows, dtype=F32), GRID_W)
    colp = (jnp.arange(seq_len) % GRID_W).astype(F32)
    axis_dim = ATT_HEAD_DIM // 2
    inv_freq = ROPE_THETA ** (-jnp.arange(0, axis_dim, 2, dtype=F32) / axis_dim)
    ang = jnp.concatenate([row[:, None] * inv_freq, colp[:, None] * inv_freq], axis=-1)
    cos, sin = jnp.cos(ang), jnp.sin(ang)
    cos_h = jnp.concatenate([cos, cos], axis=-1)
    sin_h = jnp.concatenate([-sin, sin], axis=-1)
    return jnp.tile(cos_h, (1, LANES // ATT_HEAD_DIM)), jnp.tile(sin_h, (1, LANES // ATT_HEAD_DIM))


def _tiles_of(idx_t, tile):
    K, T = idx_t.shape
    return idx_t.reshape(K, T // tile, tile).transpose(1, 0, 2).reshape(T // tile, 1, K * tile)


def kernel(x_prompt, x_sample, mem_prompt, mem_sample, ln_in_g, ln_in_b, w_in, q_norm_g, k_norm_g, hg_lb_fwd, hg_lb_bwd, hg_norm_g, w_out, ln1_g, ln1_b, w_xq, w_xkv, w_xo, ln2_g, ln2_b, w_router, b_router, w_up, b_up, w_down, b_down, ln3_g, ln3_b):
    Bp, Sp, D = x_prompt.shape
    Bs, Ss, _ = x_sample.shape
    Tp, Ts = Bp * Sp, Bs * Ss
    T = Tp + Ts
    tm = TOKEN_TILE
    assert Sp % tm == 0 and Ss % Sp == 0
    n_mem = mem_prompt.shape[1]
    layer = 0

    x0, x1 = x_prompt.reshape(Tp, D), x_sample.reshape(Ts, D)
    mem = jnp.concatenate([mem_prompt.reshape(Bp * n_mem, D), mem_sample.reshape(Bs * n_mem, D)], axis=0)

    def deinterleave(a):
        lead = a.shape[:-1]
        a = a.reshape(*lead, -1, ATT_HEAD_DIM // 2, 2)
        return jnp.swapaxes(a, -1, -2).reshape(*lead, -1)

    n_qk = ATT_WIDTH + ATT_KV_WIDTH
    w_in_l = w_in[layer]
    w_in_b = jnp.concatenate([deinterleave(w_in_l[:, :n_qk]), w_in_l[:, n_qk:]], axis=1).astype(BF16)
    pair = LANES // ATT_HEAD_DIM
    qg = jnp.tile(deinterleave(q_norm_g[layer]), pair).reshape(1, LANES)
    kg = jnp.tile(deinterleave(k_norm_g[layer]), pair).reshape(1, LANES)

    def lower_bound(logits):
        sm = jax.nn.softmax(logits.astype(F32), axis=0)
        return (jnp.cumsum(sm, axis=0)[layer + 1] - sm[0]).reshape(1, HG_WIDTH)

    lbf, lbb = lower_bound(hg_lb_fwd), lower_bound(hg_lb_bwd)
    cos_t, sin_t = _rope_tables(max(Sp, Ss))
    vec = lambda a: a.reshape(1, -1)

    groups = _Groups(Bp, Sp, Bs, Ss)
    pos_block = lambda i: groups.pos_of_tile(i, tm)
    seq_of_tile = lambda i: groups.seq_of_tile(i, tm)

    q, k, v, hq, hi, hg, lff, lfb = _inproj(x0, x1, vec(ln_in_g), vec(ln_in_b), w_in_b, cos_t, sin_t, qg, kg,
                                           lbf, lbb, pos_block)

    att = _attention(q, k, v, groups)
    o_fwd = _hgrn_pass(hq, hi, lff, groups, reverse=False)
    gain = hg_norm_g[layer].reshape(1, HG_DK)
    rec = _hgrn_pass(hq, hi, lfb, groups, reverse=True, finalize_args=(o_fwd, hg, gain))

    kv = _kvproj(mem, w_xkv[layer].astype(BF16)).reshape(Bp + Bs, n_mem, 2 * D)
    x2, packed = _mix_xattn(x0, x1, att, rec, vec(ln_in_g), vec(ln_in_b), w_out[layer].astype(BF16),
                            vec(ln1_g[layer]), vec(ln1_b[layer]), w_xq[layer].astype(BF16), kv,
                            w_xo[layer].astype(BF16), vec(ln2_g[layer]), vec(ln2_b[layer]), seq_of_tile)

    idx_t, gate_t, rank_t, counts = _router(x2, w_router[layer].T, b_router[layer].reshape(N_EXPERTS, 1))

    te = EXPERT_TILE
    counts = counts.reshape(N_EXPERTS)
    padded = (counts + te - 1) // te * te
    pad_end = jnp.cumsum(padded)
    pad_start = pad_end - padded
    n_blk = (T * TOP_K + N_EXPERTS * (te - 1) + te - 1) // te
    e_ids = jnp.arange(N_EXPERTS, dtype=I32)
    start_of = jnp.sum(jnp.where(idx_t[None] == e_ids[:, None, None], pad_start[:, None, None], 0), axis=0)
    dest_t = start_of + rank_t
    blk_row0 = jnp.arange(n_blk, dtype=I32) * te
    blk_expert = jnp.minimum(jnp.sum((pad_end[None, :] <= blk_row0[:, None]).astype(I32), axis=1), N_EXPERTS - 1)
    n_valid = (pad_end[-1] // te).astype(I32).reshape(1)

    n_rows = n_blk * te
    sorted_rows = _dispatch(_tiles_of(dest_t, DISPATCH_TILE), packed.reshape(T, SLAB, LANES),
                            jnp.zeros((n_rows, SLAB, LANES), U32))
    ys = _experts(blk_expert, n_valid, sorted_rows.reshape(n_rows * SLAB, LANES), w_up[layer],
                  b_up[layer].reshape(N_EXPERTS, 1, -1), w_down[layer], b_down[layer].reshape(N_EXPERTS, 1, -1))
    ys = ys.reshape(n_rows, SLAB, LANES)

    dest_c = _tiles_of(dest_t, COMBINE_TILE)
    gates = gate_t.T
    tc = COMBINE_TILE
    y_p = _combine(dest_c, ys, x2, gates, vec(ln3_g[layer]), vec(ln3_b[layer]), tile0=0, n_tiles=Tp // tc)
    y_s = _combine(dest_c, ys, x2, gates, vec(ln3_g[layer]), vec(ln3_b[layer]), tile0=Tp // tc, n_tiles=Ts // tc)
    return y_p.reshape(Bp, Sp, D), y_s.reshape(Bs, Ss, D)
```

```python
import functools

import jax
import jax.numpy as jnp
import numpy as np
from jax import lax
from jax.experimental import pallas as pl
from jax.experimental.pallas import tpu as pltpu

F32 = jnp.float32
BF16 = jnp.bfloat16
I32 = jnp.int32
U32 = jnp.uint32

GRID_W = 64
ATT_HEADS = 8
ATT_KV_HEADS = 2
ATT_HEAD_DIM = 64
ATT_WIDTH = ATT_HEADS * ATT_HEAD_DIM
ATT_KV_WIDTH = ATT_KV_HEADS * ATT_HEAD_DIM
ROPE_THETA = 10000.0
HG_HEADS = 4
HG_DK = 128
HG_WIDTH = HG_HEADS * HG_DK
X_HEADS = 4
N_EXPERTS = 32
TOP_K = 4
SWIGLU_LIMIT = 7.0
SWIGLU_ALPHA = 1.702
LN_EPS = 1e-5
RMS_EPS = 1e-6
DEPTH = 1
DEEPNORM_ALPHA = (2.0 * DEPTH) ** 0.25

LANES = 128
VMEM_LIMIT_BYTES = 56 * 1024 * 1024
DMA_PRIORITIES = 2
DMA_ISSUE_UNROLL = 4

TOKEN_TILE = 512
ATT_Q_TILE = 256
ATT_K_TILE = 4096
HG_CHUNK = 64
HG_SUB = 16
HG_HEADS_PER_STEP = 2

ATT_SAFE_SCORE = 60.0
ATT_NORM_SLACK = 1.01
HG_SAFE_LOG = 5.0
EXPERT_TILE = 512
COMBINE_TILE = 256
DISPATCH_TILE = 512
ROW_PARTS = 2
NEG_INF = float("-inf")


def _cparams(sem):
    return pltpu.CompilerParams(dimension_semantics=sem, vmem_limit_bytes=VMEM_LIMIT_BYTES)


def _layer_norm(x, g, b):
    mu = jnp.mean(x, axis=-1, keepdims=True)
    xc = x - mu
    var = jnp.mean(xc * xc, axis=-1, keepdims=True)
    return xc * lax.rsqrt(var + LN_EPS) * g + b


def _sigmoid(x):
    return 1.0 / (1.0 + jnp.exp(-x))


def _dot(a, b):
    return jnp.dot(a, b, preferred_element_type=F32)


def _dot_nt(a, b):
    return lax.dot_general(a, b, (((1,), (1,)), ((), ())), preferred_element_type=F32)


def _dot_tn(a, b):
    return lax.dot_general(a, b, (((0,), (0,)), ((), ())), preferred_element_type=F32)


def _split3(x):
    h1 = x.astype(BF16)
    r1 = x - h1.astype(F32)
    h2 = r1.astype(BF16)
    h3 = (r1 - h2.astype(F32)).astype(BF16)
    return h1, h2, h3


def _pack_bf16_pair(lo, hi):
    lo_b = pltpu.bitcast(lo.astype(BF16).astype(F32), U32)
    hi_b = pltpu.bitcast(hi.astype(BF16).astype(F32), U32)
    return (hi_b & jnp.uint32(0xFFFF0000)) | (lo_b >> 16)


def _unpack_bf16_pair(p):
    lo = pltpu.bitcast(p << 16, F32)
    hi = pltpu.bitcast(p & jnp.uint32(0xFFFF0000), F32)
    return lo, hi


SLAB = 4


def _store_slabs(ref, value, row0=0):
    n = value.shape[0]
    for s in range(SLAB):
        ref[pl.ds(row0 * SLAB + s, n, stride=SLAB), :] = value[:, s * LANES:(s + 1) * LANES]


def _load_slabs(ref, row0=0, n=None):
    n = ref.shape[0] // SLAB if n is None else n
    return jnp.concatenate([ref[pl.ds(row0 * SLAB + s, n, stride=SLAB), :] for s in range(SLAB)], axis=-1)


def _row_parts(n_rows):
    part = n_rows // ROW_PARTS
    return [slice(r * part, (r + 1) * part) for r in range(ROW_PARTS)]


def _staggered(stage_fn, n_rows):
    pending = [stage_fn(rows) for rows in _row_parts(n_rows)]
    while pending:
        for g in list(pending):
            try:
                next(g)
            except StopIteration:
                pending.remove(g)


def _group_rows(x0_ref, x1_ref, tiles_group0):
    return jnp.where(pl.program_id(0) < tiles_group0, x0_ref[...], x1_ref[...])


def _group_specs(tile, width, tiles_group0):
    return [pl.BlockSpec((tile, width), lambda i: (jnp.minimum(i, tiles_group0 - 1), 0)),
            pl.BlockSpec((tile, width), lambda i: (jnp.maximum(i - tiles_group0, 0), 0))]


def _inproj_kernel(x0_ref, x1_ref, g_ref, b_ref, w_ref, cos_ref, sin_ref, qg_ref, kg_ref, lbf_ref, lbb_ref,
                   q_out, k_out, v_out, hq_out, hi_out, hg_out, lff_out, lfb_out, *, tiles_group0):
    xn = _layer_norm(_group_rows(x0_ref, x1_ref, tiles_group0), g_ref[...], b_ref[...])
    xb = xn.astype(BF16)
    cos = cos_ref[...]
    sin = sin_ref[...]
    tm = xb.shape[0]

    lane = lax.broadcasted_iota(I32, (tm, LANES), 1)
    first_half = (lane & (ATT_HEAD_DIM - 1)) < (ATT_HEAD_DIM // 2)
    r = lax.broadcasted_iota(I32, (LANES, LANES), 0)
    c = lax.broadcasted_iota(I32, (LANES, LANES), 1)
    head_ones = jnp.where((r >> 6) == (c >> 6), 1.0, 0.0).astype(BF16)

    def normed_rope(u, gain, scale):
        sq = u * u
        s_hi = sq.astype(BF16)
        s_lo = (sq - s_hi.astype(F32)).astype(BF16)
        ms = (_dot(s_hi, head_ones) + _dot(s_lo, head_ones)) * (1.0 / ATT_HEAD_DIM)
        un = u * lax.rsqrt(ms + RMS_EPS) * gain
        rot = jnp.where(first_half, pltpu.roll(un, LANES - 32, 1), pltpu.roll(un, 32, 1))
        return (un * cos + rot * sin) * scale

    col = 0
    for j in range(ATT_WIDTH // (2 * LANES)):
        u = _dot(xb, w_ref[:, col:col + 2 * LANES])
        for h in range(2):
            c0 = (2 * j + h) * LANES
            q_out[:, c0:c0 + LANES] = normed_rope(u[:, h * LANES:(h + 1) * LANES], qg_ref[...],
                                                  ATT_HEAD_DIM ** -0.5).astype(BF16)
        col += 2 * LANES
    u = _dot(xb, w_ref[:, col:col + 2 * LANES])
    k_out[...] = normed_rope(u[:, :LANES], kg_ref[...], 1.0).astype(BF16)
    v_out[:, :LANES] = u[:, LANES:].astype(BF16)
    v_out[:, LANES:] = jnp.ones((tm, LANES), BF16)
    col += 2 * LANES

    u = _dot(xb, w_ref[:, col:col + HG_WIDTH])
    hq_out[...] = (u * _sigmoid(u)).astype(BF16)
    col += HG_WIDTH
    for lb_ref, lf_out in ((lbf_ref, lff_out), (lbb_ref, lfb_out)):
        u = _dot(xb, w_ref[:, col:col + HG_WIDTH])
        lb = lb_ref[...]
        lf_out[...] = jnp.log(lb + (1.0 - lb) * _sigmoid(u))
        col += HG_WIDTH
    hi_out[...] = _dot(xb, w_ref[:, col:col + HG_WIDTH]).astype(BF16)
    col += HG_WIDTH
    u = _dot(xb, w_ref[:, col:col + HG_WIDTH])
    hg_out[...] = (u * _sigmoid(u)).astype(BF16)


def _inproj(x0, x1, g, b, w, cos_t, sin_t, qg, kg, lbf, lbb, pos_block):
    D = x0.shape[1]
    T = x0.shape[0] + x1.shape[0]
    tm = TOKEN_TILE
    tiles_group0 = x0.shape[0] // tm
    n_in = w.shape[1]
    row = lambda i: (i, 0)
    const = lambda i: (0, 0)
    out_shape = (
        jax.ShapeDtypeStruct((T, ATT_WIDTH), BF16),
        jax.ShapeDtypeStruct((T, ATT_KV_WIDTH), BF16),
        jax.ShapeDtypeStruct((T, 2 * ATT_KV_WIDTH), BF16),
        jax.ShapeDtypeStruct((T, HG_WIDTH), BF16),
        jax.ShapeDtypeStruct((T, HG_WIDTH), BF16),
        jax.ShapeDtypeStruct((T, HG_WIDTH), BF16),
        jax.ShapeDtypeStruct((T, HG_WIDTH), F32),
        jax.ShapeDtypeStruct((T, HG_WIDTH), F32),
    )
    out_specs = (
        pl.BlockSpec((tm, ATT_WIDTH), row),
        pl.BlockSpec((tm, ATT_KV_WIDTH), row),
        pl.BlockSpec((tm, 2 * ATT_KV_WIDTH), row),
        pl.BlockSpec((tm, HG_WIDTH), row),
        pl.BlockSpec((tm, HG_WIDTH), row),
        pl.BlockSpec((tm, HG_WIDTH), row),
        pl.BlockSpec((tm, HG_WIDTH), row),
        pl.BlockSpec((tm, HG_WIDTH), row),
    )
    return pl.pallas_call(
        functools.partial(_inproj_kernel, tiles_group0=tiles_group0),
        out_shape=out_shape,
        grid=(T // tm,),
        in_specs=_group_specs(tm, D, tiles_group0) + [
            pl.BlockSpec((1, D), const),
            pl.BlockSpec((1, D), const),
            pl.BlockSpec((D, n_in), const),
            pl.BlockSpec((tm, LANES), lambda i: (pos_block(i), 0)),
            pl.BlockSpec((tm, LANES), lambda i: (pos_block(i), 0)),
            pl.BlockSpec((1, LANES), const),
            pl.BlockSpec((1, LANES), const),
            pl.BlockSpec((1, HG_WIDTH), const),
            pl.BlockSpec((1, HG_WIDTH), const),
        ],
        out_specs=out_specs,
        compiler_params=_cparams(("parallel",)),
        name="inproj",
    )(x0, x1, g, b, w, cos_t, sin_t, qg, kg, lbf, lbb)


class _Groups:
    def __init__(self, n0, s0, n1, s1):
        self.n0, self.s0, self.n1, self.s1 = n0, s0, n1, s1
        self.t0 = n0 * s0
        self.total = self.t0 + n1 * s1

    def split(self, i, tile):
        nt0 = self.t0 // tile
        per0, per1 = self.s0 // tile, self.s1 // tile
        in1 = i >= nt0
        j1 = jnp.maximum(i - nt0, 0)
        seq = jnp.where(in1, self.n0 + j1 // per1, i // per0)
        pos = jnp.where(in1, j1 % per1, i % per0)
        return in1, seq, pos, jnp.where(in1, per1, per0)

    def seq_of_tile(self, i, tile):
        return self.split(i, tile)[1]

    def pos_of_tile(self, i, tile):
        return self.split(i, tile)[2]

    def reversed_tile(self, i, tile):
        _, _, pos, per = self.split(i, tile)
        return i - pos + (per - 1 - pos)


def _attention_kernel(*refs, n_kv_blocks, groups):
    q_ref = refs[0]
    k_refs = refs[1:1 + n_kv_blocks]
    v_refs = refs[1 + n_kv_blocks:1 + 2 * n_kv_blocks]
    o_ref, kmax_ref, out_ref = refs[1 + 2 * n_kv_blocks:]
    tq = q_ref.shape[0]
    kv_rows = k_refs[0].shape[0]
    tk = min(ATT_K_TILE, kv_rows)
    n_k = kv_rows // tk
    group = ATT_HEADS // ATT_KV_HEADS
    rows = group * tq
    lane = lax.broadcasted_iota(I32, (tq, LANES), 1)
    low_half = lane < ATT_HEAD_DIM
    in_group1, _, pos, _ = groups.split(pl.program_id(0), tq)
    extra_trips = jnp.where(in_group1, n_k, 0)

    @pl.when(pos == 0)
    def _():
        r = lax.broadcasted_iota(I32, (LANES, LANES), 0)
        c = lax.broadcasted_iota(I32, (LANES, LANES), 1)
        head_ones = jnp.where((r >> 6) == (c >> 6), 1.0, 0.0).astype(BF16)
        best = jnp.zeros((1, LANES), F32)
        for k_ref in k_refs:
            kf = k_ref[...].astype(F32)
            best = jnp.maximum(best, jnp.max(_dot((kf * kf).astype(BF16), head_ones), axis=0, keepdims=True))
        kmax_ref[...] = best * ATT_NORM_SLACK

    for kvh in range(ATT_KV_HEADS):
        keep = low_half if kvh == 0 else jnp.logical_not(low_half)
        parts = []
        for g in range(group):
            c0 = kvh * 2 * LANES + (g // 2) * LANES
            pair = q_ref[:, c0:c0 + LANES].astype(F32)
            if g % 2 != kvh:
                pair = pltpu.roll(pair, ATT_HEAD_DIM, 1)
            parts.append(jnp.where(keep, pair, 0.0).astype(BF16))
        qp = jnp.concatenate(parts, axis=0)

        qf = qp.astype(F32)
        q_sq = jnp.sum(jnp.max(qf * qf, axis=0, keepdims=True))
        k_sq = jnp.max(jnp.where(keep[:1], kmax_ref[...], 0.0))
        unshifted_ok = q_sq * k_sq <= ATT_SAFE_SCORE * ATT_SAFE_SCORE

        def run(make_body, carry):
            carry = lax.fori_loop(0, n_k, make_body(k_refs[0], v_refs[0]), carry)
            for kb in range(1, n_kv_blocks):
                carry = lax.fori_loop(0, extra_trips, make_body(k_refs[kb], v_refs[kb]), carry)
            return carry

        @pl.when(unshifted_ok)
        def _():
            def make_body(k_ref, v_ref):
                def body(kc, acc):
                    r0 = pl.multiple_of(kc * tk, tk)
                    p = jnp.exp(_dot_nt(qp, k_ref[pl.ds(r0, tk), :]))
                    return acc + _dot(p.astype(BF16), v_ref[pl.ds(r0, tk), :])
                return body

            acc = run(make_body, jnp.zeros((rows, 2 * LANES), F32))
            out_ref[...] = acc[:, :LANES] / acc[:, LANES:]

        @pl.when(jnp.logical_not(unshifted_ok))
        def _():
            def make_body(k_ref, v_ref):
                def body(kc, carry):
                    m, l, acc = carry
                    r0 = pl.multiple_of(kc * tk, tk)
                    s = _dot_nt(qp, k_ref[pl.ds(r0, tk), :])
                    m_new = jnp.maximum(m, jnp.max(s, axis=-1, keepdims=True))
                    alpha = jnp.exp(m - m_new)
                    p = jnp.exp(s - m_new)
                    l_new = alpha * l + jnp.sum(p, axis=-1, keepdims=True)
                    acc_new = alpha * acc + _dot(p.astype(BF16), v_ref[pl.ds(r0, tk), :LANES])
                    return m_new, l_new, acc_new
                return body

            init = (jnp.full((rows, 1), NEG_INF, F32), jnp.zeros((rows, 1), F32), jnp.zeros((rows, LANES), F32))
            m, l, acc = run(make_body, init)
            out_ref[...] = acc / l

        out = out_ref[...]
        for j in range(group // 2):
            a = out[(2 * j) * tq:(2 * j + 1) * tq]
            b = out[(2 * j + 1) * tq:(2 * j + 2) * tq]
            if kvh == 1:
                a = pltpu.roll(a, ATT_HEAD_DIM, 1)
            else:
                b = pltpu.roll(b, ATT_HEAD_DIM, 1)
            c0 = kvh * 2 * LANES + j * LANES
            o_ref[:, c0:c0 + LANES] = jnp.where(low_half, a, b).astype(BF16)


def _attention(q, k, v, groups):
    T = q.shape[0]
    tq = ATT_Q_TILE
    kv_rows = groups.s0
    n_kv_blocks = groups.s1 // groups.s0

    def kv_map(j):
        def index_map(i):
            in1, seq, _, _ = groups.split(i, tq)
            first = jnp.where(in1, groups.t0 // kv_rows + (seq - groups.n0) * n_kv_blocks, seq)
            return (first + jnp.where(in1, j, 0), 0)
        return index_map

    k_specs = [pl.BlockSpec((kv_rows, ATT_KV_WIDTH), kv_map(j)) for j in range(n_kv_blocks)]
    v_specs = [pl.BlockSpec((kv_rows, 2 * ATT_KV_WIDTH), kv_map(j)) for j in range(n_kv_blocks)]
    return pl.pallas_call(
        functools.partial(_attention_kernel, n_kv_blocks=n_kv_blocks, groups=groups),
        out_shape=jax.ShapeDtypeStruct((T, ATT_WIDTH), BF16),
        grid=(T // tq,),
        in_specs=[pl.BlockSpec((tq, ATT_WIDTH), lambda i: (i, 0))] + k_specs + v_specs,
        out_specs=pl.BlockSpec((tq, ATT_WIDTH), lambda i: (i, 0)),
        scratch_shapes=[pltpu.VMEM((1, LANES), F32),
                        pltpu.VMEM((ATT_HEADS // ATT_KV_HEADS * tq, LANES), F32)],
        compiler_params=_cparams(("arbitrary",)),
        name="attention",
    )(q, *([k] * n_kv_blocks), *([v] * n_kv_blocks))


def _hgrn_kernel(*refs, reverse, finalize, groups):
    if finalize:
        q_ref, v_ref, lf_ref, of_ref, g_ref, gain_ref, o_ref, st_ref = refs
    else:
        q_ref, v_ref, lf_ref, o_ref, st_ref = refs
    tb = q_ref.shape[0]
    n_heads = q_ref.shape[1] // HG_DK
    C = min(HG_CHUNK, tb)
    n_chunk = tb // C
    n_sub = C // HG_SUB

    @pl.when(groups.pos_of_tile(pl.program_id(1), tb) == 0)
    def _():
        st_ref[...] = jnp.zeros_like(st_ref)

    ri = lax.broadcasted_iota(I32, (C, C), 0)
    ci = lax.broadcasted_iota(I32, (C, C), 1)
    sub_lo = (ri >> 4) << 4
    if reverse:
        inc = ci >= ri
        loc_m = inc & (ci < sub_lo + HG_SUB)
    else:
        inc = ci <= ri
        loc_m = inc & (ci >= sub_lo)
    cum_mat = jnp.where(inc, 1.0, 0.0).astype(BF16)
    loc_mat = jnp.where(loc_m, 1.0, 0.0).astype(BF16)
    row_id = lax.broadcasted_iota(I32, (C, 1), 0)
    sub_row = lax.broadcasted_iota(I32, (HG_SUB, 1), 0)
    lane_c = lax.broadcasted_iota(I32, (HG_SUB, C), 1)
    edge = 0 if reverse else C - 1

    def cumulative(lf):
        h1, h2, h3 = _split3(lf)
        cum = _dot(cum_mat, h1) + _dot(cum_mat, h2) + _dot(cum_mat, h3)
        loc = _dot(loc_mat, h1) + _dot(loc_mat, h2) + _dot(loc_mat, h3)
        return cum, loc

    def head_cols(head):
        return slice(head * HG_DK, (head + 1) * HG_DK)

    def emit(r0, head, o):
        rows, cols = pl.ds(r0, C), head_cols(head)
        if finalize:
            tot = o + of_ref[rows, cols]
            ms = jnp.mean(tot * tot, axis=-1, keepdims=True)
            res = tot * lax.rsqrt(ms + RMS_EPS) * gain_ref[...] * g_ref[rows, cols].astype(F32)
            o_ref[rows, cols] = res.astype(o_ref.dtype)
        else:
            o_ref[rows, cols] = o

    def pairwise_chunk(step, carry, *, head):
        cidx = (n_chunk - 1 - step) if reverse else step
        r0 = pl.multiple_of(cidx * C, C)
        rows, cols = pl.ds(r0, C), head_cols(head)
        lf = lf_ref[rows, cols]
        q = q_ref[rows, cols].astype(F32)
        v = v_ref[rows, cols]
        cum, loc = cumulative(lf)
        kk = 1.0 - jnp.exp(lf)
        total = cum[edge:edge + 1, :]
        st = st_ref[head]
        o_inter = _dot_nt((q * jnp.exp(cum)).astype(BF16), st.astype(BF16))
        k_state = (kk * jnp.exp(total - cum)).astype(BF16)
        st_ref[head] = st * jnp.exp(total) + _dot_tn(v, k_state)
        qh_b = (q * jnp.exp(loc)).astype(BF16)
        blocks = []
        for s in range(n_sub):
            lo = s * HG_SUB
            hi = lo + HG_SUB
            has_off = (s < n_sub - 1) if reverse else (s > 0)
            if has_off:
                if reverse:
                    expo = jnp.where(row_id >= hi, cum[hi:hi + 1, :] - cum, NEG_INF)
                else:
                    expo = jnp.where(row_id < lo, cum[lo - 1:lo, :] - cum, NEG_INF)
                k_off = (kk * jnp.exp(expo)).astype(BF16)
                a_s = _dot_nt(qh_b[lo:hi], k_off)
            else:
                a_s = jnp.zeros((HG_SUB, C), F32)
            loc_s = loc[lo:hi]
            q_s = q[lo:hi]
            for j in range(HG_SUB):
                jj = lo + j
                d = loc_s - loc[jj:jj + 1, :]
                msk = (sub_row <= j) if reverse else (sub_row >= j)
                e = jnp.exp(jnp.where(msk, d, NEG_INF))
                colv = jnp.sum(q_s * e * kk[jj:jj + 1, :], axis=-1, keepdims=True)
                a_s = jnp.where(lane_c == jj, colv, a_s)
            blocks.append(a_s)
        a = jnp.concatenate(blocks, axis=0)
        emit(r0, head, o_inter + _dot(a.astype(BF16), v))
        return carry

    def staged_tile():
        items = [(((n_chunk - 1 - s) if reverse else s) * C, head) for s in range(n_chunk) for head in range(n_heads)]
        lfs = [lf_ref[r0:r0 + C, head_cols(h)] for r0, h in items]
        qs = [q_ref[r0:r0 + C, head_cols(h)].astype(F32) for r0, h in items]
        vs = [v_ref[r0:r0 + C, head_cols(h)] for r0, h in items]
        cums, locs = zip(*[cumulative(lf) for lf in lfs])
        kks = [1.0 - jnp.exp(lf) for lf in lfs]
        totals = [cum[edge:edge + 1, :] for cum in cums]
        a_mats = []
        for q, kk, cum, loc in zip(qs, kks, cums, locs):
            qh_b = (q * jnp.exp(loc)).astype(BF16)
            blocks = []
            for s in range(n_sub):
                lo = s * HG_SUB
                hi = lo + HG_SUB
                if reverse:
                    edge_row = cum[hi:hi + 1, :] if s < n_sub - 1 else jnp.zeros((1, HG_DK), F32)
                    expo = jnp.where(row_id >= lo, edge_row - cum, NEG_INF)
                    tri = lane_c >= lo + sub_row
                else:
                    edge_row = cum[lo - 1:lo, :] if s > 0 else jnp.zeros((1, HG_DK), F32)
                    expo = jnp.where(row_id < hi, edge_row - cum, NEG_INF)
                    tri = lane_c <= lo + sub_row
                k_rel = (kk * jnp.exp(expo)).astype(BF16)
                blocks.append(jnp.where(tri, _dot_nt(qh_b[lo:hi], k_rel), 0.0))
            a_mats.append(jnp.concatenate(blocks, axis=0).astype(BF16))
        intras = [_dot(a, v) for a, v in zip(a_mats, vs)]
        updates = [_dot_tn(v, (kk * jnp.exp(tot - cum)).astype(BF16))
                   for v, kk, tot, cum in zip(vs, kks, totals, cums)]
        q_ins = [(q * jnp.exp(cum)).astype(BF16) for q, cum in zip(qs, cums)]
        states = [st_ref[h] for h in range(n_heads)]
        for (r0, h), q_in, intra, tot, upd in zip(items, q_ins, intras, totals, updates):
            emit(r0, h, _dot_nt(q_in, states[h].astype(BF16)) + intra)
            states[h] = states[h] * jnp.exp(tot) + upd
        for h in range(n_heads):
            st_ref[h] = states[h]

    mild_decay = jnp.min(lf_ref[...]) >= -HG_SAFE_LOG

    @pl.when(mild_decay)
    def _():
        staged_tile()

    @pl.when(jnp.logical_not(mild_decay))
    def _():
        for head in range(n_heads):
            lax.fori_loop(0, n_chunk, functools.partial(pairwise_chunk, head=head), 0)


def _hgrn_pass(q, v, lf, groups, *, reverse, finalize_args=None):
    T = q.shape[0]
    tb = TOKEN_TILE
    finalize = finalize_args is not None

    def tmap(h, i):
        return (groups.reversed_tile(i, tb) if reverse else i, h)

    spec = pl.BlockSpec((tb, HG_DK * HG_HEADS_PER_STEP), tmap)
    in_specs = [spec, spec, spec]
    args = [q, v, lf]
    if finalize:
        o_fwd, gate, gain = finalize_args
        in_specs += [spec, spec, pl.BlockSpec((1, HG_DK), lambda h, i: (0, 0))]
        args += [o_fwd, gate, gain]
    return pl.pallas_call(
        functools.partial(_hgrn_kernel, reverse=reverse, finalize=finalize, groups=groups),
        out_shape=jax.ShapeDtypeStruct((T, HG_WIDTH), BF16 if finalize else F32),
        grid=(HG_HEADS // HG_HEADS_PER_STEP, T // tb),
        in_specs=in_specs,
        out_specs=spec,
        scratch_shapes=[pltpu.VMEM((HG_HEADS_PER_STEP, HG_DK, HG_DK), F32)],
        compiler_params=_cparams(("parallel", "arbitrary")),
        name="hgrn_bwd" if reverse else "hgrn_fwd",
    )(*args)


def _kvproj_kernel(m_ref, w_ref, o_ref):
    o_ref[...] = _dot(m_ref[...].astype(BF16), w_ref[...]).astype(BF16)


def _kvproj(mem, w):
    R, D = mem.shape
    n = w.shape[1]
    tr = 256
    return pl.pallas_call(
        _kvproj_kernel,
        out_shape=jax.ShapeDtypeStruct((R, n), BF16),
        grid=(R // tr,),
        in_specs=[pl.BlockSpec((tr, D), lambda i: (i, 0)), pl.BlockSpec((D, n), lambda i: (0, 0))],
        out_specs=pl.BlockSpec((tr, n), lambda i: (i, 0)),
        compiler_params=_cparams(("parallel",)),
        name="kvproj",
    )(mem, w)


def _mix_xattn_kernel(x0_ref, x1_ref, att_ref, rec_ref, g0_ref, b0_ref, wo_ref, g1_ref, b1_ref, wq_ref, kv_ref,
                      wxo_ref, g2_ref, b2_ref, x2_ref, p_ref, *, tiles_group0):
    D = x0_ref.shape[1]
    dh = D // X_HEADS
    x_tile = _group_rows(x0_ref, x1_ref, tiles_group0)

    def stages(rows):
        mix = _dot(att_ref[rows, :], wo_ref[:ATT_WIDTH, :]) + _dot(rec_ref[rows, :], wo_ref[ATT_WIDTH:, :])
        yield
        xn = _layer_norm(x_tile[rows], g0_ref[...], b0_ref[...])
        x1 = _layer_norm(DEEPNORM_ALPHA * xn + mix, g1_ref[...], b1_ref[...])
        qx = (_dot(x1.astype(BF16), wq_ref[...]) * (dh ** -0.5)).astype(BF16)
        yield
        scores = [_dot_nt(qx[:, h * dh:(h + 1) * dh], kv_ref[0, :, h * dh:(h + 1) * dh]) for h in range(X_HEADS)]
        yield
        heads = []
        for h, s in enumerate(scores):
            s = s - jnp.max(s, axis=-1, keepdims=True)
            p = jnp.exp(s)
            p = p / jnp.sum(p, axis=-1, keepdims=True)
            heads.append(_dot(p.astype(BF16), kv_ref[0, :, D + h * dh:D + (h + 1) * dh]).astype(BF16))
        yield
        xo = _dot(jnp.concatenate(heads, axis=-1), wxo_ref[...])
        yield
        x2 = _layer_norm(DEEPNORM_ALPHA * x1 + xo, g2_ref[...], b2_ref[...])
        x2_ref[rows, :] = x2
        _store_slabs(p_ref, _pack_bf16_pair(x2[:, :D // 2], x2[:, D // 2:]), row0=rows.start)

    _staggered(stages, x_tile.shape[0])


def _mix_xattn(x0, x1, att, rec, g0, b0, wo, g1, b1, wq, kv, wxo, g2, b2, seq_of_tile):
    D = x0.shape[1]
    T = x0.shape[0] + x1.shape[0]
    tm = TOKEN_TILE
    tiles_group0 = x0.shape[0] // tm
    n_mem = kv.shape[1]
    row = lambda i: (i, 0)
    const = lambda i: (0, 0)
    vec = pl.BlockSpec((1, D), const)
    return pl.pallas_call(
        functools.partial(_mix_xattn_kernel, tiles_group0=tiles_group0),
        out_shape=(jax.ShapeDtypeStruct((T, D), F32), jax.ShapeDtypeStruct((T * SLAB, LANES), U32)),
        grid=(T // tm,),
        in_specs=_group_specs(tm, D, tiles_group0) + [
            pl.BlockSpec((tm, ATT_WIDTH), row),
            pl.BlockSpec((tm, HG_WIDTH), row),
            vec, vec,
            pl.BlockSpec((ATT_WIDTH + HG_WIDTH, D), const),
            vec, vec,
            pl.BlockSpec((D, D), const),
            pl.BlockSpec((1, n_mem, 2 * D), lambda i: (seq_of_tile(i), 0, 0)),
            pl.BlockSpec((D, D), const),
            vec, vec,
        ],
        out_specs=(pl.BlockSpec((tm, D), row), pl.BlockSpec((tm * SLAB, LANES), row)),
        compiler_params=_cparams(("parallel",)),
        name="mix_xattn",
    )(x0, x1, att, rec, g0, b0, wo, g1, b1, wq, kv, wxo, g2, b2)


def _router_kernel(x_ref, wt_ref, b_ref, idx_ref, gate_ref, rank_ref, cnt_ref, run_ref):
    tm = x_ref.shape[0]

    @pl.when(pl.program_id(0) == 0)
    def _():
        run_ref[...] = jnp.zeros_like(run_ref)

    xh, xl, _ = _split3(x_ref[...])
    wh, wl, _ = _split3(wt_ref[...])
    logits = _dot_nt(wh, xh) + _dot_nt(wh, xl) + _dot_nt(wl, xh) + b_ref[...]
    e_id = lax.broadcasted_iota(I32, (N_EXPERTS, tm), 0)
    work = logits
    vals, idxs = [], []
    chosen = jnp.zeros((N_EXPERTS, tm), F32)
    for _ in range(TOP_K):
        m = jnp.max(work, axis=0, keepdims=True)
        idx = jnp.min(jnp.where(work == m, e_id, N_EXPERTS), axis=0, keepdims=True)
        hit = e_id == idx
        chosen = jnp.where(hit, 1.0, chosen)
        work = jnp.where(hit, NEG_INF, work)
        vals.append(m)
        idxs.append(idx)
    ex = [jnp.exp(v - vals[0]) for v in vals]
    den = ex[0] + ex[1] + ex[2] + ex[3]
    t_r = lax.broadcasted_iota(I32, (tm, tm), 0)
    t_c = lax.broadcasted_iota(I32, (tm, tm), 1)
    before = jnp.where(t_r < t_c, 1.0, 0.0).astype(BF16)
    rank_e = _dot(chosen.astype(BF16), before) + run_ref[...]
    run_ref[...] = run_ref[...] + jnp.sum(chosen, axis=1, keepdims=True)
    for k in range(TOP_K):
        idx_ref[k:k + 1, :] = idxs[k]
        gate_ref[k:k + 1, :] = ex[k] / den
        rank_ref[k:k + 1, :] = jnp.sum(jnp.where(e_id == idxs[k], rank_e, 0.0), axis=0, keepdims=True).astype(I32)
    cnt_ref[...] = run_ref[...].astype(I32)


def _router(x2, w_router_t, b_router):
    T, D = x2.shape
    tm = TOKEN_TILE
    tok = pl.BlockSpec((TOP_K, tm), lambda i: (0, i))
    return pl.pallas_call(
        _router_kernel,
        out_shape=(jax.ShapeDtypeStruct((TOP_K, T), I32), jax.ShapeDtypeStruct((TOP_K, T), F32),
                   jax.ShapeDtypeStruct((TOP_K, T), I32), jax.ShapeDtypeStruct((N_EXPERTS, 1), I32)),
        grid=(T // tm,),
        in_specs=[pl.BlockSpec((tm, D), lambda i: (i, 0)),
                  pl.BlockSpec((N_EXPERTS, D), lambda i: (0, 0)),
                  pl.BlockSpec((N_EXPERTS, 1), lambda i: (0, 0))],
        out_specs=(tok, tok, tok, pl.BlockSpec((N_EXPERTS, 1), lambda i: (0, 0))),
        scratch_shapes=[pltpu.VMEM((N_EXPERTS, 1), F32)],
        compiler_params=_cparams(("arbitrary",)),
        name="router",
    )(x2, w_router_t, b_router)


def _dispatch_kernel(dest_ref, p_ref, zero_hbm, out_hbm, sem):
    del zero_hbm
    td = DISPATCH_TILE

    def start(t, c):
        for k in range(TOP_K):
            d = dest_ref[0, 0, k * td + t]
            pltpu.make_async_copy(p_ref.at[t], out_hbm.at[d], sem).start(priority=k % DMA_PRIORITIES)
        return c

    lax.fori_loop(0, td, start, 0, unroll=DMA_ISSUE_UNROLL)
    everything = out_hbm.at[pl.ds(0, TOP_K * td)]
    pltpu.make_async_copy(everything, everything, sem).wait()


def _dispatch(dest_tiles, p, zeros):
    T = p.shape[0]
    td = DISPATCH_TILE
    return pl.pallas_call(
        _dispatch_kernel,
        out_shape=jax.ShapeDtypeStruct(zeros.shape, zeros.dtype),
        grid=(T // td,),
        in_specs=[pl.BlockSpec((1, 1, TOP_K * td), lambda i: (i, 0, 0), memory_space=pltpu.SMEM),
                  pl.BlockSpec((td,) + p.shape[1:], lambda i: (i, 0, 0)),
                  pl.BlockSpec(memory_space=pl.ANY)],
        out_specs=pl.BlockSpec(memory_space=pl.ANY),
        scratch_shapes=[pltpu.SemaphoreType.DMA],
        input_output_aliases={2: 0},
        compiler_params=_cparams(("arbitrary",)),
        name="dispatch",
    )(dest_tiles, p, zeros)


def _expert_kernel(blk_e_ref, nvalid_ref, p_ref, wu_ref, bu_ref, wd_ref, bd_ref, y_ref, wu_b, wd_b):
    i = pl.program_id(0)
    dff = wd_ref.shape[1]
    half = SLAB * LANES
    live = i < nvalid_ref[0]
    new_expert = (i == 0) | (blk_e_ref[i] != blk_e_ref[jnp.maximum(i - 1, 0)])

    @pl.when(live & new_expert)
    def _():
        wu_b[...] = wu_ref[0].astype(BF16)
        wd_b[...] = wd_ref[0].astype(BF16)

    @pl.when(live)
    def _():
        def stages(rows):
            lo, hi = _unpack_bf16_pair(_load_slabs(p_ref, rows.start, rows.stop - rows.start))
            lo = lo.astype(BF16)
            hi = hi.astype(BF16)

            def up(c0):
                return (_dot(lo, wu_b[:half, c0:c0 + dff]) + _dot(hi, wu_b[half:, c0:c0 + dff])
                        + bu_ref[0, :, c0:c0 + dff])

            glu = up(0)
            yield
            lin = up(dff)
            yield
            glu = jnp.minimum(glu, SWIGLU_LIMIT)
            lin = jnp.clip(lin, -SWIGLU_LIMIT, SWIGLU_LIMIT)
            a = glu * _sigmoid(SWIGLU_ALPHA * glu) * (lin + 1.0)
            y = _dot(a.astype(BF16), wd_b[...]) + bd_ref[0]
            yield
            _store_slabs(y_ref, _pack_bf16_pair(y[:, :half], y[:, half:]), row0=rows.start)

        _staggered(stages, p_ref.shape[0] // SLAB)

    @pl.when(i >= nvalid_ref[0])
    def _():
        y_ref[...] = jnp.zeros_like(y_ref)


def _experts(blk_expert, n_valid, ps, wu, bu, wd, bd):
    R = ps.shape[0] // SLAB
    te = EXPERT_TILE
    E, D, two_f = wu.shape
    dff = wd.shape[1]
    grid_spec = pltpu.PrefetchScalarGridSpec(
        num_scalar_prefetch=2,
        grid=(R // te,),
        in_specs=[
            pl.BlockSpec((te * SLAB, LANES), lambda i, be, nv: (i, 0)),
            pl.BlockSpec((1, D, two_f), lambda i, be, nv: (be[i], 0, 0)),
            pl.BlockSpec((1, 1, two_f), lambda i, be, nv: (be[i], 0, 0)),
            pl.BlockSpec((1, dff, D), lambda i, be, nv: (be[i], 0, 0)),
            pl.BlockSpec((1, 1, D), lambda i, be, nv: (be[i], 0, 0)),
        ],
        out_specs=pl.BlockSpec((te * SLAB, LANES), lambda i, be, nv: (i, 0)),
        scratch_shapes=[pltpu.VMEM((D, two_f), BF16), pltpu.VMEM((dff, D), BF16)],
    )
    return pl.pallas_call(
        _expert_kernel,
        out_shape=jax.ShapeDtypeStruct((R * SLAB, LANES), U32),
        grid_spec=grid_spec,
        compiler_params=_cparams(("arbitrary",)),
        name="experts",
    )(blk_expert, n_valid, ps, wu, bu, wd, bd)


def _combine_kernel(dest_ref, dest_next_ref, ys_hbm, x2_ref, gate_ref, g_ref, b_ref, o_ref, buf, sems):
    tc = x2_ref.shape[0]
    half = SLAB * LANES
    i = pl.program_id(0)
    slot = i % 2

    def start_tile(idx_ref, s):
        def body(t, c):
            rows = pl.ds(pl.multiple_of(t * SLAB, SLAB), SLAB)
            for k in range(TOP_K):
                d = idx_ref[0, 0, k * tc + t]
                pltpu.make_async_copy(ys_hbm.at[d], buf.at[s, k, rows], sems.at[s]).start(
                    priority=k % DMA_PRIORITIES)
            return c
        lax.fori_loop(0, tc, body, 0, unroll=DMA_ISSUE_UNROLL)

    @pl.when(i == 0)
    def _():
        start_tile(dest_ref, 0)

    for s in range(2):
        @pl.when((i + 1 < pl.num_programs(0)) & (slot == s))
        def _():
            start_tile(dest_next_ref, 1 - s)

    for s in range(2):
        @pl.when(slot == s)
        def _():
            pltpu.make_async_copy(buf.at[s], buf.at[s], sems.at[s]).wait()

    gates = gate_ref[...]
    acc_lo = jnp.zeros((tc, half), F32)
    acc_hi = jnp.zeros((tc, half), F32)
    for k in range(TOP_K):
        lo, hi = _unpack_bf16_pair(_load_slabs(buf.at[slot, k]))
        gk = gates[:, k:k + 1]
        acc_lo = acc_lo + gk * lo
        acc_hi = acc_hi + gk * hi
    ffn = jnp.concatenate([acc_lo, acc_hi], axis=-1)
    o_ref[...] = _layer_norm(DEEPNORM_ALPHA * x2_ref[...] + ffn, g_ref[...], b_ref[...])


def _combine(dest_tiles, ys, x2, gates, g, b, *, tile0, n_tiles):
    D = x2.shape[1]
    tc = COMBINE_TILE
    return pl.pallas_call(
        _combine_kernel,
        out_shape=jax.ShapeDtypeStruct((n_tiles * tc, D), F32),
        grid=(n_tiles,),
        in_specs=[pl.BlockSpec((1, 1, TOP_K * tc), lambda i: (tile0 + i, 0, 0), memory_space=pltpu.SMEM),
                  pl.BlockSpec((1, 1, TOP_K * tc), lambda i: (tile0 + jnp.minimum(i + 1, n_tiles - 1), 0, 0),
                               memory_space=pltpu.SMEM),
                  pl.BlockSpec(memory_space=pl.ANY),
                  pl.BlockSpec((tc, D), lambda i: (tile0 + i, 0)),
                  pl.BlockSpec((tc, TOP_K), lambda i: (tile0 + i, 0)),
                  pl.BlockSpec((1, D), lambda i: (0, 0)),
                  pl.BlockSpec((1, D), lambda i: (0, 0))],
        out_specs=pl.BlockSpec((tc, D), lambda i: (i, 0)),
        scratch_shapes=[pltpu.VMEM((2, TOP_K, tc * SLAB, LANES), U32), pltpu.SemaphoreType.DMA((2,))],
        compiler_params=_cparams(("arbitrary",)),
        name="combine",
    )(dest_tiles, dest_tiles, ys, x2, gates, g, b)


def _rope_tables(seq_len):
    rows = seq_len // GRID_W
    row = jnp.repeat(jnp.arange(rows, dtype=F32), GRID_W)
    colp = (jnp.arange(seq_len) % GRID_W).astype(F32)
    axis_dim = ATT_HEAD_DIM // 2
    inv_freq = ROPE_THETA ** (-jnp.arange(0, axis_dim, 2, dtype=F32) / axis_dim)
    ang = jnp.concatenate([row[:, None] * inv_freq, colp[:, None] * inv_freq], axis=-1)
    cos, sin = jnp.cos(ang), jnp.sin(ang)
    cos_h = jnp.concatenate([cos, cos], axis=-1)
    sin_h = jnp.concatenate([-sin, sin], axis=-1)
    return jnp.tile(cos_h, (1, LANES // ATT_HEAD_DIM)), jnp.tile(sin_h, (1, LANES // ATT_HEAD_DIM))


def _tiles_of(idx_t, tile):
    K, T = idx_t.shape
    return idx_t.reshape(K, T // tile, tile).transpose(1, 0, 2).reshape(T // tile, 1, K * tile)


def kernel(x_prompt, x_sample, mem_prompt, mem_sample, ln_in_g, ln_in_b, w_in, q_norm_g, k_norm_g, hg_lb_fwd, hg_lb_bwd, hg_norm_g, w_out, ln1_g, ln1_b, w_xq, w_xkv, w_xo, ln2_g, ln2_b, w_router, b_router, w_up, b_up, w_down, b_down, ln3_g, ln3_b):
    Bp, Sp, D = x_prompt.shape
    Bs, Ss, _ = x_sample.shape
    Tp, Ts = Bp * Sp, Bs * Ss
    T = Tp + Ts
    tm = TOKEN_TILE
    assert Sp % tm == 0 and Ss % Sp == 0
    n_mem = mem_prompt.shape[1]
    layer = 0

    x0, x1 = x_prompt.reshape(Tp, D), x_sample.reshape(Ts, D)
    mem = jnp.concatenate([mem_prompt.reshape(Bp * n_mem, D), mem_sample.reshape(Bs * n_mem, D)], axis=0)

    def deinterleave(a):
        lead = a.shape[:-1]
        a = a.reshape(*lead, -1, ATT_HEAD_DIM // 2, 2)
        return jnp.swapaxes(a, -1, -2).reshape(*lead, -1)

    n_qk = ATT_WIDTH + ATT_KV_WIDTH
    w_in_l = w_in[layer]
    w_in_b = jnp.concatenate([deinterleave(w_in_l[:, :n_qk]), w_in_l[:, n_qk:]], axis=1).astype(BF16)
    pair = LANES // ATT_HEAD_DIM
    qg = jnp.tile(deinterleave(q_norm_g[layer]), pair).reshape(1, LANES)
    kg = jnp.tile(deinterleave(k_norm_g[layer]), pair).reshape(1, LANES)

    def lower_bound(logits):
        sm = jax.nn.softmax(logits.astype(F32), axis=0)
        return (jnp.cumsum(sm, axis=0)[layer + 1] - sm[0]).reshape(1, HG_WIDTH)

    lbf, lbb = lower_bound(hg_lb_fwd), lower_bound(hg_lb_bwd)
    cos_t, sin_t = _rope_tables(max(Sp, Ss))
    vec = lambda a: a.reshape(1, -1)

    groups = _Groups(Bp, Sp, Bs, Ss)
    pos_block = lambda i: groups.pos_of_tile(i, tm)
    seq_of_tile = lambda i: groups.seq_of_tile(i, tm)

    q, k, v, hq, hi, hg, lff, lfb = _inproj(x0, x1, vec(ln_in_g), vec(ln_in_b), w_in_b, cos_t, sin_t, qg, kg,
                                           lbf, lbb, pos_block)

    att = _attention(q, k, v, groups)
    o_fwd = _hgrn_pass(hq, hi, lff, groups, reverse=False)
    gain = hg_norm_g[layer].reshape(1, HG_DK)
    rec = _hgrn_pass(hq, hi, lfb, groups, reverse=True, finalize_args=(o_fwd, hg, gain))

    kv = _kvproj(mem, w_xkv[layer].astype(BF16)).reshape(Bp + Bs, n_mem, 2 * D)
    x2, packed = _mix_xattn(x0, x1, att, rec, vec(ln_in_g), vec(ln_in_b), w_out[layer].astype(BF16),
                            vec(ln1_g[layer]), vec(ln1_b[layer]), w_xq[layer].astype(BF16), kv,
                            w_xo[layer].astype(BF16), vec(ln2_g[layer]), vec(ln2_b[layer]), seq_of_tile)

    idx_t, gate_t, rank_t, counts = _router(x2, w_router[layer].T, b_router[layer].reshape(N_EXPERTS, 1))

    te = EXPERT_TILE
    counts = counts.reshape(N_EXPERTS)
    padded = (counts + te - 1) // te * te
    pad_end = jnp.cumsum(padded)
    pad_start = pad_end - padded
    n_blk = (T * TOP_K + N_EXPERTS * (te - 1) + te - 1) // te
    e_ids = jnp.arange(N_EXPERTS, dtype=I32)
    start_of = jnp.sum(jnp.where(idx_t[None] == e_ids[:, None, None], pad_start[:, None, None], 0), axis=0)
    dest_t = start_of + rank_t
    blk_row0 = jnp.arange(n_blk, dtype=I32) * te
    blk_expert = jnp.minimum(jnp.sum((pad_end[None, :] <= blk_row0[:, None]).astype(I32), axis=1), N_EXPERTS - 1)
    n_valid = (pad_end[-1] // te).astype(I32).reshape(1)

    n_rows = n_blk * te
    sorted_rows = _dispatch(_tiles_of(dest_t, DISPATCH_TILE), packed.reshape(T, SLAB, LANES),
                            jnp.zeros((n_rows, SLAB, LANES), U32))
    ys = _experts(blk_expert, n_valid, sorted_rows.reshape(n_rows * SLAB, LANES), w_up[layer],
                  b_up[layer].reshape(N_EXPERTS, 1, -1), w_down[layer], b_down[layer].reshape(N_EXPERTS, 1, -1))
    ys = ys.reshape(n_rows, SLAB, LANES)

    dest_c = _tiles_of(dest_t, COMBINE_TILE)
    gates = gate_t.T
    tc = COMBINE_TILE
    y_p = _combine(dest_c, ys, x2, gates, vec(ln3_g[layer]), vec(ln3_b[layer]), tile0=0, n_tiles=Tp // tc)
    y_s = _combine(dest_c, ys, x2, gates, vec(ln3_g[layer]), vec(ln3_b[layer]), tile0=Tp // tc, n_tiles=Ts // tc)
    return y_p.reshape(Bp, Sp, D), y_s.reshape(Bs, Ss, D)
```

```python
import functools

import jax
import jax.numpy as jnp
import numpy as np
from jax import lax
from jax.experimental import pallas as pl
from jax.experimental.pallas import tpu as pltpu

F32 = jnp.float32
BF16 = jnp.bfloat16
I32 = jnp.int32
U32 = jnp.uint32

GRID_W = 64
ATT_HEADS = 8
ATT_KV_HEADS = 2
ATT_HEAD_DIM = 64
ATT_WIDTH = ATT_HEADS * ATT_HEAD_DIM
ATT_KV_WIDTH = ATT_KV_HEADS * ATT_HEAD_DIM
ROPE_THETA = 10000.0
HG_HEADS = 4
HG_DK = 128
HG_WIDTH = HG_HEADS * HG_DK
X_HEADS = 4
N_EXPERTS = 32
TOP_K = 4
SWIGLU_LIMIT = 7.0
SWIGLU_ALPHA = 1.702
LN_EPS = 1e-5
RMS_EPS = 1e-6
DEPTH = 1
DEEPNORM_ALPHA = (2.0 * DEPTH) ** 0.25

LANES = 128
VMEM_LIMIT_BYTES = 56 * 1024 * 1024
DMA_PRIORITIES = 2
DMA_ISSUE_UNROLL = 4

TOKEN_TILE = 512
ATT_Q_TILE = 256
ATT_K_TILE = 4096
HG_CHUNK = 64
HG_SUB = 16
HG_HEADS_PER_STEP = 4

ATT_SAFE_SCORE = 60.0
ATT_NORM_SLACK = 1.02
HG_SAFE_LOG = 5.0
EXPERT_TILE = 512
COMBINE_TILE = 256
DISPATCH_TILE = 512
ROW_PARTS = 2
NEG_INF = float("-inf")


def _cparams(sem):
    return pltpu.CompilerParams(dimension_semantics=sem, vmem_limit_bytes=VMEM_LIMIT_BYTES)


def _layer_norm(x, g, b):
    mu = jnp.mean(x, axis=-1, keepdims=True)
    xc = x - mu
    var = jnp.mean(xc * xc, axis=-1, keepdims=True)
    return xc * lax.rsqrt(var + LN_EPS) * g + b


def _sigmoid(x):
    return 1.0 / (1.0 + jnp.exp(-x))


def _dot(a, b):
    return jnp.dot(a, b, preferred_element_type=F32)


def _dot_nt(a, b):
    return lax.dot_general(a, b, (((1,), (1,)), ((), ())), preferred_element_type=F32)


def _dot_tn(a, b):
    return lax.dot_general(a, b, (((0,), (0,)), ((), ())), preferred_element_type=F32)


def _split3(x):
    h1 = x.astype(BF16)
    r1 = x - h1.astype(F32)
    h2 = r1.astype(BF16)
    h3 = (r1 - h2.astype(F32)).astype(BF16)
    return h1, h2, h3


def _pack_bf16_pair(lo, hi):
    lo_b = pltpu.bitcast(lo.astype(BF16).astype(F32), U32)
    hi_b = pltpu.bitcast(hi.astype(BF16).astype(F32), U32)
    return (hi_b & jnp.uint32(0xFFFF0000)) | (lo_b >> 16)


def _unpack_bf16_pair(p):
    lo = pltpu.bitcast(p << 16, F32)
    hi = pltpu.bitcast(p & jnp.uint32(0xFFFF0000), F32)
    return lo, hi


SLAB = 4


def _store_slabs(ref, value, row0=0):
    n = value.shape[0]
    for s in range(SLAB):
        ref[pl.ds(row0 * SLAB + s, n, stride=SLAB), :] = value[:, s * LANES:(s + 1) * LANES]


def _load_slabs(ref, row0=0, n=None):
    n = ref.shape[0] // SLAB if n is None else n
    return jnp.concatenate([ref[pl.ds(row0 * SLAB + s, n, stride=SLAB), :] for s in range(SLAB)], axis=-1)


def _row_parts(n_rows):
    part = n_rows // ROW_PARTS
    return [slice(r * part, (r + 1) * part) for r in range(ROW_PARTS)]


def _staggered(stage_fn, n_rows):
    pending = [stage_fn(rows) for rows in _row_parts(n_rows)]
    while pending:
        for g in list(pending):
            try:
                next(g)
            except StopIteration:
                pending.remove(g)


def _group_rows(x0_ref, x1_ref, tiles_group0):
    return jnp.where(pl.program_id(0) < tiles_group0, x0_ref[...], x1_ref[...])


def _group_specs(tile, width, tiles_group0):
    return [pl.BlockSpec((tile, width), lambda i: (jnp.minimum(i, tiles_group0 - 1), 0)),
            pl.BlockSpec((tile, width), lambda i: (jnp.maximum(i - tiles_group0, 0), 0))]


def _inproj_kernel(x0_ref, x1_ref, g_ref, b_ref, w_ref, cos_ref, sin_ref, qg_ref, kg_ref, lbf_ref, lbb_ref,
                   q_out, k_out, v_out, hq_out, hi_out, hg_out, lff_out, lfb_out, *, tiles_group0):
    xn = _layer_norm(_group_rows(x0_ref, x1_ref, tiles_group0), g_ref[...], b_ref[...])
    xb = xn.astype(BF16)
    cos = cos_ref[...]
    sin = sin_ref[...]
    tm = xb.shape[0]

    lane = lax.broadcasted_iota(I32, (tm, LANES), 1)
    first_half = (lane & (ATT_HEAD_DIM - 1)) < (ATT_HEAD_DIM // 2)
    r = lax.broadcasted_iota(I32, (LANES, LANES), 0)
    c = lax.broadcasted_iota(I32, (LANES, LANES), 1)
    head_ones = jnp.where((r >> 6) == (c >> 6), 1.0, 0.0).astype(BF16)

    def normed_rope(u, gain, scale):
        sq = u * u
        s_hi = sq.astype(BF16)
        s_lo = (sq - s_hi.astype(F32)).astype(BF16)
        ms = (_dot(s_hi, head_ones) + _dot(s_lo, head_ones)) * (1.0 / ATT_HEAD_DIM)
        un = u * lax.rsqrt(ms + RMS_EPS) * gain
        rot = jnp.where(first_half, pltpu.roll(un, LANES - 32, 1), pltpu.roll(un, 32, 1))
        return (un * cos + rot * sin) * scale

    n_qkv = ATT_WIDTH + 2 * ATT_KV_WIDTH
    u_att = [_dot(xb, w_ref[:, c:c + 2 * LANES]) for c in range(0, n_qkv, 2 * LANES)]
    col = n_qkv

    def hg_proj():
        nonlocal col
        u = _dot(xb, w_ref[:, col:col + HG_WIDTH])
        col += HG_WIDTH
        return u

    u = hg_proj()
    hq_out[...] = (u * _sigmoid(u)).astype(BF16)
    for j in range(ATT_WIDTH // (2 * LANES)):
        for h in range(2):
            c0 = (2 * j + h) * LANES
            q_out[:, c0:c0 + LANES] = normed_rope(u_att[j][:, h * LANES:(h + 1) * LANES], qg_ref[...],
                                                  ATT_HEAD_DIM ** -0.5).astype(BF16)
        u = hg_proj()
        lb = (lbf_ref, lbb_ref)[j][...]
        (lff_out, lfb_out)[j][...] = jnp.log(lb + (1.0 - lb) * _sigmoid(u))
    k_out[...] = normed_rope(u_att[-1][:, :LANES], kg_ref[...], 1.0).astype(BF16)
    v_out[:, :LANES] = u_att[-1][:, LANES:].astype(BF16)
    v_out[:, LANES:] = jnp.ones((tm, LANES), BF16)
    hi_out[...] = hg_proj().astype(BF16)
    u = hg_proj()
    hg_out[...] = (u * _sigmoid(u)).astype(BF16)


def _inproj(x0, x1, g, b, w, cos_t, sin_t, qg, kg, lbf, lbb, pos_block):
    D = x0.shape[1]
    T = x0.shape[0] + x1.shape[0]
    tm = TOKEN_TILE
    tiles_group0 = x0.shape[0] // tm
    n_in = w.shape[1]
    row = lambda i: (i, 0)
    const = lambda i: (0, 0)
    out_shape = (
        jax.ShapeDtypeStruct((T, ATT_WIDTH), BF16),
        jax.ShapeDtypeStruct((T, ATT_KV_WIDTH), BF16),
        jax.ShapeDtypeStruct((T, 2 * ATT_KV_WIDTH), BF16),
        jax.ShapeDtypeStruct((T, HG_WIDTH), BF16),
        jax.ShapeDtypeStruct((T, HG_WIDTH), BF16),
        jax.ShapeDtypeStruct((T, HG_WIDTH), BF16),
        jax.ShapeDtypeStruct((T, HG_WIDTH), F32),
        jax.ShapeDtypeStruct((T, HG_WIDTH), F32),
    )
    out_specs = (
        pl.BlockSpec((tm, ATT_WIDTH), row),
        pl.BlockSpec((tm, ATT_KV_WIDTH), row),
        pl.BlockSpec((tm, 2 * ATT_KV_WIDTH), row),
        pl.BlockSpec((tm, HG_WIDTH), row),
        pl.BlockSpec((tm, HG_WIDTH), row),
        pl.BlockSpec((tm, HG_WIDTH), row),
        pl.BlockSpec((tm, HG_WIDTH), row),
        pl.BlockSpec((tm, HG_WIDTH), row),
    )
    return pl.pallas_call(
        functools.partial(_inproj_kernel, tiles_group0=tiles_group0),
        out_shape=out_shape,
        grid=(T // tm,),
        in_specs=_group_specs(tm, D, tiles_group0) + [
            pl.BlockSpec((1, D), const),
            pl.BlockSpec((1, D), const),
            pl.BlockSpec((D, n_in), const),
            pl.BlockSpec((tm, LANES), lambda i: (pos_block(i), 0)),
            pl.BlockSpec((tm, LANES), lambda i: (pos_block(i), 0)),
            pl.BlockSpec((1, LANES), const),
            pl.BlockSpec((1, LANES), const),
            pl.BlockSpec((1, HG_WIDTH), const),
            pl.BlockSpec((1, HG_WIDTH), const),
        ],
        out_specs=out_specs,
        compiler_params=_cparams(("parallel",)),
        name="inproj",
    )(x0, x1, g, b, w, cos_t, sin_t, qg, kg, lbf, lbb)


class _Groups:
    def __init__(self, n0, s0, n1, s1):
        self.n0, self.s0, self.n1, self.s1 = n0, s0, n1, s1
        self.t0 = n0 * s0
        self.total = self.t0 + n1 * s1

    def split(self, i, tile):
        nt0 = self.t0 // tile
        per0, per1 = self.s0 // tile, self.s1 // tile
        in1 = i >= nt0
        j1 = jnp.maximum(i - nt0, 0)
        seq = jnp.where(in1, self.n0 + j1 // per1, i // per0)
        pos = jnp.where(in1, j1 % per1, i % per0)
        return in1, seq, pos, jnp.where(in1, per1, per0)

    def seq_of_tile(self, i, tile):
        return self.split(i, tile)[1]

    def pos_of_tile(self, i, tile):
        return self.split(i, tile)[2]

    def reversed_tile(self, i, tile):
        _, _, pos, per = self.split(i, tile)
        return i - pos + (per - 1 - pos)


def _attention_kernel(*refs, n_kv_blocks, groups):
    bound_ref, q_ref = refs[:2]
    k_refs = refs[2:2 + n_kv_blocks]
    v_refs = refs[2 + n_kv_blocks:2 + 2 * n_kv_blocks]
    o_ref, out_ref = refs[2 + 2 * n_kv_blocks:]
    tq = q_ref.shape[0]
    kv_rows = k_refs[0].shape[0]
    tk = min(ATT_K_TILE, kv_rows)
    n_k = kv_rows // tk
    group = ATT_HEADS // ATT_KV_HEADS
    rows = group * tq
    lane = lax.broadcasted_iota(I32, (tq, LANES), 1)
    low_half = lane < ATT_HEAD_DIM
    in_group1 = groups.split(pl.program_id(0), tq)[0]
    extra_trips = jnp.where(in_group1, n_k, 0)
    unshifted_ok = bound_ref[0] <= ATT_SAFE_SCORE

    for kvh in range(ATT_KV_HEADS):
        keep = low_half if kvh == 0 else jnp.logical_not(low_half)
        parts = []
        for g in range(group):
            c0 = kvh * 2 * LANES + (g // 2) * LANES
            pair = q_ref[:, c0:c0 + LANES].astype(F32)
            if g % 2 != kvh:
                pair = pltpu.roll(pair, ATT_HEAD_DIM, 1)
            parts.append(jnp.where(keep, pair, 0.0).astype(BF16))
        qp = jnp.concatenate(parts, axis=0)

        def run(make_body, carry):
            carry = lax.fori_loop(0, n_k, make_body(k_refs[0], v_refs[0]), carry)
            for kb in range(1, n_kv_blocks):
                carry = lax.fori_loop(0, extra_trips, make_body(k_refs[kb], v_refs[kb]), carry)
            return carry

        @pl.when(unshifted_ok)
        def _():
            def make_body(k_ref, v_ref):
                def body(kc, acc):
                    r0 = pl.multiple_of(kc * tk, tk)
                    p = jnp.exp(_dot_nt(qp, k_ref[pl.ds(r0, tk), :]))
                    return acc + _dot(p.astype(BF16), v_ref[pl.ds(r0, tk), :])
                return body

            acc = run(make_body, jnp.zeros((rows, 2 * LANES), F32))
            out_ref[...] = acc[:, :LANES] / acc[:, LANES:]

        @pl.when(jnp.logical_not(unshifted_ok))
        def _():
            def make_body(k_ref, v_ref):
                def body(kc, carry):
                    m, l, acc = carry
                    r0 = pl.multiple_of(kc * tk, tk)
                    s = _dot_nt(qp, k_ref[pl.ds(r0, tk), :])
                    m_new = jnp.maximum(m, jnp.max(s, axis=-1, keepdims=True))
                    alpha = jnp.exp(m - m_new)
                    p = jnp.exp(s - m_new)
                    l_new = alpha * l + jnp.sum(p, axis=-1, keepdims=True)
                    acc_new = alpha * acc + _dot(p.astype(BF16), v_ref[pl.ds(r0, tk), :LANES])
                    return m_new, l_new, acc_new
                return body

            init = (jnp.full((rows, 1), NEG_INF, F32), jnp.zeros((rows, 1), F32), jnp.zeros((rows, LANES), F32))
            m, l, acc = run(make_body, init)
            out_ref[...] = acc / l

        out = out_ref[...]
        for j in range(group // 2):
            a = out[(2 * j) * tq:(2 * j + 1) * tq]
            b = out[(2 * j + 1) * tq:(2 * j + 2) * tq]
            if kvh == 1:
                a = pltpu.roll(a, ATT_HEAD_DIM, 1)
            else:
                b = pltpu.roll(b, ATT_HEAD_DIM, 1)
            c0 = kvh * 2 * LANES + j * LANES
            o_ref[:, c0:c0 + LANES] = jnp.where(low_half, a, b).astype(BF16)


def _attention(score_bound, q, k, v, groups):
    T = q.shape[0]
    tq = ATT_Q_TILE
    kv_rows = groups.s0
    n_kv_blocks = groups.s1 // groups.s0

    def kv_map(j):
        def index_map(i):
            in1, seq, _, _ = groups.split(i, tq)
            first = jnp.where(in1, groups.t0 // kv_rows + (seq - groups.n0) * n_kv_blocks, seq)
            return (first + jnp.where(in1, j, 0), 0)
        return index_map

    k_specs = [pl.BlockSpec((kv_rows, ATT_KV_WIDTH), kv_map(j)) for j in range(n_kv_blocks)]
    v_specs = [pl.BlockSpec((kv_rows, 2 * ATT_KV_WIDTH), kv_map(j)) for j in range(n_kv_blocks)]
    return pl.pallas_call(
        functools.partial(_attention_kernel, n_kv_blocks=n_kv_blocks, groups=groups),
        out_shape=jax.ShapeDtypeStruct((T, ATT_WIDTH), BF16),
        grid=(T // tq,),
        in_specs=[pl.BlockSpec(memory_space=pltpu.SMEM),
                  pl.BlockSpec((tq, ATT_WIDTH), lambda i: (i, 0))] + k_specs + v_specs,
        out_specs=pl.BlockSpec((tq, ATT_WIDTH), lambda i: (i, 0)),
        scratch_shapes=[pltpu.VMEM((ATT_HEADS // ATT_KV_HEADS * tq, LANES), F32)],
        compiler_params=_cparams(("parallel",)),
        name="attention",
    )(score_bound, q, *([k] * n_kv_blocks), *([v] * n_kv_blocks))


def _hgrn_kernel(*refs, reverse, finalize, groups):
    if finalize:
        q_ref, v_ref, lf_ref, of_ref, g_ref, gain_ref, o_ref, st_ref = refs
    else:
        q_ref, v_ref, lf_ref, o_ref, st_ref = refs
    tb = q_ref.shape[0]
    n_heads = q_ref.shape[1] // HG_DK
    C = min(HG_CHUNK, tb)
    n_chunk = tb // C
    n_sub = C // HG_SUB

    @pl.when(groups.pos_of_tile(pl.program_id(1), tb) == 0)
    def _():
        st_ref[...] = jnp.zeros_like(st_ref)

    ri = lax.broadcasted_iota(I32, (C, C), 0)
    ci = lax.broadcasted_iota(I32, (C, C), 1)
    sub_lo = (ri >> 4) << 4
    if reverse:
        inc = ci >= ri
        loc_m = inc & (ci < sub_lo + HG_SUB)
    else:
        inc = ci <= ri
        loc_m = inc & (ci >= sub_lo)
    cum_mat = jnp.where(inc, 1.0, 0.0).astype(BF16)
    loc_mat = jnp.where(loc_m, 1.0, 0.0).astype(BF16)
    row_id = lax.broadcasted_iota(I32, (C, 1), 0)
    sub_row = lax.broadcasted_iota(I32, (HG_SUB, 1), 0)
    lane_c = lax.broadcasted_iota(I32, (HG_SUB, C), 1)
    edge = 0 if reverse else C - 1

    def cumulative(lf):
        h1, h2, h3 = _split3(lf)
        cum = _dot(cum_mat, h1) + _dot(cum_mat, h2) + _dot(cum_mat, h3)
        loc = _dot(loc_mat, h1) + _dot(loc_mat, h2) + _dot(loc_mat, h3)
        return cum, loc

    def head_cols(head):
        return slice(head * HG_DK, (head + 1) * HG_DK)

    def emit(r0, head, o):
        rows, cols = pl.ds(r0, C), head_cols(head)
        if finalize:
            tot = o + of_ref[rows, cols]
            ms = jnp.mean(tot * tot, axis=-1, keepdims=True)
            res = tot * lax.rsqrt(ms + RMS_EPS) * gain_ref[...] * g_ref[rows, cols].astype(F32)
            o_ref[rows, cols] = res.astype(o_ref.dtype)
        else:
            o_ref[rows, cols] = o

    def pairwise_chunk(step, carry, *, head):
        cidx = (n_chunk - 1 - step) if reverse else step
        r0 = pl.multiple_of(cidx * C, C)
        rows, cols = pl.ds(r0, C), head_cols(head)
        lf = lf_ref[rows, cols]
        q = q_ref[rows, cols].astype(F32)
        v = v_ref[rows, cols]
        cum, loc = cumulative(lf)
        kk = 1.0 - jnp.exp(lf)
        total = cum[edge:edge + 1, :]
        st = st_ref[head]
        o_inter = _dot_nt((q * jnp.exp(cum)).astype(BF16), st.astype(BF16))
        k_state = (kk * jnp.exp(total - cum)).astype(BF16)
        st_ref[head] = st * jnp.exp(total) + _dot_tn(v, k_state)
        qh_b = (q * jnp.exp(loc)).astype(BF16)
        blocks = []
        for s in range(n_sub):
            lo = s * HG_SUB
            hi = lo + HG_SUB
            has_off = (s < n_sub - 1) if reverse else (s > 0)
            if has_off:
                if reverse:
                    expo = jnp.where(row_id >= hi, cum[hi:hi + 1, :] - cum, NEG_INF)
                else:
                    expo = jnp.where(row_id < lo, cum[lo - 1:lo, :] - cum, NEG_INF)
                k_off = (kk * jnp.exp(expo)).astype(BF16)
                a_s = _dot_nt(qh_b[lo:hi], k_off)
            else:
                a_s = jnp.zeros((HG_SUB, C), F32)
            loc_s = loc[lo:hi]
            q_s = q[lo:hi]
            for j in range(HG_SUB):
                jj = lo + j
                d = loc_s - loc[jj:jj + 1, :]
                msk = (sub_row <= j) if reverse else (sub_row >= j)
                e = jnp.exp(jnp.where(msk, d, NEG_INF))
                colv = jnp.sum(q_s * e * kk[jj:jj + 1, :], axis=-1, keepdims=True)
                a_s = jnp.where(lane_c == jj, colv, a_s)
            blocks.append(a_s)
        a = jnp.concatenate(blocks, axis=0)
        emit(r0, head, o_inter + _dot(a.astype(BF16), v))
        return carry

    def staged_tile():
        items = [(((n_chunk - 1 - s) if reverse else s) * C, head) for s in range(n_chunk) for head in range(n_heads)]
        lfs = [lf_ref[r0:r0 + C, head_cols(h)] for r0, h in items]
        qs = [q_ref[r0:r0 + C, head_cols(h)].astype(F32) for r0, h in items]
        vs = [v_ref[r0:r0 + C, head_cols(h)] for r0, h in items]
        cums, locs = zip(*[cumulative(lf) for lf in lfs])
        kks = [1.0 - jnp.exp(lf) for lf in lfs]
        totals = [cum[edge:edge + 1, :] for cum in cums]
        a_mats = []
        for q, kk, cum, loc in zip(qs, kks, cums, locs):
            qh_b = (q * jnp.exp(loc)).astype(BF16)
            blocks = []
            for s in range(n_sub):
                lo = s * HG_SUB
                hi = lo + HG_SUB
                if reverse:
                    edge_row = cum[hi:hi + 1, :] if s < n_sub - 1 else jnp.zeros((1, HG_DK), F32)
                    expo = jnp.where(row_id >= lo, edge_row - cum, NEG_INF)
                    tri = lane_c >= lo + sub_row
                else:
                    edge_row = cum[lo - 1:lo, :] if s > 0 else jnp.zeros((1, HG_DK), F32)
                    expo = jnp.where(row_id < hi, edge_row - cum, NEG_INF)
                    tri = lane_c <= lo + sub_row
                k_rel = (kk * jnp.exp(expo)).astype(BF16)
                blocks.append(jnp.where(tri, _dot_nt(qh_b[lo:hi], k_rel), 0.0))
            a_mats.append(jnp.concatenate(blocks, axis=0).astype(BF16))
        intras = [_dot(a, v) for a, v in zip(a_mats, vs)]
        updates = [_dot_tn(v, (kk * jnp.exp(tot - cum)).astype(BF16))
                   for v, kk, tot, cum in zip(vs, kks, totals, cums)]
        q_ins = [(q * jnp.exp(cum)).astype(BF16) for q, cum in zip(qs, cums)]
        states = [st_ref[h] for h in range(n_heads)]
        for (r0, h), q_in, intra, tot, upd in zip(items, q_ins, intras, totals, updates):
            emit(r0, h, _dot_nt(q_in, states[h].astype(BF16)) + intra)
            states[h] = states[h] * jnp.exp(tot) + upd
        for h in range(n_heads):
            st_ref[h] = states[h]

    mild_decay = jnp.min(lf_ref[...]) >= -HG_SAFE_LOG

    @pl.when(mild_decay)
    def _():
        staged_tile()

    @pl.when(jnp.logical_not(mild_decay))
    def _():
        for head in range(n_heads):
            lax.fori_loop(0, n_chunk, functools.partial(pairwise_chunk, head=head), 0)


def _hgrn_pass(q, v, lf, groups, *, reverse, finalize_args=None):
    T = q.shape[0]
    tb = TOKEN_TILE
    finalize = finalize_args is not None

    def tmap(h, i):
        return (groups.reversed_tile(i, tb) if reverse else i, h)

    spec = pl.BlockSpec((tb, HG_DK * HG_HEADS_PER_STEP), tmap)
    in_specs = [spec, spec, spec]
    args = [q, v, lf]
    if finalize:
        o_fwd, gate, gain = finalize_args
        in_specs += [spec, spec, pl.BlockSpec((1, HG_DK), lambda h, i: (0, 0))]
        args += [o_fwd, gate, gain]
    return pl.pallas_call(
        functools.partial(_hgrn_kernel, reverse=reverse, finalize=finalize, groups=groups),
        out_shape=jax.ShapeDtypeStruct((T, HG_WIDTH), BF16 if finalize else F32),
        grid=(HG_HEADS // HG_HEADS_PER_STEP, T // tb),
        in_specs=in_specs,
        out_specs=spec,
        scratch_shapes=[pltpu.VMEM((HG_HEADS_PER_STEP, HG_DK, HG_DK), F32)],
        compiler_params=_cparams(("parallel", "arbitrary")),
        name="hgrn_bwd" if reverse else "hgrn_fwd",
    )(*args)


def _kvproj_kernel(m_ref, w_ref, o_ref):
    o_ref[...] = _dot(m_ref[...].astype(BF16), w_ref[...]).astype(BF16)


def _kvproj(mem, w):
    R, D = mem.shape
    n = w.shape[1]
    tr = 256
    return pl.pallas_call(
        _kvproj_kernel,
        out_shape=jax.ShapeDtypeStruct((R, n), BF16),
        grid=(R // tr,),
        in_specs=[pl.BlockSpec((tr, D), lambda i: (i, 0)), pl.BlockSpec((D, n), lambda i: (0, 0))],
        out_specs=pl.BlockSpec((tr, n), lambda i: (i, 0)),
        compiler_params=_cparams(("parallel",)),
        name="kvproj",
    )(mem, w)


def _mix_xattn_kernel(x0_ref, x1_ref, att_ref, rec_ref, g0_ref, b0_ref, wo_ref, g1_ref, b1_ref, wq_ref, kv_ref,
                      wxo_ref, g2_ref, b2_ref, x2_ref, p_ref, *, tiles_group0):
    D = x0_ref.shape[1]
    dh = D // X_HEADS
    x_tile = _group_rows(x0_ref, x1_ref, tiles_group0)

    def stages(rows):
        mix = _dot(att_ref[rows, :], wo_ref[:ATT_WIDTH, :]) + _dot(rec_ref[rows, :], wo_ref[ATT_WIDTH:, :])
        yield
        xn = _layer_norm(x_tile[rows], g0_ref[...], b0_ref[...])
        x1 = _layer_norm(DEEPNORM_ALPHA * xn + mix, g1_ref[...], b1_ref[...])
        qx = (_dot(x1.astype(BF16), wq_ref[...]) * (dh ** -0.5)).astype(BF16)
        yield
        scores = [_dot_nt(qx[:, h * dh:(h + 1) * dh], kv_ref[0, :, h * dh:(h + 1) * dh]) for h in range(X_HEADS)]
        yield
        heads = []
        for h, s in enumerate(scores):
            s = s - jnp.max(s, axis=-1, keepdims=True)
            p = jnp.exp(s)
            p = p / jnp.sum(p, axis=-1, keepdims=True)
            heads.append(_dot(p.astype(BF16), kv_ref[0, :, D + h * dh:D + (h + 1) * dh]).astype(BF16))
        yield
        xo = _dot(jnp.concatenate(heads, axis=-1), wxo_ref[...])
        yield
        x2 = _layer_norm(DEEPNORM_ALPHA * x1 + xo, g2_ref[...], b2_ref[...])
        x2_ref[rows, :] = x2
        _store_slabs(p_ref, _pack_bf16_pair(x2[:, :D // 2], x2[:, D // 2:]), row0=rows.start)

    _staggered(stages, x_tile.shape[0])


def _mix_xattn(x0, x1, att, rec, g0, b0, wo, g1, b1, wq, kv, wxo, g2, b2, seq_of_tile):
    D = x0.shape[1]
    T = x0.shape[0] + x1.shape[0]
    tm = TOKEN_TILE
    tiles_group0 = x0.shape[0] // tm
    n_mem = kv.shape[1]
    row = lambda i: (i, 0)
    const = lambda i: (0, 0)
    vec = pl.BlockSpec((1, D), const)
    return pl.pallas_call(
        functools.partial(_mix_xattn_kernel, tiles_group0=tiles_group0),
        out_shape=(jax.ShapeDtypeStruct((T, D), F32), jax.ShapeDtypeStruct((T * SLAB, LANES), U32)),
        grid=(T // tm,),
        in_specs=_group_specs(tm, D, tiles_group0) + [
            pl.BlockSpec((tm, ATT_WIDTH), row),
            pl.BlockSpec((tm, HG_WIDTH), row),
            vec, vec,
            pl.BlockSpec((ATT_WIDTH + HG_WIDTH, D), const),
            vec, vec,
            pl.BlockSpec((D, D), const),
            pl.BlockSpec((1, n_mem, 2 * D), lambda i: (seq_of_tile(i), 0, 0)),
            pl.BlockSpec((D, D), const),
            vec, vec,
        ],
        out_specs=(pl.BlockSpec((tm, D), row), pl.BlockSpec((tm * SLAB, LANES), row)),
        compiler_params=_cparams(("parallel",)),
        name="mix_xattn",
    )(x0, x1, att, rec, g0, b0, wo, g1, b1, wq, kv, wxo, g2, b2)


def _router_kernel(x_ref, wt_ref, b_ref, idx_ref, gate_ref, rank_ref, cnt_ref, run_ref):
    tm = x_ref.shape[0]

    @pl.when(pl.program_id(0) == 0)
    def _():
        run_ref[...] = jnp.zeros_like(run_ref)

    xh, xl, _ = _split3(x_ref[...])
    wh, wl, _ = _split3(wt_ref[...])
    logits = _dot_nt(wh, xh) + _dot_nt(wh, xl) + _dot_nt(wl, xh) + b_ref[...]
    e_id = lax.broadcasted_iota(I32, (N_EXPERTS, tm), 0)
    work = logits
    vals, idxs = [], []
    chosen = jnp.zeros((N_EXPERTS, tm), F32)
    for _ in range(TOP_K):
        m = jnp.max(work, axis=0, keepdims=True)
        idx = jnp.min(jnp.where(work == m, e_id, N_EXPERTS), axis=0, keepdims=True)
        hit = e_id == idx
        chosen = jnp.where(hit, 1.0, chosen)
        work = jnp.where(hit, NEG_INF, work)
        vals.append(m)
        idxs.append(idx)
    ex = [jnp.exp(v - vals[0]) for v in vals]
    den = ex[0] + ex[1] + ex[2] + ex[3]
    t_r = lax.broadcasted_iota(I32, (tm, tm), 0)
    t_c = lax.broadcasted_iota(I32, (tm, tm), 1)
    before = jnp.where(t_r < t_c, 1.0, 0.0).astype(BF16)
    rank_e = _dot(chosen.astype(BF16), before) + run_ref[...]
    run_ref[...] = run_ref[...] + jnp.sum(chosen, axis=1, keepdims=True)
    for k in range(TOP_K):
        idx_ref[k:k + 1, :] = idxs[k]
        gate_ref[k:k + 1, :] = ex[k] / den
        rank_ref[k:k + 1, :] = jnp.sum(jnp.where(e_id == idxs[k], rank_e, 0.0), axis=0, keepdims=True).astype(I32)
    cnt_ref[...] = run_ref[...].astype(I32)


def _router(x2, w_router_t, b_router):
    T, D = x2.shape
    tm = TOKEN_TILE
    tok = pl.BlockSpec((TOP_K, tm), lambda i: (0, i))
    return pl.pallas_call(
        _router_kernel,
        out_shape=(jax.ShapeDtypeStruct((TOP_K, T), I32), jax.ShapeDtypeStruct((TOP_K, T), F32),
                   jax.ShapeDtypeStruct((TOP_K, T), I32), jax.ShapeDtypeStruct((N_EXPERTS, 1), I32)),
        grid=(T // tm,),
        in_specs=[pl.BlockSpec((tm, D), lambda i: (i, 0)),
                  pl.BlockSpec((N_EXPERTS, D), lambda i: (0, 0)),
                  pl.BlockSpec((N_EXPERTS, 1), lambda i: (0, 0))],
        out_specs=(tok, tok, tok, pl.BlockSpec((N_EXPERTS, 1), lambda i: (0, 0))),
        scratch_shapes=[pltpu.VMEM((N_EXPERTS, 1), F32)],
        compiler_params=_cparams(("arbitrary",)),
        name="router",
    )(x2, w_router_t, b_router)


def _dispatch_kernel(dest_ref, p_ref, zero_hbm, out_hbm, sem):
    del zero_hbm
    td = DISPATCH_TILE

    def start(t, c):
        for k in range(TOP_K):
            d = dest_ref[0, 0, k * td + t]
            pltpu.make_async_copy(p_ref.at[t], out_hbm.at[d], sem).start(priority=k % DMA_PRIORITIES)
        return c

    lax.fori_loop(0, td, start, 0, unroll=DMA_ISSUE_UNROLL)
    everything = out_hbm.at[pl.ds(0, TOP_K * td)]
    pltpu.make_async_copy(everything, everything, sem).wait()


def _dispatch(dest_tiles, p, zeros):
    T = p.shape[0]
    td = DISPATCH_TILE
    return pl.pallas_call(
        _dispatch_kernel,
        out_shape=jax.ShapeDtypeStruct(zeros.shape, zeros.dtype),
        grid=(T // td,),
        in_specs=[pl.BlockSpec((1, 1, TOP_K * td), lambda i: (i, 0, 0), memory_space=pltpu.SMEM),
                  pl.BlockSpec((td,) + p.shape[1:], lambda i: (i, 0, 0)),
                  pl.BlockSpec(memory_space=pl.ANY)],
        out_specs=pl.BlockSpec(memory_space=pl.ANY),
        scratch_shapes=[pltpu.SemaphoreType.DMA],
        input_output_aliases={2: 0},
        compiler_params=_cparams(("arbitrary",)),
        name="dispatch",
    )(dest_tiles, p, zeros)


def _expert_kernel(blk_e_ref, nvalid_ref, p_ref, wu_ref, bu_ref, wd_ref, bd_ref, y_ref, wu_b, wd_b):
    i = pl.program_id(0)
    dff = wd_ref.shape[1]
    half = SLAB * LANES
    live = i < nvalid_ref[0]
    new_expert = (i == 0) | (blk_e_ref[i] != blk_e_ref[jnp.maximum(i - 1, 0)])

    @pl.when(live & new_expert)
    def _():
        wu_b[...] = wu_ref[0].astype(BF16)
        wd_b[...] = wd_ref[0].astype(BF16)

    @pl.when(live)
    def _():
        lo, hi = _unpack_bf16_pair(_load_slabs(p_ref))
        lo = lo.astype(BF16)
        hi = hi.astype(BF16)

        def up(c0):
            return (_dot(lo, wu_b[:half, c0:c0 + dff]) + _dot(hi, wu_b[half:, c0:c0 + dff])
                    + bu_ref[0, :, c0:c0 + dff])

        glu = jnp.minimum(up(0), SWIGLU_LIMIT)
        lin = jnp.clip(up(dff), -SWIGLU_LIMIT, SWIGLU_LIMIT)
        a = glu * _sigmoid(SWIGLU_ALPHA * glu) * (lin + 1.0)
        y = _dot(a.astype(BF16), wd_b[...]) + bd_ref[0]
        _store_slabs(y_ref, _pack_bf16_pair(y[:, :half], y[:, half:]))

    @pl.when(i >= nvalid_ref[0])
    def _():
        y_ref[...] = jnp.zeros_like(y_ref)


def _experts(blk_expert, n_valid, ps, wu, bu, wd, bd):
    R = ps.shape[0] // SLAB
    te = EXPERT_TILE
    E, D, two_f = wu.shape
    dff = wd.shape[1]
    grid_spec = pltpu.PrefetchScalarGridSpec(
        num_scalar_prefetch=2,
        grid=(R // te,),
        in_specs=[
            pl.BlockSpec((te * SLAB, LANES), lambda i, be, nv: (i, 0)),
            pl.BlockSpec((1, D, two_f), lambda i, be, nv: (be[i], 0, 0)),
            pl.BlockSpec((1, 1, two_f), lambda i, be, nv: (be[i], 0, 0)),
            pl.BlockSpec((1, dff, D), lambda i, be, nv: (be[i], 0, 0)),
            pl.BlockSpec((1, 1, D), lambda i, be, nv: (be[i], 0, 0)),
        ],
        out_specs=pl.BlockSpec((te * SLAB, LANES), lambda i, be, nv: (i, 0)),
        scratch_shapes=[pltpu.VMEM((D, two_f), BF16), pltpu.VMEM((dff, D), BF16)],
    )
    return pl.pallas_call(
        _expert_kernel,
        out_shape=jax.ShapeDtypeStruct((R * SLAB, LANES), U32),
        grid_spec=grid_spec,
        compiler_params=_cparams(("arbitrary",)),
        name="experts",
    )(blk_expert, n_valid, ps, wu, bu, wd, bd)


def _combine_kernel(dest_ref, dest_next_ref, ys_hbm, x2_ref, gate_ref, g_ref, b_ref, o_ref, buf, sems):
    tc = x2_ref.shape[0]
    half = SLAB * LANES
    i = pl.program_id(0)
    slot = i % 2

    def start_tile(idx_ref, s):
        def body(t, c):
            rows = pl.ds(pl.multiple_of(t * SLAB, SLAB), SLAB)
            for k in range(TOP_K):
                d = idx_ref[0, 0, k * tc + t]
                pltpu.make_async_copy(ys_hbm.at[d], buf.at[s, k, rows], sems.at[s]).start(
                    priority=k % DMA_PRIORITIES)
            return c
        lax.fori_loop(0, tc, body, 0, unroll=DMA_ISSUE_UNROLL)

    @pl.when(i == 0)
    def _():
        start_tile(dest_ref, 0)

    for s in range(2):
        @pl.when((i + 1 < pl.num_programs(0)) & (slot == s))
        def _():
            start_tile(dest_next_ref, 1 - s)

    for s in range(2):
        @pl.when(slot == s)
        def _():
            pltpu.make_async_copy(buf.at[s], buf.at[s], sems.at[s]).wait()

    gates = gate_ref[...]
    acc_lo = jnp.zeros((tc, half), F32)
    acc_hi = jnp.zeros((tc, half), F32)
    for k in range(TOP_K):
        lo, hi = _unpack_bf16_pair(_load_slabs(buf.at[slot, k]))
        gk = gates[:, k:k + 1]
        acc_lo = acc_lo + gk * lo
        acc_hi = acc_hi + gk * hi
    ffn = jnp.concatenate([acc_lo, acc_hi], axis=-1)
    o_ref[...] = _layer_norm(DEEPNORM_ALPHA * x2_ref[...] + ffn, g_ref[...], b_ref[...])


def _combine(dest_tiles, ys, x2, gates, g, b, *, tile0, n_tiles):
    D = x2.shape[1]
    tc = COMBINE_TILE
    return pl.pallas_call(
        _combine_kernel,
        out_shape=jax.ShapeDtypeStruct((n_tiles * tc, D), F32),
        grid=(n_tiles,),
        in_specs=[pl.BlockSpec((1, 1, TOP_K * tc), lambda i: (tile0 + i, 0, 0), memory_space=pltpu.SMEM),
                  pl.BlockSpec((1, 1, TOP_K * tc), lambda i: (tile0 + jnp.minimum(i + 1, n_tiles - 1), 0, 0),
                               memory_space=pltpu.SMEM),
                  pl.BlockSpec(memory_space=pl.ANY),
                  pl.BlockSpec((tc, D), lambda i: (tile0 + i, 0)),
                  pl.BlockSpec((tc, TOP_K), lambda i: (tile0 + i, 0)),
                  pl.BlockSpec((1, D), lambda i: (0, 0)),
                  pl.BlockSpec((1, D), lambda i: (0, 0))],
        out_specs=pl.BlockSpec((tc, D), lambda i: (i, 0)),
        scratch_shapes=[pltpu.VMEM((2, TOP_K, tc * SLAB, LANES), U32), pltpu.SemaphoreType.DMA((2,))],
        compiler_params=_cparams(("arbitrary",)),
        name="combine",
    )(dest_tiles, dest_tiles, ys, x2, gates, g, b)


def _rope_tables(seq_len):
    rows = seq_len // GRID_W
    row = jnp.repeat(jnp.arange(rows, dtype=F32), GRID_W)
    colp = (jnp.arange(seq_len) % GRID_W).astype(F32)
    axis_dim = ATT_HEAD_DIM // 2
    inv_freq = ROPE_THETA ** (-jnp.arange(0, axis_dim, 2, dtype=F32) / axis_dim)
    ang = jnp.concatenate([row[:, None] * inv_freq, colp[:, None] * inv_freq], axis=-1)
    cos, sin = jnp.cos(ang), jnp.sin(ang)
    cos_h = jnp.concatenate([cos, cos], axis=-1)
    sin_h = jnp.concatenate([-sin, sin], axis=-1)
    return jnp.tile(cos_h, (1, LANES // ATT_HEAD_DIM)), jnp.tile(sin_h, (1, LANES // ATT_HEAD_DIM))


def _tiles_of(idx_t, tile):
    K, T = idx_t.shape
    return idx_t.reshape(K, T // tile, tile).transpose(1, 0, 2).reshape(T // tile, 1, K * tile)


def kernel(x_prompt, x_sample, mem_prompt, mem_sample, ln_in_g, ln_in_b, w_in, q_norm_g, k_norm_g, hg_lb_fwd, hg_lb_bwd, hg_norm_g, w_out, ln1_g, ln1_b, w_xq, w_xkv, w_xo, ln2_g, ln2_b, w_router, b_router, w_up, b_up, w_down, b_down, ln3_g, ln3_b):
    Bp, Sp, D = x_prompt.shape
    Bs, Ss, _ = x_sample.shape
    Tp, Ts = Bp * Sp, Bs * Ss
    T = Tp + Ts
    tm = TOKEN_TILE
    assert Sp % tm == 0 and Ss % Sp == 0
    n_mem = mem_prompt.shape[1]
    layer = 0

    x0, x1 = x_prompt.reshape(Tp, D), x_sample.reshape(Ts, D)
    mem = jnp.concatenate([mem_prompt.reshape(Bp * n_mem, D), mem_sample.reshape(Bs * n_mem, D)], axis=0)

    def deinterleave(a):
        lead = a.shape[:-1]
        a = a.reshape(*lead, -1, ATT_HEAD_DIM // 2, 2)
        return jnp.swapaxes(a, -1, -2).reshape(*lead, -1)

    n_qk = ATT_WIDTH + ATT_KV_WIDTH
    w_in_l = w_in[layer]
    w_in_b = jnp.concatenate([deinterleave(w_in_l[:, :n_qk]), w_in_l[:, n_qk:]], axis=1).astype(BF16)
    pair = LANES // ATT_HEAD_DIM
    qg = jnp.tile(deinterleave(q_norm_g[layer]), pair).reshape(1, LANES)
    kg = jnp.tile(deinterleave(k_norm_g[layer]), pair).reshape(1, LANES)

    def lower_bound(logits):
        sm = jax.nn.softmax(logits.astype(F32), axis=0)
        return (jnp.cumsum(sm, axis=0)[layer + 1] - sm[0]).reshape(1, HG_WIDTH)

    lbf, lbb = lower_bound(hg_lb_fwd), lower_bound(hg_lb_bwd)
    cos_t, sin_t = _rope_tables(max(Sp, Ss))
    vec = lambda a: a.reshape(1, -1)

    groups = _Groups(Bp, Sp, Bs, Ss)
    pos_block = lambda i: groups.pos_of_tile(i, tm)
    seq_of_tile = lambda i: groups.seq_of_tile(i, tm)

    q, k, v, hq, hi, hg, lff, lfb = _inproj(x0, x1, vec(ln_in_g), vec(ln_in_b), w_in_b, cos_t, sin_t, qg, kg,
                                           lbf, lbb, pos_block)

    score_bound = (jnp.max(jnp.abs(q_norm_g[layer])) * jnp.max(jnp.abs(k_norm_g[layer]))
                   * (ATT_HEAD_DIM ** 0.5 * ATT_NORM_SLACK)).astype(F32).reshape(1)
    att = _attention(score_bound, q, k, v, groups)
    o_fwd = _hgrn_pass(hq, hi, lff, groups, reverse=False)
    gain = hg_norm_g[layer].reshape(1, HG_DK)
    rec = _hgrn_pass(hq, hi, lfb, groups, reverse=True, finalize_args=(o_fwd, hg, gain))

    kv = _kvproj(mem, w_xkv[layer].astype(BF16)).reshape(Bp + Bs, n_mem, 2 * D)
    x2, packed = _mix_xattn(x0, x1, att, rec, vec(ln_in_g), vec(ln_in_b), w_out[layer].astype(BF16),
                            vec(ln1_g[layer]), vec(ln1_b[layer]), w_xq[layer].astype(BF16), kv,
                            w_xo[layer].astype(BF16), vec(ln2_g[layer]), vec(ln2_b[layer]), seq_of_tile)

    idx_t, gate_t, rank_t, counts = _router(x2, w_router[layer].T, b_router[layer].reshape(N_EXPERTS, 1))

    te = EXPERT_TILE
    counts = counts.reshape(N_EXPERTS)
    padded = (counts + te - 1) // te * te
    pad_end = jnp.cumsum(padded)
    pad_start = pad_end - padded
    n_blk = (T * TOP_K + N_EXPERTS * (te - 1) + te - 1) // te
    e_ids = jnp.arange(N_EXPERTS, dtype=I32)
    start_of = jnp.sum(jnp.where(idx_t[None] == e_ids[:, None, None], pad_start[:, None, None], 0), axis=0)
    dest_t = start_of + rank_t
    blk_row0 = jnp.arange(n_blk, dtype=I32) * te
    blk_expert = jnp.minimum(jnp.sum((pad_end[None, :] <= blk_row0[:, None]).astype(I32), axis=1), N_EXPERTS - 1)
    n_valid = (pad_end[-1] // te).astype(I32).reshape(1)

    n_rows = n_blk * te
    sorted_rows = _dispatch(_tiles_of(dest_t, DISPATCH_TILE), packed.reshape(T, SLAB, LANES),
                            jnp.zeros((n_rows, SLAB, LANES), U32))
    ys = _experts(blk_expert, n_valid, sorted_rows.reshape(n_rows * SLAB, LANES), w_up[layer],
                  b_up[layer].reshape(N_EXPERTS, 1, -1), w_down[layer], b_down[layer].reshape(N_EXPERTS, 1, -1))
    ys = ys.reshape(n_rows, SLAB, LANES)

    dest_c = _tiles_of(dest_t, COMBINE_TILE)
    gates = gate_t.T
    tc = COMBINE_TILE
    y_p = _combine(dest_c, ys, x2, gates, vec(ln3_g[layer]), vec(ln3_b[layer]), tile0=0, n_tiles=Tp // tc)
    y_s = _combine(dest_c, ys, x2, gates, vec(ln3_g[layer]), vec(ln3_b[layer]), tile0=Tp // tc, n_tiles=Ts // tc)
    return y_p.reshape(Bp, Sp, D), y_s.reshape(Bs, Ss, D)
```

```python
import functools

import jax
import jax.numpy as jnp
import numpy as np
from jax import lax
from jax.experimental import pallas as pl
from jax.experimental.pallas import tpu as pltpu

F32 = jnp.float32
BF16 = jnp.bfloat16
I32 = jnp.int32
U32 = jnp.uint32

GRID_W = 64
ATT_HEADS = 8
ATT_KV_HEADS = 2
ATT_HEAD_DIM = 64
ATT_WIDTH = ATT_HEADS * ATT_HEAD_DIM
ATT_KV_WIDTH = ATT_KV_HEADS * ATT_HEAD_DIM
ROPE_THETA = 10000.0
HG_HEADS = 4
HG_DK = 128
HG_WIDTH = HG_HEADS * HG_DK
X_HEADS = 4
N_EXPERTS = 32
TOP_K = 4
SWIGLU_LIMIT = 7.0
SWIGLU_ALPHA = 1.702
LN_EPS = 1e-5
RMS_EPS = 1e-6
DEPTH = 1
DEEPNORM_ALPHA = (2.0 * DEPTH) ** 0.25

LANES = 128
VMEM_LIMIT_BYTES = 56 * 1024 * 1024
DMA_PRIORITIES = 2
DMA_ISSUE_UNROLL = 4

TOKEN_TILE = 512
ATT_Q_TILE = 256
ATT_K_TILE = 4096
HG_CHUNK = 64
HG_SUB = 16
HG_HEADS_PER_STEP = 4

ATT_SAFE_SCORE = 60.0
ATT_NORM_SLACK = 1.02
HG_SAFE_LOG = 5.0
EXPERT_TILE = 512
COMBINE_TILE = 256
DISPATCH_TILE = 512
FILL_CHUNK_LOG2 = 5
FILL_CHUNK = 1 << FILL_CHUNK_LOG2
ROW_PARTS = 2
NEG_INF = float("-inf")


def _cparams(sem):
    return pltpu.CompilerParams(dimension_semantics=sem, vmem_limit_bytes=VMEM_LIMIT_BYTES)


def _layer_norm(x, g, b):
    mu = jnp.mean(x, axis=-1, keepdims=True)
    xc = x - mu
    var = jnp.mean(xc * xc, axis=-1, keepdims=True)
    return xc * lax.rsqrt(var + LN_EPS) * g + b


def _sigmoid(x):
    return 1.0 / (1.0 + jnp.exp(-x))


def _dot(a, b):
    return jnp.dot(a, b, preferred_element_type=F32)


def _dot_nt(a, b):
    return lax.dot_general(a, b, (((1,), (1,)), ((), ())), preferred_element_type=F32)


def _dot_tn(a, b):
    return lax.dot_general(a, b, (((0,), (0,)), ((), ())), preferred_element_type=F32)


def _split3(x):
    h1 = x.astype(BF16)
    r1 = x - h1.astype(F32)
    h2 = r1.astype(BF16)
    h3 = (r1 - h2.astype(F32)).astype(BF16)
    return h1, h2, h3


def _pack_bf16_pair(lo, hi):
    lo_b = pltpu.bitcast(lo.astype(BF16).astype(F32), U32)
    hi_b = pltpu.bitcast(hi.astype(BF16).astype(F32), U32)
    return (hi_b & jnp.uint32(0xFFFF0000)) | (lo_b >> 16)


def _unpack_bf16_pair(p):
    lo = pltpu.bitcast(p << 16, F32)
    hi = pltpu.bitcast(p & jnp.uint32(0xFFFF0000), F32)
    return lo, hi


SLAB = 4


def _store_slabs(ref, value, row0=0):
    n = value.shape[0]
    for s in range(SLAB):
        ref[pl.ds(row0 * SLAB + s, n, stride=SLAB), :] = value[:, s * LANES:(s + 1) * LANES]


def _load_slabs(ref, row0=0, n=None):
    n = ref.shape[0] // SLAB if n is None else n
    return jnp.concatenate([ref[pl.ds(row0 * SLAB + s, n, stride=SLAB), :] for s in range(SLAB)], axis=-1)


def _row_parts(n_rows):
    part = n_rows // ROW_PARTS
    return [slice(r * part, (r + 1) * part) for r in range(ROW_PARTS)]


def _staggered(stage_fn, n_rows):
    pending = [stage_fn(rows) for rows in _row_parts(n_rows)]
    while pending:
        for g in list(pending):
            try:
                next(g)
            except StopIteration:
                pending.remove(g)


def _group_rows(x0_ref, x1_ref, tiles_group0):
    return jnp.where(pl.program_id(0) < tiles_group0, x0_ref[...], x1_ref[...])


def _group_specs(tile, width, tiles_group0):
    return [pl.BlockSpec((tile, width), lambda i: (jnp.minimum(i, tiles_group0 - 1), 0)),
            pl.BlockSpec((tile, width), lambda i: (jnp.maximum(i - tiles_group0, 0), 0))]


def _inproj_kernel(x0_ref, x1_ref, g_ref, b_ref, w_ref, cos_ref, sin_ref, qg_ref, kg_ref, lbf_ref, lbb_ref,
                   q_out, k_out, v_out, hq_out, hi_out, hg_out, lff_out, lfb_out, *, tiles_group0):
    xn = _layer_norm(_group_rows(x0_ref, x1_ref, tiles_group0), g_ref[...], b_ref[...])
    xb = xn.astype(BF16)
    cos = cos_ref[...]
    sin = sin_ref[...]
    tm = xb.shape[0]

    lane = lax.broadcasted_iota(I32, (tm, LANES), 1)
    first_half = (lane & (ATT_HEAD_DIM - 1)) < (ATT_HEAD_DIM // 2)
    r = lax.broadcasted_iota(I32, (LANES, LANES), 0)
    c = lax.broadcasted_iota(I32, (LANES, LANES), 1)
    head_ones = jnp.where((r >> 6) == (c >> 6), 1.0, 0.0).astype(BF16)

    def normed_rope(u, gain, scale):
        sq = u * u
        s_hi = sq.astype(BF16)
        s_lo = (sq - s_hi.astype(F32)).astype(BF16)
        ms = (_dot(s_hi, head_ones) + _dot(s_lo, head_ones)) * (1.0 / ATT_HEAD_DIM)
        un = u * lax.rsqrt(ms + RMS_EPS) * gain
        rot = jnp.where(first_half, pltpu.roll(un, LANES - 32, 1), pltpu.roll(un, 32, 1))
        return (un * cos + rot * sin) * scale

    n_qkv = ATT_WIDTH + 2 * ATT_KV_WIDTH
    u_att = [_dot(xb, w_ref[:, c:c + 2 * LANES]) for c in range(0, n_qkv, 2 * LANES)]
    col = n_qkv

    def hg_proj():
        nonlocal col
        u = _dot(xb, w_ref[:, col:col + HG_WIDTH])
        col += HG_WIDTH
        return u

    u = hg_proj()
    hq_out[...] = (u * _sigmoid(u)).astype(BF16)
    for j in range(ATT_WIDTH // (2 * LANES)):
        for h in range(2):
            c0 = (2 * j + h) * LANES
            q_out[:, c0:c0 + LANES] = normed_rope(u_att[j][:, h * LANES:(h + 1) * LANES], qg_ref[...],
                                                  ATT_HEAD_DIM ** -0.5).astype(BF16)
        u = hg_proj()
        lb = (lbf_ref, lbb_ref)[j][...]
        (lff_out, lfb_out)[j][...] = jnp.log(lb + (1.0 - lb) * _sigmoid(u))
    k_out[...] = normed_rope(u_att[-1][:, :LANES], kg_ref[...], 1.0).astype(BF16)
    v_out[:, :LANES] = u_att[-1][:, LANES:].astype(BF16)
    v_out[:, LANES:] = jnp.ones((tm, LANES), BF16)
    hi_out[...] = hg_proj().astype(BF16)
    u = hg_proj()
    hg_out[...] = (u * _sigmoid(u)).astype(BF16)


def _inproj(x0, x1, g, b, w, cos_t, sin_t, qg, kg, lbf, lbb, pos_block):
    D = x0.shape[1]
    T = x0.shape[0] + x1.shape[0]
    tm = TOKEN_TILE
    tiles_group0 = x0.shape[0] // tm
    n_in = w.shape[1]
    row = lambda i: (i, 0)
    const = lambda i: (0, 0)
    out_shape = (
        jax.ShapeDtypeStruct((T, ATT_WIDTH), BF16),
        jax.ShapeDtypeStruct((T, ATT_KV_WIDTH), BF16),
        jax.ShapeDtypeStruct((T, 2 * ATT_KV_WIDTH), BF16),
        jax.ShapeDtypeStruct((T, HG_WIDTH), BF16),
        jax.ShapeDtypeStruct((T, HG_WIDTH), BF16),
        jax.ShapeDtypeStruct((T, HG_WIDTH), BF16),
        jax.ShapeDtypeStruct((T, HG_WIDTH), F32),
        jax.ShapeDtypeStruct((T, HG_WIDTH), F32),
    )
    out_specs = (
        pl.BlockSpec((tm, ATT_WIDTH), row),
        pl.BlockSpec((tm, ATT_KV_WIDTH), row),
        pl.BlockSpec((tm, 2 * ATT_KV_WIDTH), row),
        pl.BlockSpec((tm, HG_WIDTH), row),
        pl.BlockSpec((tm, HG_WIDTH), row),
        pl.BlockSpec((tm, HG_WIDTH), row),
        pl.BlockSpec((tm, HG_WIDTH), row),
        pl.BlockSpec((tm, HG_WIDTH), row),
    )
    return pl.pallas_call(
        functools.partial(_inproj_kernel, tiles_group0=tiles_group0),
        out_shape=out_shape,
        grid=(T // tm,),
        in_specs=_group_specs(tm, D, tiles_group0) + [
            pl.BlockSpec((1, D), const),
            pl.BlockSpec((1, D), const),
            pl.BlockSpec((D, n_in), const),
            pl.BlockSpec((tm, LANES), lambda i: (pos_block(i), 0)),
            pl.BlockSpec((tm, LANES), lambda i: (pos_block(i), 0)),
            pl.BlockSpec((1, LANES), const),
            pl.BlockSpec((1, LANES), const),
            pl.BlockSpec((1, HG_WIDTH), const),
            pl.BlockSpec((1, HG_WIDTH), const),
        ],
        out_specs=out_specs,
        compiler_params=_cparams(("parallel",)),
        name="inproj",
    )(x0, x1, g, b, w, cos_t, sin_t, qg, kg, lbf, lbb)


class _Groups:
    def __init__(self, n0, s0, n1, s1):
        self.n0, self.s0, self.n1, self.s1 = n0, s0, n1, s1
        self.t0 = n0 * s0
        self.total = self.t0 + n1 * s1

    def split(self, i, tile):
        nt0 = self.t0 // tile
        per0, per1 = self.s0 // tile, self.s1 // tile
        in1 = i >= nt0
        j1 = jnp.maximum(i - nt0, 0)
        seq = jnp.where(in1, self.n0 + j1 // per1, i // per0)
        pos = jnp.where(in1, j1 % per1, i % per0)
        return in1, seq, pos, jnp.where(in1, per1, per0)

    def seq_of_tile(self, i, tile):
        return self.split(i, tile)[1]

    def pos_of_tile(self, i, tile):
        return self.split(i, tile)[2]

    def reversed_tile(self, i, tile):
        _, _, pos, per = self.split(i, tile)
        return i - pos + (per - 1 - pos)


def _attention_kernel(*refs, n_kv_blocks, groups):
    bound_ref, q_ref = refs[:2]
    k_refs = refs[2:2 + n_kv_blocks]
    v_refs = refs[2 + n_kv_blocks:2 + 2 * n_kv_blocks]
    o_ref, out_ref = refs[2 + 2 * n_kv_blocks:]
    tq = q_ref.shape[0]
    kv_rows = k_refs[0].shape[0]
    tk = min(ATT_K_TILE, kv_rows)
    n_k = kv_rows // tk
    group = ATT_HEADS // ATT_KV_HEADS
    rows = group * tq
    lane = lax.broadcasted_iota(I32, (tq, LANES), 1)
    low_half = lane < ATT_HEAD_DIM
    in_group1 = groups.split(pl.program_id(0), tq)[0]
    extra_trips = jnp.where(in_group1, n_k, 0)
    unshifted_ok = bound_ref[0] <= ATT_SAFE_SCORE

    for kvh in range(ATT_KV_HEADS):
        keep = low_half if kvh == 0 else jnp.logical_not(low_half)
        parts = []
        for g in range(group):
            c0 = kvh * 2 * LANES + (g // 2) * LANES
            pair = q_ref[:, c0:c0 + LANES].astype(F32)
            if g % 2 != kvh:
                pair = pltpu.roll(pair, ATT_HEAD_DIM, 1)
            parts.append(jnp.where(keep, pair, 0.0).astype(BF16))
        qp = jnp.concatenate(parts, axis=0)

        def run(make_body, carry):
            carry = lax.fori_loop(0, n_k, make_body(k_refs[0], v_refs[0]), carry)
            for kb in range(1, n_kv_blocks):
                carry = lax.fori_loop(0, extra_trips, make_body(k_refs[kb], v_refs[kb]), carry)
            return carry

        @pl.when(unshifted_ok)
        def _():
            def make_body(k_ref, v_ref):
                def body(kc, acc):
                    r0 = pl.multiple_of(kc * tk, tk)
                    p = jnp.exp(_dot_nt(qp, k_ref[pl.ds(r0, tk), :]))
                    return acc + _dot(p.astype(BF16), v_ref[pl.ds(r0, tk), :])
                return body

            acc = run(make_body, jnp.zeros((rows, 2 * LANES), F32))
            out_ref[...] = acc[:, :LANES] / acc[:, LANES:]

        @pl.when(jnp.logical_not(unshifted_ok))
        def _():
            def make_body(k_ref, v_ref):
                def body(kc, carry):
                    m, l, acc = carry
                    r0 = pl.multiple_of(kc * tk, tk)
                    s = _dot_nt(qp, k_ref[pl.ds(r0, tk), :])
                    m_new = jnp.maximum(m, jnp.max(s, axis=-1, keepdims=True))
                    alpha = jnp.exp(m - m_new)
                    p = jnp.exp(s - m_new)
                    l_new = alpha * l + jnp.sum(p, axis=-1, keepdims=True)
                    acc_new = alpha * acc + _dot(p.astype(BF16), v_ref[pl.ds(r0, tk), :LANES])
                    return m_new, l_new, acc_new
                return body

            init = (jnp.full((rows, 1), NEG_INF, F32), jnp.zeros((rows, 1), F32), jnp.zeros((rows, LANES), F32))
            m, l, acc = run(make_body, init)
            out_ref[...] = acc / l

        out = out_ref[...]
        for j in range(group // 2):
            a = out[(2 * j) * tq:(2 * j + 1) * tq]
            b = out[(2 * j + 1) * tq:(2 * j + 2) * tq]
            if kvh == 1:
                a = pltpu.roll(a, ATT_HEAD_DIM, 1)
            else:
                b = pltpu.roll(b, ATT_HEAD_DIM, 1)
            c0 = kvh * 2 * LANES + j * LANES
            o_ref[:, c0:c0 + LANES] = jnp.where(low_half, a, b).astype(BF16)


def _attention(score_bound, q, k, v, groups):
    T = q.shape[0]
    tq = ATT_Q_TILE
    kv_rows = groups.s0
    n_kv_blocks = groups.s1 // groups.s0

    def kv_map(j):
        def index_map(i):
            in1, seq, _, _ = groups.split(i, tq)
            first = jnp.where(in1, groups.t0 // kv_rows + (seq - groups.n0) * n_kv_blocks, seq)
            return (first + jnp.where(in1, j, 0), 0)
        return index_map

    k_specs = [pl.BlockSpec((kv_rows, ATT_KV_WIDTH), kv_map(j)) for j in range(n_kv_blocks)]
    v_specs = [pl.BlockSpec((kv_rows, 2 * ATT_KV_WIDTH), kv_map(j)) for j in range(n_kv_blocks)]
    return pl.pallas_call(
        functools.partial(_attention_kernel, n_kv_blocks=n_kv_blocks, groups=groups),
        out_shape=jax.ShapeDtypeStruct((T, ATT_WIDTH), BF16),
        grid=(T // tq,),
        in_specs=[pl.BlockSpec(memory_space=pltpu.SMEM),
                  pl.BlockSpec((tq, ATT_WIDTH), lambda i: (i, 0))] + k_specs + v_specs,
        out_specs=pl.BlockSpec((tq, ATT_WIDTH), lambda i: (i, 0)),
        scratch_shapes=[pltpu.VMEM((ATT_HEADS // ATT_KV_HEADS * tq, LANES), F32)],
        compiler_params=_cparams(("parallel",)),
        name="attention",
    )(score_bound, q, *([k] * n_kv_blocks), *([v] * n_kv_blocks))


def _hgrn_kernel(*refs, reverse, finalize, groups):
    if finalize:
        q_ref, v_ref, lf_ref, of_ref, g_ref, gain_ref, o_ref, st_ref = refs
    else:
        q_ref, v_ref, lf_ref, o_ref, st_ref = refs
    tb = q_ref.shape[0]
    n_heads = q_ref.shape[1] // HG_DK
    C = min(HG_CHUNK, tb)
    n_chunk = tb // C
    n_sub = C // HG_SUB

    @pl.when(groups.pos_of_tile(pl.program_id(1), tb) == 0)
    def _():
        st_ref[...] = jnp.zeros_like(st_ref)

    ri = lax.broadcasted_iota(I32, (C, C), 0)
    ci = lax.broadcasted_iota(I32, (C, C), 1)
    sub_lo = (ri >> 4) << 4
    if reverse:
        loc_m = (ci >= ri) & (ci < sub_lo + HG_SUB)
    else:
        loc_m = (ci <= ri) & (ci >= sub_lo)
    loc_mat = jnp.where(loc_m, 1.0, 0.0).astype(BF16)
    row_id = lax.broadcasted_iota(I32, (C, 1), 0)
    sub_row = lax.broadcasted_iota(I32, (HG_SUB, 1), 0)
    lane_c = lax.broadcasted_iota(I32, (HG_SUB, C), 1)
    edge = 0 if reverse else C - 1

    def cumulative(lf):
        h1, h2, h3 = _split3(lf)
        loc = _dot(loc_mat, h1) + _dot(loc_mat, h2) + _dot(loc_mat, h3)
        subs = [loc[s * HG_SUB:(s + 1) * HG_SUB] for s in range(n_sub)]
        order = range(n_sub - 1, -1, -1) if reverse else range(n_sub)
        carry = jnp.zeros((1, HG_DK), F32)
        for s in order:
            sub_total = subs[s][0:1, :] if reverse else subs[s][HG_SUB - 1:HG_SUB, :]
            subs[s] = subs[s] + carry
            carry = carry + sub_total
        return jnp.concatenate(subs, axis=0), loc

    def head_cols(head):
        return slice(head * HG_DK, (head + 1) * HG_DK)

    def emit(r0, head, o):
        rows, cols = pl.ds(r0, C), head_cols(head)
        if finalize:
            tot = o + of_ref[rows, cols]
            ms = jnp.mean(tot * tot, axis=-1, keepdims=True)
            res = tot * lax.rsqrt(ms + RMS_EPS) * gain_ref[...] * g_ref[rows, cols].astype(F32)
            o_ref[rows, cols] = res.astype(o_ref.dtype)
        else:
            o_ref[rows, cols] = o

    def pairwise_chunk(step, carry, *, head):
        cidx = (n_chunk - 1 - step) if reverse else step
        r0 = pl.multiple_of(cidx * C, C)
        rows, cols = pl.ds(r0, C), head_cols(head)
        lf = lf_ref[rows, cols]
        q = q_ref[rows, cols].astype(F32)
        v = v_ref[rows, cols]
        cum, loc = cumulative(lf)
        kk = 1.0 - jnp.exp(lf)
        total = cum[edge:edge + 1, :]
        st = st_ref[head]
        o_inter = _dot_nt((q * jnp.exp(cum)).astype(BF16), st.astype(BF16))
        k_state = (kk * jnp.exp(total - cum)).astype(BF16)
        st_ref[head] = st * jnp.exp(total) + _dot_tn(v, k_state)
        qh_b = (q * jnp.exp(loc)).astype(BF16)
        blocks = []
        for s in range(n_sub):
            lo = s * HG_SUB
            hi = lo + HG_SUB
            has_off = (s < n_sub - 1) if reverse else (s > 0)
            if has_off:
                if reverse:
                    expo = jnp.where(row_id >= hi, cum[hi:hi + 1, :] - cum, NEG_INF)
                else:
                    expo = jnp.where(row_id < lo, cum[lo - 1:lo, :] - cum, NEG_INF)
                k_off = (kk * jnp.exp(expo)).astype(BF16)
                a_s = _dot_nt(qh_b[lo:hi], k_off)
            else:
                a_s = jnp.zeros((HG_SUB, C), F32)
            loc_s = loc[lo:hi]
            q_s = q[lo:hi]
            for j in range(HG_SUB):
                jj = lo + j
                d = loc_s - loc[jj:jj + 1, :]
                msk = (sub_row <= j) if reverse else (sub_row >= j)
                e = jnp.exp(jnp.where(msk, d, NEG_INF))
                colv = jnp.sum(q_s * e * kk[jj:jj + 1, :], axis=-1, keepdims=True)
                a_s = jnp.where(lane_c == jj, colv, a_s)
            blocks.append(a_s)
        a = jnp.concatenate(blocks, axis=0)
        emit(r0, head, o_inter + _dot(a.astype(BF16), v))
        return carry

    def staged_tile():
        items = [(((n_chunk - 1 - s) if reverse else s) * C, head) for s in range(n_chunk) for head in range(n_heads)]
        lfs = [lf_ref[r0:r0 + C, head_cols(h)] for r0, h in items]
        qs = [q_ref[r0:r0 + C, head_cols(h)].astype(F32) for r0, h in items]
        vs = [v_ref[r0:r0 + C, head_cols(h)] for r0, h in items]
        cums, locs = zip(*[cumulative(lf) for lf in lfs])
        kks = [1.0 - jnp.exp(lf) for lf in lfs]
        totals = [cum[edge:edge + 1, :] for cum in cums]
        a_mats = []
        for q, kk, cum, loc in zip(qs, kks, cums, locs):
            qh_b = (q * jnp.exp(loc)).astype(BF16)
            blocks = []
            for s in range(n_sub):
                lo = s * HG_SUB
                hi = lo + HG_SUB
                if reverse:
                    edge_row = cum[hi:hi + 1, :] if s < n_sub - 1 else jnp.zeros((1, HG_DK), F32)
                    expo = jnp.where(row_id >= lo, edge_row - cum, NEG_INF)
                    tri = lane_c >= lo + sub_row
                else:
                    edge_row = cum[lo - 1:lo, :] if s > 0 else jnp.zeros((1, HG_DK), F32)
                    expo = jnp.where(row_id < hi, edge_row - cum, NEG_INF)
                    tri = lane_c <= lo + sub_row
                k_rel = (kk * jnp.exp(expo)).astype(BF16)
                blocks.append(jnp.where(tri, _dot_nt(qh_b[lo:hi], k_rel), 0.0))
            a_mats.append(jnp.concatenate(blocks, axis=0).astype(BF16))
        intras = [_dot(a, v) for a, v in zip(a_mats, vs)]
        updates = [_dot_tn(v, (kk * jnp.exp(tot - cum)).astype(BF16))
                   for v, kk, tot, cum in zip(vs, kks, totals, cums)]
        q_ins = [(q * jnp.exp(cum)).astype(BF16) for q, cum in zip(qs, cums)]
        states = [st_ref[h] for h in range(n_heads)]
        for (r0, h), q_in, intra, tot, upd in zip(items, q_ins, intras, totals, updates):
            emit(r0, h, _dot_nt(q_in, states[h].astype(BF16)) + intra)
            states[h] = states[h] * jnp.exp(tot) + upd
        for h in range(n_heads):
            st_ref[h] = states[h]

    mild_decay = jnp.min(lf_ref[...]) >= -HG_SAFE_LOG

    @pl.when(mild_decay)
    def _():
        staged_tile()

    @pl.when(jnp.logical_not(mild_decay))
    def _():
        for head in range(n_heads):
            lax.fori_loop(0, n_chunk, functools.partial(pairwise_chunk, head=head), 0)


def _hgrn_pass(q, v, lf, groups, *, reverse, finalize_args=None):
    T = q.shape[0]
    tb = TOKEN_TILE
    finalize = finalize_args is not None

    def tmap(h, i):
        return (groups.reversed_tile(i, tb) if reverse else i, h)

    spec = pl.BlockSpec((tb, HG_DK * HG_HEADS_PER_STEP), tmap)
    in_specs = [spec, spec, spec]
    args = [q, v, lf]
    if finalize:
        o_fwd, gate, gain = finalize_args
        in_specs += [spec, spec, pl.BlockSpec((1, HG_DK), lambda h, i: (0, 0))]
        args += [o_fwd, gate, gain]
    return pl.pallas_call(
        functools.partial(_hgrn_kernel, reverse=reverse, finalize=finalize, groups=groups),
        out_shape=jax.ShapeDtypeStruct((T, HG_WIDTH), BF16 if finalize else F32),
        grid=(HG_HEADS // HG_HEADS_PER_STEP, T // tb),
        in_specs=in_specs,
        out_specs=spec,
        scratch_shapes=[pltpu.VMEM((HG_HEADS_PER_STEP, HG_DK, HG_DK), F32)],
        compiler_params=_cparams(("parallel", "arbitrary")),
        name="hgrn_bwd" if reverse else "hgrn_fwd",
    )(*args)


def _kvproj_kernel(m_ref, w_ref, o_ref):
    o_ref[...] = _dot(m_ref[...].astype(BF16), w_ref[...]).astype(BF16)


def _kvproj(mem, w):
    R, D = mem.shape
    n = w.shape[1]
    tr = 256
    return pl.pallas_call(
        _kvproj_kernel,
        out_shape=jax.ShapeDtypeStruct((R, n), BF16),
        grid=(R // tr,),
        in_specs=[pl.BlockSpec((tr, D), lambda i: (i, 0)), pl.BlockSpec((D, n), lambda i: (0, 0))],
        out_specs=pl.BlockSpec((tr, n), lambda i: (i, 0)),
        compiler_params=_cparams(("parallel",)),
        name="kvproj",
    )(mem, w)


def _mix_xattn_kernel(x0_ref, x1_ref, att_ref, rec_ref, g0_ref, b0_ref, wo_ref, g1_ref, b1_ref, wq_ref, kv_ref,
                      wxo_ref, g2_ref, b2_ref, x2_ref, p_ref, *, tiles_group0):
    D = x0_ref.shape[1]
    dh = D // X_HEADS
    x_tile = _group_rows(x0_ref, x1_ref, tiles_group0)

    def stages(rows):
        mix = _dot(att_ref[rows, :], wo_ref[:ATT_WIDTH, :]) + _dot(rec_ref[rows, :], wo_ref[ATT_WIDTH:, :])
        yield
        xn = _layer_norm(x_tile[rows], g0_ref[...], b0_ref[...])
        x1 = _layer_norm(DEEPNORM_ALPHA * xn + mix, g1_ref[...], b1_ref[...])
        qx = (_dot(x1.astype(BF16), wq_ref[...]) * (dh ** -0.5)).astype(BF16)
        yield
        scores = [_dot_nt(qx[:, h * dh:(h + 1) * dh], kv_ref[0, :, h * dh:(h + 1) * dh]) for h in range(X_HEADS)]
        yield
        heads = []
        for h, s in enumerate(scores):
            s = s - jnp.max(s, axis=-1, keepdims=True)
            p = jnp.exp(s)
            p = p / jnp.sum(p, axis=-1, keepdims=True)
            heads.append(_dot(p.astype(BF16), kv_ref[0, :, D + h * dh:D + (h + 1) * dh]).astype(BF16))
        yield
        xo = _dot(jnp.concatenate(heads, axis=-1), wxo_ref[...])
        yield
        x2 = _layer_norm(DEEPNORM_ALPHA * x1 + xo, g2_ref[...], b2_ref[...])
        x2_ref[rows, :] = x2
        _store_slabs(p_ref, _pack_bf16_pair(x2[:, :D // 2], x2[:, D // 2:]), row0=rows.start)

    _staggered(stages, x_tile.shape[0])


def _mix_xattn(x0, x1, att, rec, g0, b0, wo, g1, b1, wq, kv, wxo, g2, b2, seq_of_tile):
    D = x0.shape[1]
    T = x0.shape[0] + x1.shape[0]
    tm = TOKEN_TILE
    tiles_group0 = x0.shape[0] // tm
    n_mem = kv.shape[1]
    row = lambda i: (i, 0)
    const = lambda i: (0, 0)
    vec = pl.BlockSpec((1, D), const)
    return pl.pallas_call(
        functools.partial(_mix_xattn_kernel, tiles_group0=tiles_group0),
        out_shape=(jax.ShapeDtypeStruct((T, D), F32), jax.ShapeDtypeStruct((T * SLAB, LANES), U32)),
        grid=(T // tm,),
        in_specs=_group_specs(tm, D, tiles_group0) + [
            pl.BlockSpec((tm, ATT_WIDTH), row),
            pl.BlockSpec((tm, HG_WIDTH), row),
            vec, vec,
            pl.BlockSpec((ATT_WIDTH + HG_WIDTH, D), const),
            vec, vec,
            pl.BlockSpec((D, D), const),
            pl.BlockSpec((1, n_mem, 2 * D), lambda i: (seq_of_tile(i), 0, 0)),
            pl.BlockSpec((D, D), const),
            vec, vec,
        ],
        out_specs=(pl.BlockSpec((tm, D), row), pl.BlockSpec((tm * SLAB, LANES), row)),
        compiler_params=_cparams(("parallel",)),
        name="mix_xattn",
    )(x0, x1, att, rec, g0, b0, wo, g1, b1, wq, kv, wxo, g2, b2)


def _router_kernel(x_ref, wt_ref, b_ref, idx_ref, gate_ref, rank_ref, cnt_ref, run_ref):
    tm = x_ref.shape[0]

    @pl.when(pl.program_id(0) == 0)
    def _():
        run_ref[...] = jnp.zeros_like(run_ref)

    xh, xl, _ = _split3(x_ref[...])
    wh, wl, _ = _split3(wt_ref[...])
    logits = _dot_nt(wh, xh) + _dot_nt(wh, xl) + _dot_nt(wl, xh) + b_ref[...]
    e_id = lax.broadcasted_iota(I32, (N_EXPERTS, tm), 0)
    work = logits
    vals, idxs = [], []
    chosen = jnp.zeros((N_EXPERTS, tm), F32)
    for _ in range(TOP_K):
        m = jnp.max(work, axis=0, keepdims=True)
        idx = jnp.min(jnp.where(work == m, e_id, N_EXPERTS), axis=0, keepdims=True)
        hit = e_id == idx
        chosen = jnp.where(hit, 1.0, chosen)
        work = jnp.where(hit, NEG_INF, work)
        vals.append(m)
        idxs.append(idx)
    ex = [jnp.exp(v - vals[0]) for v in vals]
    den = ex[0] + ex[1] + ex[2] + ex[3]
    t_r = lax.broadcasted_iota(I32, (tm, tm), 0)
    t_c = lax.broadcasted_iota(I32, (tm, tm), 1)
    before = jnp.where(t_r < t_c, 1.0, 0.0).astype(BF16)
    rank_e = _dot(chosen.astype(BF16), before) + run_ref[...]
    run_ref[...] = run_ref[...] + jnp.sum(chosen, axis=1, keepdims=True)
    for k in range(TOP_K):
        idx_ref[k:k + 1, :] = idxs[k]
        gate_ref[k:k + 1, :] = ex[k] / den
        rank_ref[k:k + 1, :] = jnp.sum(jnp.where(e_id == idxs[k], rank_e, 0.0), axis=0, keepdims=True).astype(I32)
    cnt_ref[...] = run_ref[...].astype(I32)


def _router(x2, w_router_t, b_router):
    T, D = x2.shape
    tm = TOKEN_TILE
    tok = pl.BlockSpec((TOP_K, tm), lambda i: (0, i))
    return pl.pallas_call(
        _router_kernel,
        out_shape=(jax.ShapeDtypeStruct((TOP_K, T), I32), jax.ShapeDtypeStruct((TOP_K, T), F32),
                   jax.ShapeDtypeStruct((TOP_K, T), I32), jax.ShapeDtypeStruct((N_EXPERTS, 1), I32)),
        grid=(T // tm,),
        in_specs=[pl.BlockSpec((tm, D), lambda i: (i, 0)),
                  pl.BlockSpec((N_EXPERTS, D), lambda i: (0, 0)),
                  pl.BlockSpec((N_EXPERTS, 1), lambda i: (0, 0))],
        out_specs=(tok, tok, tok, pl.BlockSpec((N_EXPERTS, 1), lambda i: (0, 0))),
        scratch_shapes=[pltpu.VMEM((N_EXPERTS, 1), F32)],
        compiler_params=_cparams(("arbitrary",)),
        name="router",
    )(x2, w_router_t, b_router)


def _dispatch_kernel(fill_from_ref, fill_len_ref, dest_ref, p_ref, out_hbm, sem, fill_sem, *, n_fill):
    td = DISPATCH_TILE

    @pl.when(pl.program_id(0) == 0)
    def _():
        for r in range(fill_from_ref.shape[0]):
            first, n = fill_from_ref[r], fill_len_ref[r]
            n_chunks = lax.shift_right_logical(n, FILL_CHUNK_LOG2)

            def fill_chunk(j, c):
                rows = pl.ds(first + j * FILL_CHUNK, FILL_CHUNK)
                pltpu.make_async_copy(p_ref.at[pl.ds(0, FILL_CHUNK)], out_hbm.at[rows], fill_sem).start()
                return c

            def fill_row(j, c):
                pltpu.make_async_copy(p_ref.at[0], out_hbm.at[first + n_chunks * FILL_CHUNK + j], fill_sem).start()
                return c

            lax.fori_loop(0, n_chunks, fill_chunk, 0)
            lax.fori_loop(0, n - n_chunks * FILL_CHUNK, fill_row, 0)
        filled = out_hbm.at[pl.ds(0, n_fill)]
        pltpu.make_async_copy(filled, filled, fill_sem).wait()

    def start(t, c):
        for k in range(TOP_K):
            d = dest_ref[0, 0, k * td + t]
            pltpu.make_async_copy(p_ref.at[t], out_hbm.at[d], sem).start(priority=k % DMA_PRIORITIES)
        return c

    lax.fori_loop(0, td, start, 0, unroll=DMA_ISSUE_UNROLL)
    everything = out_hbm.at[pl.ds(0, TOP_K * td)]
    pltpu.make_async_copy(everything, everything, sem).wait()


def _dispatch(fill_from, fill_len, dest_tiles, p, n_rows):
    T = p.shape[0]
    td = DISPATCH_TILE
    grid_spec = pltpu.PrefetchScalarGridSpec(
        num_scalar_prefetch=2,
        grid=(T // td,),
        in_specs=[pl.BlockSpec((1, 1, TOP_K * td), lambda i, ff, fl: (i, 0, 0), memory_space=pltpu.SMEM),
                  pl.BlockSpec((td,) + p.shape[1:], lambda i, ff, fl: (i, 0, 0))],
        out_specs=pl.BlockSpec(memory_space=pl.ANY),
        scratch_shapes=[pltpu.SemaphoreType.DMA, pltpu.SemaphoreType.DMA],
    )
    return pl.pallas_call(
        functools.partial(_dispatch_kernel, n_fill=n_rows - T * TOP_K),
        out_shape=jax.ShapeDtypeStruct((n_rows,) + p.shape[1:], p.dtype),
        grid_spec=grid_spec,
        compiler_params=_cparams(("arbitrary",)),
        name="dispatch",
    )(fill_from, fill_len, dest_tiles, p)


def _expert_kernel(blk_e_ref, nvalid_ref, p_ref, wu_ref, bu_ref, wd_ref, bd_ref, y_ref, wu_b, wd_b):
    i = pl.program_id(0)
    dff = wd_ref.shape[1]
    half = SLAB * LANES
    live = i < nvalid_ref[0]
    new_expert = (i == 0) | (blk_e_ref[i] != blk_e_ref[jnp.maximum(i - 1, 0)])

    @pl.when(live & new_expert)
    def _():
        wu_b[...] = wu_ref[0].astype(BF16)
        wd_b[...] = wd_ref[0].astype(BF16)

    @pl.when(live)
    def _():
        lo, hi = _unpack_bf16_pair(_load_slabs(p_ref))
        lo = lo.astype(BF16)
        hi = hi.astype(BF16)

        def up(c0):
            return (_dot(lo, wu_b[:half, c0:c0 + dff]) + _dot(hi, wu_b[half:, c0:c0 + dff])
                    + bu_ref[0, :, c0:c0 + dff])

        glu = jnp.minimum(up(0), SWIGLU_LIMIT)
        lin = jnp.clip(up(dff), -SWIGLU_LIMIT, SWIGLU_LIMIT)
        a = glu * _sigmoid(SWIGLU_ALPHA * glu) * (lin + 1.0)
        y = _dot(a.astype(BF16), wd_b[...]) + bd_ref[0]
        _store_slabs(y_ref, _pack_bf16_pair(y[:, :half], y[:, half:]))

    @pl.when(i >= nvalid_ref[0])
    def _():
        y_ref[...] = jnp.zeros_like(y_ref)


def _experts(blk_expert, n_valid, ps, wu, bu, wd, bd):
    R = ps.shape[0] // SLAB
    te = EXPERT_TILE
    E, D, two_f = wu.shape
    dff = wd.shape[1]
    grid_spec = pltpu.PrefetchScalarGridSpec(
        num_scalar_prefetch=2,
        grid=(R // te,),
        in_specs=[
            pl.BlockSpec((te * SLAB, LANES), lambda i, be, nv: (i, 0)),
            pl.BlockSpec((1, D, two_f), lambda i, be, nv: (be[i], 0, 0)),
            pl.BlockSpec((1, 1, two_f), lambda i, be, nv: (be[i], 0, 0)),
            pl.BlockSpec((1, dff, D), lambda i, be, nv: (be[i], 0, 0)),
            pl.BlockSpec((1, 1, D), lambda i, be, nv: (be[i], 0, 0)),
        ],
        out_specs=pl.BlockSpec((te * SLAB, LANES), lambda i, be, nv: (i, 0)),
        scratch_shapes=[pltpu.VMEM((D, two_f), BF16), pltpu.VMEM((dff, D), BF16)],
    )
    return pl.pallas_call(
        _expert_kernel,
        out_shape=jax.ShapeDtypeStruct((R * SLAB, LANES), U32),
        grid_spec=grid_spec,
        compiler_params=_cparams(("arbitrary",)),
        name="experts",
    )(blk_expert, n_valid, ps, wu, bu, wd, bd)


def _combine_kernel(dest_ref, dest_next_ref, ys_hbm, x2_ref, gate_ref, g_ref, b_ref, o_ref, buf, sems):
    tc = x2_ref.shape[0]
    half = SLAB * LANES
    i = pl.program_id(0)
    slot = i % 2

    def start_tile(idx_ref, s):
        def body(t, c):
            rows = pl.ds(pl.multiple_of(t * SLAB, SLAB), SLAB)
            for k in range(TOP_K):
                d = idx_ref[0, 0, k * tc + t]
                pltpu.make_async_copy(ys_hbm.at[d], buf.at[s, k, rows], sems.at[s]).start(
                    priority=k % DMA_PRIORITIES)
            return c
        lax.fori_loop(0, tc, body, 0, unroll=DMA_ISSUE_UNROLL)

    @pl.when(i == 0)
    def _():
        start_tile(dest_ref, 0)

    for s in range(2):
        @pl.when((i + 1 < pl.num_programs(0)) & (slot == s))
        def _():
            start_tile(dest_next_ref, 1 - s)

    for s in range(2):
        @pl.when(slot == s)
        def _():
            pltpu.make_async_copy(buf.at[s], buf.at[s], sems.at[s]).wait()

    gates = gate_ref[...]
    acc_lo = jnp.zeros((tc, half), F32)
    acc_hi = jnp.zeros((tc, half), F32)
    for k in range(TOP_K):
        lo, hi = _unpack_bf16_pair(_load_slabs(buf.at[slot, k]))
        gk = gates[:, k:k + 1]
        acc_lo = acc_lo + gk * lo
        acc_hi = acc_hi + gk * hi
    ffn = jnp.concatenate([acc_lo, acc_hi], axis=-1)
    o_ref[...] = _layer_norm(DEEPNORM_ALPHA * x2_ref[...] + ffn, g_ref[...], b_ref[...])


def _combine(dest_tiles, ys, x2, gates, g, b, *, tile0, n_tiles):
    D = x2.shape[1]
    tc = COMBINE_TILE
    return pl.pallas_call(
        _combine_kernel,
        out_shape=jax.ShapeDtypeStruct((n_tiles * tc, D), F32),
        grid=(n_tiles,),
        in_specs=[pl.BlockSpec((1, 1, TOP_K * tc), lambda i: (tile0 + i, 0, 0), memory_space=pltpu.SMEM),
                  pl.BlockSpec((1, 1, TOP_K * tc), lambda i: (tile0 + jnp.minimum(i + 1, n_tiles - 1), 0, 0),
                               memory_space=pltpu.SMEM),
                  pl.BlockSpec(memory_space=pl.ANY),
                  pl.BlockSpec((tc, D), lambda i: (tile0 + i, 0)),
                  pl.BlockSpec((tc, TOP_K), lambda i: (tile0 + i, 0)),
                  pl.BlockSpec((1, D), lambda i: (0, 0)),
                  pl.BlockSpec((1, D), lambda i: (0, 0))],
        out_specs=pl.BlockSpec((tc, D), lambda i: (i, 0)),
        scratch_shapes=[pltpu.VMEM((2, TOP_K, tc * SLAB, LANES), U32), pltpu.SemaphoreType.DMA((2,))],
        compiler_params=_cparams(("arbitrary",)),
        name="combine",
    )(dest_tiles, dest_tiles, ys, x2, gates, g, b)


def _rope_tables(seq_len):
    rows = seq_len // GRID_W
    row = jnp.repeat(jnp.arange(rows, dtype=F32), GRID_W)
    colp = (jnp.arange(seq_len) % GRID_W).astype(F32)
    axis_dim = ATT_HEAD_DIM // 2
    inv_freq = ROPE_THETA ** (-jnp.arange(0, axis_dim, 2, dtype=F32) / axis_dim)
    ang = jnp.concatenate([row[:, None] * inv_freq, colp[:, None] * inv_freq], axis=-1)
    cos, sin = jnp.cos(ang), jnp.sin(ang)
    cos_h = jnp.concatenate([cos, cos], axis=-1)
    sin_h = jnp.concatenate([-sin, sin], axis=-1)
    return jnp.tile(cos_h, (1, LANES // ATT_HEAD_DIM)), jnp.tile(sin_h, (1, LANES // ATT_HEAD_DIM))


def _tiles_of(idx_t, tile):
    K, T = idx_t.shape
    return idx_t.reshape(K, T // tile, tile).transpose(1, 0, 2).reshape(T // tile, 1, K * tile)


def kernel(x_prompt, x_sample, mem_prompt, mem_sample, ln_in_g, ln_in_b, w_in, q_norm_g, k_norm_g, hg_lb_fwd, hg_lb_bwd, hg_norm_g, w_out, ln1_g, ln1_b, w_xq, w_xkv, w_xo, ln2_g, ln2_b, w_router, b_router, w_up, b_up, w_down, b_down, ln3_g, ln3_b):
    Bp, Sp, D = x_prompt.shape
    Bs, Ss, _ = x_sample.shape
    Tp, Ts = Bp * Sp, Bs * Ss
    T = Tp + Ts
    tm = TOKEN_TILE
    assert Sp % tm == 0 and Ss % Sp == 0
    n_mem = mem_prompt.shape[1]
    layer = 0

    x0, x1 = x_prompt.reshape(Tp, D), x_sample.reshape(Ts, D)
    mem = jnp.concatenate([mem_prompt.reshape(Bp * n_mem, D), mem_sample.reshape(Bs * n_mem, D)], axis=0)

    def deinterleave(a):
        lead = a.shape[:-1]
        a = a.reshape(*lead, -1, ATT_HEAD_DIM // 2, 2)
        return jnp.swapaxes(a, -1, -2).reshape(*lead, -1)

    n_qk = ATT_WIDTH + ATT_KV_WIDTH
    w_in_l = w_in[layer]
    w_in_b = jnp.concatenate([deinterleave(w_in_l[:, :n_qk]), w_in_l[:, n_qk:]], axis=1).astype(BF16)
    pair = LANES // ATT_HEAD_DIM
    qg = jnp.tile(deinterleave(q_norm_g[layer]), pair).reshape(1, LANES)
    kg = jnp.tile(deinterleave(k_norm_g[layer]), pair).reshape(1, LANES)

    def lower_bound(logits):
        sm = jax.nn.softmax(logits.astype(F32), axis=0)
        return (jnp.cumsum(sm, axis=0)[layer + 1] - sm[0]).reshape(1, HG_WIDTH)

    lbf, lbb = lower_bound(hg_lb_fwd), lower_bound(hg_lb_bwd)
    cos_t, sin_t = _rope_tables(max(Sp, Ss))
    vec = lambda a: a.reshape(1, -1)

    groups = _Groups(Bp, Sp, Bs, Ss)
    pos_block = lambda i: groups.pos_of_tile(i, tm)
    seq_of_tile = lambda i: groups.seq_of_tile(i, tm)

    q, k, v, hq, hi, hg, lff, lfb = _inproj(x0, x1, vec(ln_in_g), vec(ln_in_b), w_in_b, cos_t, sin_t, qg, kg,
                                           lbf, lbb, pos_block)

    score_bound = (jnp.max(jnp.abs(q_norm_g[layer])) * jnp.max(jnp.abs(k_norm_g[layer]))
                   * (ATT_HEAD_DIM ** 0.5 * ATT_NORM_SLACK)).astype(F32).reshape(1)
    att = _attention(score_bound, q, k, v, groups)
    o_fwd = _hgrn_pass(hq, hi, lff, groups, reverse=False)
    gain = hg_norm_g[layer].reshape(1, HG_DK)
    rec = _hgrn_pass(hq, hi, lfb, groups, reverse=True, finalize_args=(o_fwd, hg, gain))

    kv = _kvproj(mem, w_xkv[layer].astype(BF16)).reshape(Bp + Bs, n_mem, 2 * D)
    x2, packed = _mix_xattn(x0, x1, att, rec, vec(ln_in_g), vec(ln_in_b), w_out[layer].astype(BF16),
                            vec(ln1_g[layer]), vec(ln1_b[layer]), w_xq[layer].astype(BF16), kv,
                            w_xo[layer].astype(BF16), vec(ln2_g[layer]), vec(ln2_b[layer]), seq_of_tile)

    idx_t, gate_t, rank_t, counts = _router(x2, w_router[layer].T, b_router[layer].reshape(N_EXPERTS, 1))

    te = EXPERT_TILE
    counts = counts.reshape(N_EXPERTS)
    padded = (counts + te - 1) // te * te
    pad_end = jnp.cumsum(padded)
    pad_start = pad_end - padded
    n_blk = (T * TOP_K + N_EXPERTS * (te - 1) + te - 1) // te
    e_ids = jnp.arange(N_EXPERTS, dtype=I32)
    start_of = jnp.sum(jnp.where(idx_t[None] == e_ids[:, None, None], pad_start[:, None, None], 0), axis=0)
    dest_t = start_of + rank_t
    blk_row0 = jnp.arange(n_blk, dtype=I32) * te
    blk_expert = jnp.minimum(jnp.sum((pad_end[None, :] <= blk_row0[:, None]).astype(I32), axis=1), N_EXPERTS - 1)
    n_valid = (pad_end[-1] // te).astype(I32).reshape(1)

    n_rows = n_blk * te
    fill_from = jnp.concatenate([pad_start + counts, pad_end[-1:]]).astype(I32)
    fill_len = jnp.concatenate([padded - counts, n_rows - pad_end[-1:]]).astype(I32)
    sorted_rows = _dispatch(fill_from, fill_len, _tiles_of(dest_t, DISPATCH_TILE),
                            packed.reshape(T, SLAB, LANES), n_rows)
    ys = _experts(blk_expert, n_valid, sorted_rows.reshape(n_rows * SLAB, LANES), w_up[layer],
                  b_up[layer].reshape(N_EXPERTS, 1, -1), w_down[layer], b_down[layer].reshape(N_EXPERTS, 1, -1))
    ys = ys.reshape(n_rows, SLAB, LANES)

    dest_c = _tiles_of(dest_t, COMBINE_TILE)
    gates = gate_t.T
    tc = COMBINE_TILE
    y_p = _combine(dest_c, ys, x2, gates, vec(ln3_g[layer]), vec(ln3_b[layer]), tile0=0, n_tiles=Tp // tc)
    y_s = _combine(dest_c, ys, x2, gates, vec(ln3_g[layer]), vec(ln3_b[layer]), tile0=Tp // tc, n_tiles=Ts // tc)
    return y_p.reshape(Bp, Sp, D), y_s.reshape(Bs, Ss, D)
```

```python
import functools

import jax
import jax.numpy as jnp
import numpy as np
from jax import lax
from jax.experimental import pallas as pl
from jax.experimental.pallas import tpu as pltpu

F32 = jnp.float32
BF16 = jnp.bfloat16
I32 = jnp.int32
U32 = jnp.uint32

GRID_W = 64
ATT_HEADS = 8
ATT_KV_HEADS = 2
ATT_HEAD_DIM = 64
ATT_WIDTH = ATT_HEADS * ATT_HEAD_DIM
ATT_KV_WIDTH = ATT_KV_HEADS * ATT_HEAD_DIM
ROPE_THETA = 10000.0
HG_HEADS = 4
HG_DK = 128
HG_WIDTH = HG_HEADS * HG_DK
X_HEADS = 4
N_EXPERTS = 32
TOP_K = 4
SWIGLU_LIMIT = 7.0
SWIGLU_ALPHA = 1.702
LN_EPS = 1e-5
RMS_EPS = 1e-6
DEPTH = 1
DEEPNORM_ALPHA = (2.0 * DEPTH) ** 0.25

LANES = 128
VMEM_LIMIT_BYTES = 56 * 1024 * 1024
DMA_PRIORITIES = 2
DMA_ISSUE_UNROLL = 4

TOKEN_TILE = 512
ATT_Q_TILE = 256
ATT_K_TILE = 4096
HG_CHUNK = 64
HG_SUB = 16
HG_HEADS_PER_STEP = 4

ATT_SAFE_SCORE = 60.0
ATT_NORM_SLACK = 1.02
HG_SAFE_LOG = 5.0
EXPERT_TILE = 512
COMBINE_TILE = 512
DISPATCH_TILE = 1024
FILL_CHUNK_LOG2 = 5
FILL_CHUNK = 1 << FILL_CHUNK_LOG2
ROW_PARTS = 2
NEG_INF = float("-inf")


def _cparams(sem):
    return pltpu.CompilerParams(dimension_semantics=sem, vmem_limit_bytes=VMEM_LIMIT_BYTES)


def _layer_norm(x, g, b):
    mu = jnp.mean(x, axis=-1, keepdims=True)
    xc = x - mu
    var = jnp.mean(xc * xc, axis=-1, keepdims=True)
    return xc * lax.rsqrt(var + LN_EPS) * g + b


def _sigmoid(x):
    return 1.0 / (1.0 + jnp.exp(-x))


def _dot(a, b):
    return jnp.dot(a, b, preferred_element_type=F32)


def _dot_nt(a, b):
    return lax.dot_general(a, b, (((1,), (1,)), ((), ())), preferred_element_type=F32)


def _dot_tn(a, b):
    return lax.dot_general(a, b, (((0,), (0,)), ((), ())), preferred_element_type=F32)


def _split3(x):
    h1 = x.astype(BF16)
    r1 = x - h1.astype(F32)
    h2 = r1.astype(BF16)
    h3 = (r1 - h2.astype(F32)).astype(BF16)
    return h1, h2, h3


def _pack_bf16_pair(lo, hi):
    lo_b = pltpu.bitcast(lo.astype(BF16).astype(F32), U32)
    hi_b = pltpu.bitcast(hi.astype(BF16).astype(F32), U32)
    return (hi_b & jnp.uint32(0xFFFF0000)) | (lo_b >> 16)


def _unpack_bf16_pair(p):
    lo = pltpu.bitcast(p << 16, F32)
    hi = pltpu.bitcast(p & jnp.uint32(0xFFFF0000), F32)
    return lo, hi


SLAB = 4


def _store_slabs(ref, value, row0=0):
    n = value.shape[0]
    for s in range(SLAB):
        ref[pl.ds(row0 * SLAB + s, n, stride=SLAB), :] = value[:, s * LANES:(s + 1) * LANES]


def _load_slabs(ref, row0=0, n=None):
    n = ref.shape[0] // SLAB if n is None else n
    return jnp.concatenate([ref[pl.ds(row0 * SLAB + s, n, stride=SLAB), :] for s in range(SLAB)], axis=-1)


def _row_parts(n_rows):
    part = n_rows // ROW_PARTS
    return [slice(r * part, (r + 1) * part) for r in range(ROW_PARTS)]


def _staggered(stage_fn, n_rows):
    pending = [stage_fn(rows) for rows in _row_parts(n_rows)]
    while pending:
        for g in list(pending):
            try:
                next(g)
            except StopIteration:
                pending.remove(g)


def _group_rows(x0_ref, x1_ref, tiles_group0):
    return jnp.where(pl.program_id(0) < tiles_group0, x0_ref[...], x1_ref[...])


def _group_specs(tile, width, tiles_group0):
    return [pl.BlockSpec((tile, width), lambda i: (jnp.minimum(i, tiles_group0 - 1), 0)),
            pl.BlockSpec((tile, width), lambda i: (jnp.maximum(i - tiles_group0, 0), 0))]


def _inproj_kernel(x0_ref, x1_ref, g_ref, b_ref, w_ref, cos_ref, sin_ref, qg_ref, kg_ref, lbf_ref, lbb_ref,
                   q_out, k_out, v_out, hq_out, hi_out, hg_out, lff_out, lfb_out, *, tiles_group0):
    xn = _layer_norm(_group_rows(x0_ref, x1_ref, tiles_group0), g_ref[...], b_ref[...])
    xb = xn.astype(BF16)
    cos = cos_ref[...]
    sin = sin_ref[...]
    tm = xb.shape[0]

    lane = lax.broadcasted_iota(I32, (tm, LANES), 1)
    first_half = (lane & (ATT_HEAD_DIM - 1)) < (ATT_HEAD_DIM // 2)
    r = lax.broadcasted_iota(I32, (LANES, LANES), 0)
    c = lax.broadcasted_iota(I32, (LANES, LANES), 1)
    head_ones = jnp.where((r >> 6) == (c >> 6), 1.0, 0.0).astype(BF16)

    def normed_rope(u, gain, scale):
        sq = u * u
        s_hi = sq.astype(BF16)
        s_lo = (sq - s_hi.astype(F32)).astype(BF16)
        ms = (_dot(s_hi, head_ones) + _dot(s_lo, head_ones)) * (1.0 / ATT_HEAD_DIM)
        un = u * lax.rsqrt(ms + RMS_EPS) * gain
        rot = jnp.where(first_half, pltpu.roll(un, LANES - 32, 1), pltpu.roll(un, 32, 1))
        return (un * cos + rot * sin) * scale

    n_qkv = ATT_WIDTH + 2 * ATT_KV_WIDTH
    u_att = [_dot(xb, w_ref[:, c:c + 2 * LANES]) for c in range(0, n_qkv, 2 * LANES)]
    col = n_qkv

    def hg_proj():
        nonlocal col
        u = _dot(xb, w_ref[:, col:col + HG_WIDTH])
        col += HG_WIDTH
        return u

    u = hg_proj()
    hq_out[...] = (u * _sigmoid(u)).astype(BF16)
    for j in range(ATT_WIDTH // (2 * LANES)):
        for h in range(2):
            c0 = (2 * j + h) * LANES
            q_out[:, c0:c0 + LANES] = normed_rope(u_att[j][:, h * LANES:(h + 1) * LANES], qg_ref[...],
                                                  ATT_HEAD_DIM ** -0.5).astype(BF16)
        u = hg_proj()
        lb = (lbf_ref, lbb_ref)[j][...]
        (lff_out, lfb_out)[j][...] = jnp.log(lb + (1.0 - lb) * _sigmoid(u))
    k_out[...] = normed_rope(u_att[-1][:, :LANES], kg_ref[...], 1.0).astype(BF16)
    v_out[:, :LANES] = u_att[-1][:, LANES:].astype(BF16)
    v_out[:, LANES:] = jnp.ones((tm, LANES), BF16)
    hi_out[...] = hg_proj().astype(BF16)
    u = hg_proj()
    hg_out[...] = (u * _sigmoid(u)).astype(BF16)


def _inproj(x0, x1, g, b, w, cos_t, sin_t, qg, kg, lbf, lbb, pos_block):
    D = x0.shape[1]
    T = x0.shape[0] + x1.shape[0]
    tm = TOKEN_TILE
    tiles_group0 = x0.shape[0] // tm
    n_in = w.shape[1]
    row = lambda i: (i, 0)
    const = lambda i: (0, 0)
    out_shape = (
        jax.ShapeDtypeStruct((T, ATT_WIDTH), BF16),
        jax.ShapeDtypeStruct((T, ATT_KV_WIDTH), BF16),
        jax.ShapeDtypeStruct((T, 2 * ATT_KV_WIDTH), BF16),
        jax.ShapeDtypeStruct((T, HG_WIDTH), BF16),
        jax.ShapeDtypeStruct((T, HG_WIDTH), BF16),
        jax.ShapeDtypeStruct((T, HG_WIDTH), BF16),
        jax.ShapeDtypeStruct((T, HG_WIDTH), F32),
        jax.ShapeDtypeStruct((T, HG_WIDTH), F32),
    )
    out_specs = (
        pl.BlockSpec((tm, ATT_WIDTH), row),
        pl.BlockSpec((tm, ATT_KV_WIDTH), row),
        pl.BlockSpec((tm, 2 * ATT_KV_WIDTH), row),
        pl.BlockSpec((tm, HG_WIDTH), row),
        pl.BlockSpec((tm, HG_WIDTH), row),
        pl.BlockSpec((tm, HG_WIDTH), row),
        pl.BlockSpec((tm, HG_WIDTH), row),
        pl.BlockSpec((tm, HG_WIDTH), row),
    )
    return pl.pallas_call(
        functools.partial(_inproj_kernel, tiles_group0=tiles_group0),
        out_shape=out_shape,
        grid=(T // tm,),
        in_specs=_group_specs(tm, D, tiles_group0) + [
            pl.BlockSpec((1, D), const),
            pl.BlockSpec((1, D), const),
            pl.BlockSpec((D, n_in), const),
            pl.BlockSpec((tm, LANES), lambda i: (pos_block(i), 0)),
            pl.BlockSpec((tm, LANES), lambda i: (pos_block(i), 0)),
            pl.BlockSpec((1, LANES), const),
            pl.BlockSpec((1, LANES), const),
            pl.BlockSpec((1, HG_WIDTH), const),
            pl.BlockSpec((1, HG_WIDTH), const),
        ],
        out_specs=out_specs,
        compiler_params=_cparams(("parallel",)),
        name="inproj",
    )(x0, x1, g, b, w, cos_t, sin_t, qg, kg, lbf, lbb)


class _Groups:
    def __init__(self, n0, s0, n1, s1):
        self.n0, self.s0, self.n1, self.s1 = n0, s0, n1, s1
        self.t0 = n0 * s0
        self.total = self.t0 + n1 * s1

    def split(self, i, tile):
        nt0 = self.t0 // tile
        per0, per1 = self.s0 // tile, self.s1 // tile
        in1 = i >= nt0
        j1 = jnp.maximum(i - nt0, 0)
        seq = jnp.where(in1, self.n0 + j1 // per1, i // per0)
        pos = jnp.where(in1, j1 % per1, i % per0)
        return in1, seq, pos, jnp.where(in1, per1, per0)

    def seq_of_tile(self, i, tile):
        return self.split(i, tile)[1]

    def pos_of_tile(self, i, tile):
        return self.split(i, tile)[2]

    def reversed_tile(self, i, tile):
        _, _, pos, per = self.split(i, tile)
        return i - pos + (per - 1 - pos)


def _attention_kernel(*refs, n_kv_blocks, groups):
    bound_ref, q_ref = refs[:2]
    k_refs = refs[2:2 + n_kv_blocks]
    v_refs = refs[2 + n_kv_blocks:2 + 2 * n_kv_blocks]
    o_ref, out_ref = refs[2 + 2 * n_kv_blocks:]
    tq = q_ref.shape[0]
    kv_rows = k_refs[0].shape[0]
    tk = min(ATT_K_TILE, kv_rows)
    n_k = kv_rows // tk
    group = ATT_HEADS // ATT_KV_HEADS
    rows = group * tq
    lane = lax.broadcasted_iota(I32, (tq, LANES), 1)
    low_half = lane < ATT_HEAD_DIM
    in_group1 = groups.split(pl.program_id(0), tq)[0]
    extra_trips = jnp.where(in_group1, n_k, 0)
    unshifted_ok = bound_ref[0] <= ATT_SAFE_SCORE

    for kvh in range(ATT_KV_HEADS):
        keep = low_half if kvh == 0 else jnp.logical_not(low_half)
        parts = []
        for g in range(group):
            c0 = kvh * 2 * LANES + (g // 2) * LANES
            pair = q_ref[:, c0:c0 + LANES].astype(F32)
            if g % 2 != kvh:
                pair = pltpu.roll(pair, ATT_HEAD_DIM, 1)
            parts.append(jnp.where(keep, pair, 0.0).astype(BF16))
        qp = jnp.concatenate(parts, axis=0)

        def run(make_body, carry):
            carry = lax.fori_loop(0, n_k, make_body(k_refs[0], v_refs[0]), carry)
            for kb in range(1, n_kv_blocks):
                carry = lax.fori_loop(0, extra_trips, make_body(k_refs[kb], v_refs[kb]), carry)
            return carry

        @pl.when(unshifted_ok)
        def _():
            def make_body(k_ref, v_ref):
                def body(kc, acc):
                    r0 = pl.multiple_of(kc * tk, tk)
                    p = jnp.exp(_dot_nt(qp, k_ref[pl.ds(r0, tk), :]))
                    return acc + _dot(p.astype(BF16), v_ref[pl.ds(r0, tk), :])
                return body

            acc = run(make_body, jnp.zeros((rows, 2 * LANES), F32))
            out_ref[...] = acc[:, :LANES] / acc[:, LANES:]

        @pl.when(jnp.logical_not(unshifted_ok))
        def _():
            def make_body(k_ref, v_ref):
                def body(kc, carry):
                    m, l, acc = carry
                    r0 = pl.multiple_of(kc * tk, tk)
                    s = _dot_nt(qp, k_ref[pl.ds(r0, tk), :])
                    m_new = jnp.maximum(m, jnp.max(s, axis=-1, keepdims=True))
                    alpha = jnp.exp(m - m_new)
                    p = jnp.exp(s - m_new)
                    l_new = alpha * l + jnp.sum(p, axis=-1, keepdims=True)
                    acc_new = alpha * acc + _dot(p.astype(BF16), v_ref[pl.ds(r0, tk), :LANES])
                    return m_new, l_new, acc_new
                return body

            init = (jnp.full((rows, 1), NEG_INF, F32), jnp.zeros((rows, 1), F32), jnp.zeros((rows, LANES), F32))
            m, l, acc = run(make_body, init)
            out_ref[...] = acc / l

        out = out_ref[...]
        for j in range(group // 2):
            a = out[(2 * j) * tq:(2 * j + 1) * tq]
            b = out[(2 * j + 1) * tq:(2 * j + 2) * tq]
            if kvh == 1:
                a = pltpu.roll(a, ATT_HEAD_DIM, 1)
            else:
                b = pltpu.roll(b, ATT_HEAD_DIM, 1)
            c0 = kvh * 2 * LANES + j * LANES
            o_ref[:, c0:c0 + LANES] = jnp.where(low_half, a, b).astype(BF16)


def _attention(score_bound, q, k, v, groups):
    T = q.shape[0]
    tq = ATT_Q_TILE
    kv_rows = groups.s0
    n_kv_blocks = groups.s1 // groups.s0

    def kv_map(j):
        def index_map(i):
            in1, seq, _, _ = groups.split(i, tq)
            first = jnp.where(in1, groups.t0 // kv_rows + (seq - groups.n0) * n_kv_blocks, seq)
            return (first + jnp.where(in1, j, 0), 0)
        return index_map

    k_specs = [pl.BlockSpec((kv_rows, ATT_KV_WIDTH), kv_map(j)) for j in range(n_kv_blocks)]
    v_specs = [pl.BlockSpec((kv_rows, 2 * ATT_KV_WIDTH), kv_map(j)) for j in range(n_kv_blocks)]
    return pl.pallas_call(
        functools.partial(_attention_kernel, n_kv_blocks=n_kv_blocks, groups=groups),
        out_shape=jax.ShapeDtypeStruct((T, ATT_WIDTH), BF16),
        grid=(T // tq,),
        in_specs=[pl.BlockSpec(memory_space=pltpu.SMEM),
                  pl.BlockSpec((tq, ATT_WIDTH), lambda i: (i, 0))] + k_specs + v_specs,
        out_specs=pl.BlockSpec((tq, ATT_WIDTH), lambda i: (i, 0)),
        scratch_shapes=[pltpu.VMEM((ATT_HEADS // ATT_KV_HEADS * tq, LANES), F32)],
        compiler_params=_cparams(("parallel",)),
        name="attention",
    )(score_bound, q, *([k] * n_kv_blocks), *([v] * n_kv_blocks))


def _hgrn_kernel(*refs, reverse, finalize, groups):
    if finalize:
        q_ref, v_ref, lf_ref, of_ref, g_ref, gain_ref, o_ref, st_ref = refs
    else:
        q_ref, v_ref, lf_ref, o_ref, st_ref = refs
    tb = q_ref.shape[0]
    n_heads = q_ref.shape[1] // HG_DK
    C = min(HG_CHUNK, tb)
    n_chunk = tb // C
    n_sub = C // HG_SUB

    @pl.when(groups.pos_of_tile(pl.program_id(1), tb) == 0)
    def _():
        st_ref[...] = jnp.zeros_like(st_ref)

    ri = lax.broadcasted_iota(I32, (C, C), 0)
    ci = lax.broadcasted_iota(I32, (C, C), 1)
    sub_lo = (ri >> 4) << 4
    if reverse:
        loc_m = (ci >= ri) & (ci < sub_lo + HG_SUB)
    else:
        loc_m = (ci <= ri) & (ci >= sub_lo)
    loc_mat = jnp.where(loc_m, 1.0, 0.0).astype(BF16)
    row_id = lax.broadcasted_iota(I32, (C, 1), 0)
    sub_row = lax.broadcasted_iota(I32, (HG_SUB, 1), 0)
    lane_c = lax.broadcasted_iota(I32, (HG_SUB, C), 1)
    edge = 0 if reverse else C - 1

    def cumulative(lf):
        h1, h2, h3 = _split3(lf)
        loc = _dot(loc_mat, h1) + _dot(loc_mat, h2) + _dot(loc_mat, h3)
        subs = [loc[s * HG_SUB:(s + 1) * HG_SUB] for s in range(n_sub)]
        order = range(n_sub - 1, -1, -1) if reverse else range(n_sub)
        carry = jnp.zeros((1, HG_DK), F32)
        for s in order:
            sub_total = subs[s][0:1, :] if reverse else subs[s][HG_SUB - 1:HG_SUB, :]
            subs[s] = subs[s] + carry
            carry = carry + sub_total
        return jnp.concatenate(subs, axis=0), loc

    def head_cols(head):
        return slice(head * HG_DK, (head + 1) * HG_DK)

    def emit(r0, head, o):
        rows, cols = pl.ds(r0, C), head_cols(head)
        if finalize:
            tot = o + of_ref[rows, cols]
            ms = jnp.mean(tot * tot, axis=-1, keepdims=True)
            res = tot * lax.rsqrt(ms + RMS_EPS) * gain_ref[...] * g_ref[rows, cols].astype(F32)
            o_ref[rows, cols] = res.astype(o_ref.dtype)
        else:
            o_ref[rows, cols] = o

    def pairwise_chunk(step, carry, *, head):
        cidx = (n_chunk - 1 - step) if reverse else step
        r0 = pl.multiple_of(cidx * C, C)
        rows, cols = pl.ds(r0, C), head_cols(head)
        lf = lf_ref[rows, cols]
        q = q_ref[rows, cols].astype(F32)
        v = v_ref[rows, cols]
        cum, loc = cumulative(lf)
        kk = 1.0 - jnp.exp(lf)
        total = cum[edge:edge + 1, :]
        st = st_ref[head]
        o_inter = _dot_nt((q * jnp.exp(cum)).astype(BF16), st.astype(BF16))
        k_state = (kk * jnp.exp(total - cum)).astype(BF16)
        st_ref[head] = st * jnp.exp(total) + _dot_tn(v, k_state)
        qh_b = (q * jnp.exp(loc)).astype(BF16)
        blocks = []
        for s in range(n_sub):
            lo = s * HG_SUB
            hi = lo + HG_SUB
            has_off = (s < n_sub - 1) if reverse else (s > 0)
            if has_off:
                if reverse:
                    expo = jnp.where(row_id >= hi, cum[hi:hi + 1, :] - cum, NEG_INF)
                else:
                    expo = jnp.where(row_id < lo, cum[lo - 1:lo, :] - cum, NEG_INF)
                k_off = (kk * jnp.exp(expo)).astype(BF16)
                a_s = _dot_nt(qh_b[lo:hi], k_off)
            else:
                a_s = jnp.zeros((HG_SUB, C), F32)
            loc_s = loc[lo:hi]
            q_s = q[lo:hi]
            for j in range(HG_SUB):
                jj = lo + j
                d = loc_s - loc[jj:jj + 1, :]
                msk = (sub_row <= j) if reverse else (sub_row >= j)
                e = jnp.exp(jnp.where(msk, d, NEG_INF))
                colv = jnp.sum(q_s * e * kk[jj:jj + 1, :], axis=-1, keepdims=True)
                a_s = jnp.where(lane_c == jj, colv, a_s)
            blocks.append(a_s)
        a = jnp.concatenate(blocks, axis=0)
        emit(r0, head, o_inter + _dot(a.astype(BF16), v))
        return carry

    def staged_tile():
        items = [(((n_chunk - 1 - s) if reverse else s) * C, head) for s in range(n_chunk) for head in range(n_heads)]
        lfs = [lf_ref[r0:r0 + C, head_cols(h)] for r0, h in items]
        qs = [q_ref[r0:r0 + C, head_cols(h)].astype(F32) for r0, h in items]
        vs = [v_ref[r0:r0 + C, head_cols(h)] for r0, h in items]
        cums, locs = zip(*[cumulative(lf) for lf in lfs])
        kks = [1.0 - jnp.exp(lf) for lf in lfs]
        totals = [cum[edge:edge + 1, :] for cum in cums]
        a_mats = []
        for q, kk, cum, loc in zip(qs, kks, cums, locs):
            qh_b = (q * jnp.exp(loc)).astype(BF16)
            blocks = []
            for s in range(n_sub):
                lo = s * HG_SUB
                hi = lo + HG_SUB
                if reverse:
                    edge_row = cum[hi:hi + 1, :] if s < n_sub - 1 else jnp.zeros((1, HG_DK), F32)
                    expo = jnp.where(row_id >= lo, edge_row - cum, NEG_INF)
                    tri = lane_c >= lo + sub_row
                else:
                    edge_row = cum[lo - 1:lo, :] if s > 0 else jnp.zeros((1, HG_DK), F32)
                    expo = jnp.where(row_id < hi, edge_row - cum, NEG_INF)
                    tri = lane_c <= lo + sub_row
                k_rel = (kk * jnp.exp(expo)).astype(BF16)
                blocks.append(jnp.where(tri, _dot_nt(qh_b[lo:hi], k_rel), 0.0))
            a_mats.append(jnp.concatenate(blocks, axis=0).astype(BF16))
        intras = [_dot(a, v) for a, v in zip(a_mats, vs)]
        updates = [_dot_tn(v, (kk * jnp.exp(tot - cum)).astype(BF16))
                   for v, kk, tot, cum in zip(vs, kks, totals, cums)]
        q_ins = [(q * jnp.exp(cum)).astype(BF16) for q, cum in zip(qs, cums)]
        states = [st_ref[h] for h in range(n_heads)]
        for (r0, h), q_in, intra, tot, upd in zip(items, q_ins, intras, totals, updates):
            emit(r0, h, _dot_nt(q_in, states[h].astype(BF16)) + intra)
            states[h] = states[h] * jnp.exp(tot) + upd
        for h in range(n_heads):
            st_ref[h] = states[h]

    mild_decay = jnp.min(lf_ref[...]) >= -HG_SAFE_LOG

    @pl.when(mild_decay)
    def _():
        staged_tile()

    @pl.when(jnp.logical_not(mild_decay))
    def _():
        for head in range(n_heads):
            lax.fori_loop(0, n_chunk, functools.partial(pairwise_chunk, head=head), 0)


def _hgrn_pass(q, v, lf, groups, *, reverse, finalize_args=None):
    T = q.shape[0]
    tb = TOKEN_TILE
    finalize = finalize_args is not None

    def tmap(h, i):
        return (groups.reversed_tile(i, tb) if reverse else i, h)

    spec = pl.BlockSpec((tb, HG_DK * HG_HEADS_PER_STEP), tmap)
    in_specs = [spec, spec, spec]
    args = [q, v, lf]
    if finalize:
        o_fwd, gate, gain = finalize_args
        in_specs += [spec, spec, pl.BlockSpec((1, HG_DK), lambda h, i: (0, 0))]
        args += [o_fwd, gate, gain]
    return pl.pallas_call(
        functools.partial(_hgrn_kernel, reverse=reverse, finalize=finalize, groups=groups),
        out_shape=jax.ShapeDtypeStruct((T, HG_WIDTH), BF16 if finalize else F32),
        grid=(HG_HEADS // HG_HEADS_PER_STEP, T // tb),
        in_specs=in_specs,
        out_specs=spec,
        scratch_shapes=[pltpu.VMEM((HG_HEADS_PER_STEP, HG_DK, HG_DK), F32)],
        compiler_params=_cparams(("parallel", "arbitrary")),
        name="hgrn_bwd" if reverse else "hgrn_fwd",
    )(*args)


def _kvproj_kernel(m_ref, w_ref, o_ref):
    o_ref[...] = _dot(m_ref[...].astype(BF16), w_ref[...]).astype(BF16)


def _kvproj(mem, w):
    R, D = mem.shape
    n = w.shape[1]
    tr = 256
    return pl.pallas_call(
        _kvproj_kernel,
        out_shape=jax.ShapeDtypeStruct((R, n), BF16),
        grid=(R // tr,),
        in_specs=[pl.BlockSpec((tr, D), lambda i: (i, 0)), pl.BlockSpec((D, n), lambda i: (0, 0))],
        out_specs=pl.BlockSpec((tr, n), lambda i: (i, 0)),
        compiler_params=_cparams(("parallel",)),
        name="kvproj",
    )(mem, w)


def _mix_xattn_kernel(x0_ref, x1_ref, att_ref, rec_ref, g0_ref, b0_ref, wo_ref, g1_ref, b1_ref, wq_ref, kv_ref,
                      wxo_ref, g2_ref, b2_ref, x2_ref, p_ref, *, tiles_group0):
    D = x0_ref.shape[1]
    dh = D // X_HEADS
    x_tile = _group_rows(x0_ref, x1_ref, tiles_group0)

    def stages(rows):
        mix = _dot(att_ref[rows, :], wo_ref[:ATT_WIDTH, :]) + _dot(rec_ref[rows, :], wo_ref[ATT_WIDTH:, :])
        yield
        xn = _layer_norm(x_tile[rows], g0_ref[...], b0_ref[...])
        x1 = _layer_norm(DEEPNORM_ALPHA * xn + mix, g1_ref[...], b1_ref[...])
        qx = (_dot(x1.astype(BF16), wq_ref[...]) * (dh ** -0.5)).astype(BF16)
        yield
        scores = [_dot_nt(qx[:, h * dh:(h + 1) * dh], kv_ref[0, :, h * dh:(h + 1) * dh]) for h in range(X_HEADS)]
        yield
        heads = []
        for h, s in enumerate(scores):
            s = s - jnp.max(s, axis=-1, keepdims=True)
            p = jnp.exp(s)
            p = p / jnp.sum(p, axis=-1, keepdims=True)
            heads.append(_dot(p.astype(BF16), kv_ref[0, :, D + h * dh:D + (h + 1) * dh]).astype(BF16))
        yield
        xo = _dot(jnp.concatenate(heads, axis=-1), wxo_ref[...])
        yield
        x2 = _layer_norm(DEEPNORM_ALPHA * x1 + xo, g2_ref[...], b2_ref[...])
        x2_ref[rows, :] = x2
        _store_slabs(p_ref, _pack_bf16_pair(x2[:, :D // 2], x2[:, D // 2:]), row0=rows.start)

    _staggered(stages, x_tile.shape[0])


def _mix_xattn(x0, x1, att, rec, g0, b0, wo, g1, b1, wq, kv, wxo, g2, b2, seq_of_tile):
    D = x0.shape[1]
    T = x0.shape[0] + x1.shape[0]
    tm = TOKEN_TILE
    tiles_group0 = x0.shape[0] // tm
    n_mem = kv.shape[1]
    row = lambda i: (i, 0)
    const = lambda i: (0, 0)
    vec = pl.BlockSpec((1, D), const)
    return pl.pallas_call(
        functools.partial(_mix_xattn_kernel, tiles_group0=tiles_group0),
        out_shape=(jax.ShapeDtypeStruct((T, D), F32), jax.ShapeDtypeStruct((T * SLAB, LANES), U32)),
        grid=(T // tm,),
        in_specs=_group_specs(tm, D, tiles_group0) + [
            pl.BlockSpec((tm, ATT_WIDTH), row),
            pl.BlockSpec((tm, HG_WIDTH), row),
            vec, vec,
            pl.BlockSpec((ATT_WIDTH + HG_WIDTH, D), const),
            vec, vec,
            pl.BlockSpec((D, D), const),
            pl.BlockSpec((1, n_mem, 2 * D), lambda i: (seq_of_tile(i), 0, 0)),
            pl.BlockSpec((D, D), const),
            vec, vec,
        ],
        out_specs=(pl.BlockSpec((tm, D), row), pl.BlockSpec((tm * SLAB, LANES), row)),
        compiler_params=_cparams(("parallel",)),
        name="mix_xattn",
    )(x0, x1, att, rec, g0, b0, wo, g1, b1, wq, kv, wxo, g2, b2)


def _router_kernel(x_ref, wt_ref, b_ref, idx_ref, gate_ref, rank_ref, cnt_ref, run_ref):
    tm = x_ref.shape[0]

    @pl.when(pl.program_id(0) == 0)
    def _():
        run_ref[...] = jnp.zeros_like(run_ref)

    xh, xl, _ = _split3(x_ref[...])
    wh, wl, _ = _split3(wt_ref[...])
    logits = _dot_nt(wh, xh) + _dot_nt(wh, xl) + _dot_nt(wl, xh) + b_ref[...]
    e_id = lax.broadcasted_iota(I32, (N_EXPERTS, tm), 0)
    work = logits
    vals, idxs = [], []
    chosen = jnp.zeros((N_EXPERTS, tm), F32)
    for _ in range(TOP_K):
        m = jnp.max(work, axis=0, keepdims=True)
        idx = jnp.min(jnp.where(work == m, e_id, N_EXPERTS), axis=0, keepdims=True)
        hit = e_id == idx
        chosen = jnp.where(hit, 1.0, chosen)
        work = jnp.where(hit, NEG_INF, work)
        vals.append(m)
        idxs.append(idx)
    ex = [jnp.exp(v - vals[0]) for v in vals]
    den = ex[0] + ex[1] + ex[2] + ex[3]
    t_r = lax.broadcasted_iota(I32, (tm, tm), 0)
    t_c = lax.broadcasted_iota(I32, (tm, tm), 1)
    before = jnp.where(t_r < t_c, 1.0, 0.0).astype(BF16)
    rank_e = _dot(chosen.astype(BF16), before) + run_ref[...]
    run_ref[...] = run_ref[...] + jnp.sum(chosen, axis=1, keepdims=True)
    for k in range(TOP_K):
        idx_ref[k:k + 1, :] = idxs[k]
        gate_ref[k:k + 1, :] = ex[k] / den
        rank_ref[k:k + 1, :] = jnp.sum(jnp.where(e_id == idxs[k], rank_e, 0.0), axis=0, keepdims=True).astype(I32)
    cnt_ref[...] = run_ref[...].astype(I32)


def _router(x2, w_router_t, b_router):
    T, D = x2.shape
    tm = TOKEN_TILE
    tok = pl.BlockSpec((TOP_K, tm), lambda i: (0, i))
    return pl.pallas_call(
        _router_kernel,
        out_shape=(jax.ShapeDtypeStruct((TOP_K, T), I32), jax.ShapeDtypeStruct((TOP_K, T), F32),
                   jax.ShapeDtypeStruct((TOP_K, T), I32), jax.ShapeDtypeStruct((N_EXPERTS, 1), I32)),
        grid=(T // tm,),
        in_specs=[pl.BlockSpec((tm, D), lambda i: (i, 0)),
                  pl.BlockSpec((N_EXPERTS, D), lambda i: (0, 0)),
                  pl.BlockSpec((N_EXPERTS, 1), lambda i: (0, 0))],
        out_specs=(tok, tok, tok, pl.BlockSpec((N_EXPERTS, 1), lambda i: (0, 0))),
        scratch_shapes=[pltpu.VMEM((N_EXPERTS, 1), F32)],
        compiler_params=_cparams(("arbitrary",)),
        name="router",
    )(x2, w_router_t, b_router)


def _dispatch_kernel(fill_from_ref, fill_len_ref, dest_ref, p_ref, out_hbm, sem, fill_sem, *, n_fill):
    td = DISPATCH_TILE

    @pl.when(pl.program_id(0) == 0)
    def _():
        for r in range(fill_from_ref.shape[0]):
            first, n = fill_from_ref[r], fill_len_ref[r]
            n_chunks = lax.shift_right_logical(n, FILL_CHUNK_LOG2)

            def fill_chunk(j, c):
                rows = pl.ds(first + j * FILL_CHUNK, FILL_CHUNK)
                pltpu.make_async_copy(p_ref.at[pl.ds(0, FILL_CHUNK)], out_hbm.at[rows], fill_sem).start()
                return c

            def fill_row(j, c):
                pltpu.make_async_copy(p_ref.at[0], out_hbm.at[first + n_chunks * FILL_CHUNK + j], fill_sem).start()
                return c

            lax.fori_loop(0, n_chunks, fill_chunk, 0)
            lax.fori_loop(0, n - n_chunks * FILL_CHUNK, fill_row, 0)
        filled = out_hbm.at[pl.ds(0, n_fill)]
        pltpu.make_async_copy(filled, filled, fill_sem).wait()

    def start(t, c):
        for k in range(TOP_K):
            d = dest_ref[0, 0, k * td + t]
            pltpu.make_async_copy(p_ref.at[t], out_hbm.at[d], sem).start(priority=k % DMA_PRIORITIES)
        return c

    lax.fori_loop(0, td, start, 0, unroll=DMA_ISSUE_UNROLL)
    everything = out_hbm.at[pl.ds(0, TOP_K * td)]
    pltpu.make_async_copy(everything, everything, sem).wait()


def _dispatch(fill_from, fill_len, dest_tiles, p, n_rows):
    T = p.shape[0]
    td = DISPATCH_TILE
    grid_spec = pltpu.PrefetchScalarGridSpec(
        num_scalar_prefetch=2,
        grid=(T // td,),
        in_specs=[pl.BlockSpec((1, 1, TOP_K * td), lambda i, ff, fl: (i, 0, 0), memory_space=pltpu.SMEM),
                  pl.BlockSpec((td,) + p.shape[1:], lambda i, ff, fl: (i, 0, 0))],
        out_specs=pl.BlockSpec(memory_space=pl.ANY),
        scratch_shapes=[pltpu.SemaphoreType.DMA, pltpu.SemaphoreType.DMA],
    )
    return pl.pallas_call(
        functools.partial(_dispatch_kernel, n_fill=n_rows - T * TOP_K),
        out_shape=jax.ShapeDtypeStruct((n_rows,) + p.shape[1:], p.dtype),
        grid_spec=grid_spec,
        compiler_params=_cparams(("arbitrary",)),
        name="dispatch",
    )(fill_from, fill_len, dest_tiles, p)


def _expert_kernel(blk_e_ref, nvalid_ref, p_ref, wu_ref, bu_ref, wd_ref, bd_ref, y_ref, wu_b, wd_b):
    i = pl.program_id(0)
    dff = wd_ref.shape[1]
    half = SLAB * LANES
    live = i < nvalid_ref[0]
    new_expert = (i == 0) | (blk_e_ref[i] != blk_e_ref[jnp.maximum(i - 1, 0)])

    @pl.when(live & new_expert)
    def _():
        wu_b[...] = wu_ref[0].astype(BF16)
        wd_b[...] = wd_ref[0].astype(BF16)

    @pl.when(live)
    def _():
        lo, hi = _unpack_bf16_pair(_load_slabs(p_ref))
        lo = lo.astype(BF16)
        hi = hi.astype(BF16)

        def up(c0):
            return (_dot(lo, wu_b[:half, c0:c0 + dff]) + _dot(hi, wu_b[half:, c0:c0 + dff])
                    + bu_ref[0, :, c0:c0 + dff])

        glu = jnp.minimum(up(0), SWIGLU_LIMIT)
        lin = jnp.clip(up(dff), -SWIGLU_LIMIT, SWIGLU_LIMIT)
        a = glu * _sigmoid(SWIGLU_ALPHA * glu) * (lin + 1.0)
        y = _dot(a.astype(BF16), wd_b[...]) + bd_ref[0]
        _store_slabs(y_ref, _pack_bf16_pair(y[:, :half], y[:, half:]))

    @pl.when(i >= nvalid_ref[0])
    def _():
        y_ref[...] = jnp.zeros_like(y_ref)


def _experts(blk_expert, n_valid, ps, wu, bu, wd, bd):
    R = ps.shape[0] // SLAB
    te = EXPERT_TILE
    E, D, two_f = wu.shape
    dff = wd.shape[1]
    grid_spec = pltpu.PrefetchScalarGridSpec(
        num_scalar_prefetch=2,
        grid=(R // te,),
        in_specs=[
            pl.BlockSpec((te * SLAB, LANES), lambda i, be, nv: (i, 0)),
            pl.BlockSpec((1, D, two_f), lambda i, be, nv: (be[i], 0, 0)),
            pl.BlockSpec((1, 1, two_f), lambda i, be, nv: (be[i], 0, 0)),
            pl.BlockSpec((1, dff, D), lambda i, be, nv: (be[i], 0, 0)),
            pl.BlockSpec((1, 1, D), lambda i, be, nv: (be[i], 0, 0)),
        ],
        out_specs=pl.BlockSpec((te * SLAB, LANES), lambda i, be, nv: (i, 0)),
        scratch_shapes=[pltpu.VMEM((D, two_f), BF16), pltpu.VMEM((dff, D), BF16)],
    )
    return pl.pallas_call(
        _expert_kernel,
        out_shape=jax.ShapeDtypeStruct((R * SLAB, LANES), U32),
        grid_spec=grid_spec,
        compiler_params=_cparams(("arbitrary",)),
        name="experts",
    )(blk_expert, n_valid, ps, wu, bu, wd, bd)


def _combine_kernel(dest_ref, dest_next_ref, ys_hbm, x2_ref, gate_ref, g_ref, b_ref, o_ref, buf, sems):
    tc = x2_ref.shape[0]
    half = SLAB * LANES
    i = pl.program_id(0)
    slot = i % 2

    def start_tile(idx_ref, s):
        def body(t, c):
            rows = pl.ds(pl.multiple_of(t * SLAB, SLAB), SLAB)
            for k in range(TOP_K):
                d = idx_ref[0, 0, k * tc + t]
                pltpu.make_async_copy(ys_hbm.at[d], buf.at[s, k, rows], sems.at[s]).start(
                    priority=k % DMA_PRIORITIES)
            return c
        lax.fori_loop(0, tc, body, 0, unroll=DMA_ISSUE_UNROLL)

    @pl.when(i == 0)
    def _():
        start_tile(dest_ref, 0)

    for s in range(2):
        @pl.when((i + 1 < pl.num_programs(0)) & (slot == s))
        def _():
            start_tile(dest_next_ref, 1 - s)

    for s in range(2):
        @pl.when(slot == s)
        def _():
            pltpu.make_async_copy(buf.at[s], buf.at[s], sems.at[s]).wait()

    gates = gate_ref[...]
    acc_lo = jnp.zeros((tc, half), F32)
    acc_hi = jnp.zeros((tc, half), F32)
    for k in range(TOP_K):
        lo, hi = _unpack_bf16_pair(_load_slabs(buf.at[slot, k]))
        gk = gates[:, k:k + 1]
        acc_lo = acc_lo + gk * lo
        acc_hi = acc_hi + gk * hi
    ffn = jnp.concatenate([acc_lo, acc_hi], axis=-1)
    o_ref[...] = _layer_norm(DEEPNORM_ALPHA * x2_ref[...] + ffn, g_ref[...], b_ref[...])


def _combine(dest_tiles, ys, x2, gates, g, b, *, tile0, n_tiles):
    D = x2.shape[1]
    tc = COMBINE_TILE
    return pl.pallas_call(
        _combine_kernel,
        out_shape=jax.ShapeDtypeStruct((n_tiles * tc, D), F32),
        grid=(n_tiles,),
        in_specs=[pl.BlockSpec((1, 1, TOP_K * tc), lambda i: (tile0 + i, 0, 0), memory_space=pltpu.SMEM),
                  pl.BlockSpec((1, 1, TOP_K * tc), lambda i: (tile0 + jnp.minimum(i + 1, n_tiles - 1), 0, 0),
                               memory_space=pltpu.SMEM),
                  pl.BlockSpec(memory_space=pl.ANY),
                  pl.BlockSpec((tc, D), lambda i: (tile0 + i, 0)),
                  pl.BlockSpec((tc, TOP_K), lambda i: (tile0 + i, 0)),
                  pl.BlockSpec((1, D), lambda i: (0, 0)),
                  pl.BlockSpec((1, D), lambda i: (0, 0))],
        out_specs=pl.BlockSpec((tc, D), lambda i: (i, 0)),
        scratch_shapes=[pltpu.VMEM((2, TOP_K, tc * SLAB, LANES), U32), pltpu.SemaphoreType.DMA((2,))],
        compiler_params=_cparams(("arbitrary",)),
        name="combine",
    )(dest_tiles, dest_tiles, ys, x2, gates, g, b)


def _rope_tables(seq_len):
    rows = seq_len // GRID_W
    row = jnp.repeat(jnp.arange(rows, dtype=F32), GRID_W)
    colp = (jnp.arange(seq_len) % GRID_W).astype(F32)
    axis_dim = ATT_HEAD_DIM // 2
    inv_freq = ROPE_THETA ** (-jnp.arange(0, axis_dim, 2, dtype=F32) / axis_dim)
    ang = jnp.concatenate([row[:, None] * inv_freq, colp[:, None] * inv_freq], axis=-1)
    cos, sin = jnp.cos(ang), jnp.sin(ang)
    cos_h = jnp.concatenate([cos, cos], axis=-1)
    sin_h = jnp.concatenate([-sin, sin], axis=-1)
    return jnp.tile(cos_h, (1, LANES // ATT_HEAD_DIM)), jnp.tile(sin_h, (1, LANES // ATT_HEAD_DIM))


def _tiles_of(idx_t, tile):
    K, T = idx_t.shape
    return idx_t.reshape(K, T // tile, tile).transpose(1, 0, 2).reshape(T // tile, 1, K * tile)


def kernel(x_prompt, x_sample, mem_prompt, mem_sample, ln_in_g, ln_in_b, w_in, q_norm_g, k_norm_g, hg_lb_fwd, hg_lb_bwd, hg_norm_g, w_out, ln1_g, ln1_b, w_xq, w_xkv, w_xo, ln2_g, ln2_b, w_router, b_router, w_up, b_up, w_down, b_down, ln3_g, ln3_b):
    Bp, Sp, D = x_prompt.shape
    Bs, Ss, _ = x_sample.shape
    Tp, Ts = Bp * Sp, Bs * Ss
    T = Tp + Ts
    tm = TOKEN_TILE
    assert Sp % tm == 0 and Ss % Sp == 0
    n_mem = mem_prompt.shape[1]
    layer = 0

    x0, x1 = x_prompt.reshape(Tp, D), x_sample.reshape(Ts, D)
    mem = jnp.concatenate([mem_prompt.reshape(Bp * n_mem, D), mem_sample.reshape(Bs * n_mem, D)], axis=0)

    def deinterleave(a):
        lead = a.shape[:-1]
        a = a.reshape(*lead, -1, ATT_HEAD_DIM // 2, 2)
        return jnp.swapaxes(a, -1, -2).reshape(*lead, -1)

    n_qk = ATT_WIDTH + ATT_KV_WIDTH
    w_in_l = w_in[layer]
    w_in_b = jnp.concatenate([deinterleave(w_in_l[:, :n_qk]), w_in_l[:, n_qk:]], axis=1).astype(BF16)
    pair = LANES // ATT_HEAD_DIM
    qg = jnp.tile(deinterleave(q_norm_g[layer]), pair).reshape(1, LANES)
    kg = jnp.tile(deinterleave(k_norm_g[layer]), pair).reshape(1, LANES)

    def lower_bound(logits):
        sm = jax.nn.softmax(logits.astype(F32), axis=0)
        return (jnp.cumsum(sm, axis=0)[layer + 1] - sm[0]).reshape(1, HG_WIDTH)

    lbf, lbb = lower_bound(hg_lb_fwd), lower_bound(hg_lb_bwd)
    cos_t, sin_t = _rope_tables(max(Sp, Ss))
    vec = lambda a: a.reshape(1, -1)

    groups = _Groups(Bp, Sp, Bs, Ss)
    pos_block = lambda i: groups.pos_of_tile(i, tm)
    seq_of_tile = lambda i: groups.seq_of_tile(i, tm)

    q, k, v, hq, hi, hg, lff, lfb = _inproj(x0, x1, vec(ln_in_g), vec(ln_in_b), w_in_b, cos_t, sin_t, qg, kg,
                                           lbf, lbb, pos_block)

    score_bound = (jnp.max(jnp.abs(q_norm_g[layer])) * jnp.max(jnp.abs(k_norm_g[layer]))
                   * (ATT_HEAD_DIM ** 0.5 * ATT_NORM_SLACK)).astype(F32).reshape(1)
    att = _attention(score_bound, q, k, v, groups)
    o_fwd = _hgrn_pass(hq, hi, lff, groups, reverse=False)
    gain = hg_norm_g[layer].reshape(1, HG_DK)
    rec = _hgrn_pass(hq, hi, lfb, groups, reverse=True, finalize_args=(o_fwd, hg, gain))

    kv = _kvproj(mem, w_xkv[layer].astype(BF16)).reshape(Bp + Bs, n_mem, 2 * D)
    x2, packed = _mix_xattn(x0, x1, att, rec, vec(ln_in_g), vec(ln_in_b), w_out[layer].astype(BF16),
                            vec(ln1_g[layer]), vec(ln1_b[layer]), w_xq[layer].astype(BF16), kv,
                            w_xo[layer].astype(BF16), vec(ln2_g[layer]), vec(ln2_b[layer]), seq_of_tile)

    idx_t, gate_t, rank_t, counts = _router(x2, w_router[layer].T, b_router[layer].reshape(N_EXPERTS, 1))

    te = EXPERT_TILE
    counts = counts.reshape(N_EXPERTS)
    padded = (counts + te - 1) // te * te
    pad_end = jnp.cumsum(padded)
    pad_start = pad_end - padded
    n_blk = (T * TOP_K + N_EXPERTS * (te - 1) + te - 1) // te
    e_ids = jnp.arange(N_EXPERTS, dtype=I32)
    start_of = jnp.sum(jnp.where(idx_t[None] == e_ids[:, None, None], pad_start[:, None, None], 0), axis=0)
    dest_t = start_of + rank_t
    blk_row0 = jnp.arange(n_blk, dtype=I32) * te
    blk_expert = jnp.minimum(jnp.sum((pad_end[None, :] <= blk_row0[:, None]).astype(I32), axis=1), N_EXPERTS - 1)
    n_valid = (pad_end[-1] // te).astype(I32).reshape(1)

    n_rows = n_blk * te
    fill_from = jnp.concatenate([pad_start + counts, pad_end[-1:]]).astype(I32)
    fill_len = jnp.concatenate([padded - counts, n_rows - pad_end[-1:]]).astype(I32)
    sorted_rows = _dispatch(fill_from, fill_len, _tiles_of(dest_t, DISPATCH_TILE),
                            packed.reshape(T, SLAB, LANES), n_rows)
    ys = _experts(blk_expert, n_valid, sorted_rows.reshape(n_rows * SLAB, LANES), w_up[layer],
                  b_up[layer].reshape(N_EXPERTS, 1, -1), w_down[layer], b_down[layer].reshape(N_EXPERTS, 1, -1))
    ys = ys.reshape(n_rows, SLAB, LANES)

    dest_c = _tiles_of(dest_t, COMBINE_TILE)
    gates = gate_t.T
    tc = COMBINE_TILE
    y_p = _combine(dest_c, ys, x2, gates, vec(ln3_g[layer]), vec(ln3_b[layer]), tile0=0, n_tiles=Tp // tc)
    y_s = _combine(dest_c, ys, x2, gates, vec(ln3_g[layer]), vec(ln3_b[layer]), tile0=Tp // tc, n_tiles=Ts // tc)
    return y_p.reshape(Bp, Sp, D), y_s.reshape(Bs, Ss, D)
```

```python
import functools

import jax
import jax.numpy as jnp
import numpy as np
from jax import lax
from jax.experimental import pallas as pl
from jax.experimental.pallas import tpu as pltpu

F32 = jnp.float32
BF16 = jnp.bfloat16
I32 = jnp.int32
U32 = jnp.uint32

GRID_W = 64
ATT_HEADS = 8
ATT_KV_HEADS = 2
ATT_HEAD_DIM = 64
ATT_WIDTH = ATT_HEADS * ATT_HEAD_DIM
ATT_KV_WIDTH = ATT_KV_HEADS * ATT_HEAD_DIM
ROPE_THETA = 10000.0
HG_HEADS = 4
HG_DK = 128
HG_WIDTH = HG_HEADS * HG_DK
X_HEADS = 4
N_EXPERTS = 32
TOP_K = 4
SWIGLU_LIMIT = 7.0
SWIGLU_ALPHA = 1.702
LN_EPS = 1e-5
RMS_EPS = 1e-6
DEPTH = 1
DEEPNORM_ALPHA = (2.0 * DEPTH) ** 0.25

LANES = 128
VMEM_LIMIT_BYTES = 56 * 1024 * 1024
DMA_PRIORITIES = 2
DMA_ISSUE_UNROLL = 4

TOKEN_TILE = 512
ATT_Q_TILE = 256
ATT_K_TILE = 4096
HG_CHUNK = 64
HG_SUB = 16
HG_HEADS_PER_STEP = 4

ATT_SAFE_SCORE = 60.0
ATT_NORM_SLACK = 1.02
HG_SAFE_LOG = 5.0
EXPERT_TILE = 512
COMBINE_TILE = 512
DISPATCH_TILE = 1024
FILL_CHUNK_LOG2 = 5
FILL_CHUNK = 1 << FILL_CHUNK_LOG2
ROW_PARTS = 2
NEG_INF = float("-inf")


def _cparams(sem):
    return pltpu.CompilerParams(dimension_semantics=sem, vmem_limit_bytes=VMEM_LIMIT_BYTES)


def _layer_norm(x, g, b):
    mu = jnp.mean(x, axis=-1, keepdims=True)
    xc = x - mu
    var = jnp.mean(xc * xc, axis=-1, keepdims=True)
    return xc * lax.rsqrt(var + LN_EPS) * g + b


def _sigmoid(x):
    return 1.0 / (1.0 + jnp.exp(-x))


def _dot(a, b):
    return jnp.dot(a, b, preferred_element_type=F32)


def _dot_nt(a, b):
    return lax.dot_general(a, b, (((1,), (1,)), ((), ())), preferred_element_type=F32)


def _dot_tn(a, b):
    return lax.dot_general(a, b, (((0,), (0,)), ((), ())), preferred_element_type=F32)


def _split3(x):
    h1 = x.astype(BF16)
    r1 = x - h1.astype(F32)
    h2 = r1.astype(BF16)
    h3 = (r1 - h2.astype(F32)).astype(BF16)
    return h1, h2, h3


def _pack_bf16_pair(lo, hi):
    lo_b = pltpu.bitcast(lo.astype(BF16).astype(F32), U32)
    hi_b = pltpu.bitcast(hi.astype(BF16).astype(F32), U32)
    return (hi_b & jnp.uint32(0xFFFF0000)) | (lo_b >> 16)


def _unpack_bf16_pair(p):
    lo = pltpu.bitcast(p << 16, F32)
    hi = pltpu.bitcast(p & jnp.uint32(0xFFFF0000), F32)
    return lo, hi


SLAB = 4


def _store_slabs(ref, value, row0=0):
    n = value.shape[0]
    for s in range(SLAB):
        ref[pl.ds(row0 * SLAB + s, n, stride=SLAB), :] = value[:, s * LANES:(s + 1) * LANES]


def _load_slabs(ref, row0=0, n=None):
    n = ref.shape[0] // SLAB if n is None else n
    return jnp.concatenate([ref[pl.ds(row0 * SLAB + s, n, stride=SLAB), :] for s in range(SLAB)], axis=-1)


def _row_parts(n_rows):
    part = n_rows // ROW_PARTS
    return [slice(r * part, (r + 1) * part) for r in range(ROW_PARTS)]


def _staggered(stage_fn, n_rows):
    pending = [stage_fn(rows) for rows in _row_parts(n_rows)]
    while pending:
        for g in list(pending):
            try:
                next(g)
            except StopIteration:
                pending.remove(g)


def _group_rows(x0_ref, x1_ref, tiles_group0):
    return jnp.where(pl.program_id(0) < tiles_group0, x0_ref[...], x1_ref[...])


def _group_specs(tile, width, tiles_group0):
    return [pl.BlockSpec((tile, width), lambda i: (jnp.minimum(i, tiles_group0 - 1), 0)),
            pl.BlockSpec((tile, width), lambda i: (jnp.maximum(i - tiles_group0, 0), 0))]


def _inproj_kernel(x0_ref, x1_ref, g_ref, b_ref, w_ref, cos_ref, sin_ref, qg_ref, kg_ref, lbf_ref, lbb_ref,
                   q_out, k_out, v_out, hq_out, hi_out, hg_out, lff_out, lfb_out, *, tiles_group0):
    xn = _layer_norm(_group_rows(x0_ref, x1_ref, tiles_group0), g_ref[...], b_ref[...])
    xb = xn.astype(BF16)
    cos = cos_ref[...]
    sin = sin_ref[...]
    tm = xb.shape[0]

    lane = lax.broadcasted_iota(I32, (tm, LANES), 1)
    first_half = (lane & (ATT_HEAD_DIM - 1)) < (ATT_HEAD_DIM // 2)
    r = lax.broadcasted_iota(I32, (LANES, LANES), 0)
    c = lax.broadcasted_iota(I32, (LANES, LANES), 1)
    head_ones = jnp.where((r >> 6) == (c >> 6), 1.0, 0.0).astype(BF16)

    def normed_rope(u, gain, scale):
        sq = u * u
        s_hi = sq.astype(BF16)
        s_lo = (sq - s_hi.astype(F32)).astype(BF16)
        ms = (_dot(s_hi, head_ones) + _dot(s_lo, head_ones)) * (1.0 / ATT_HEAD_DIM)
        un = u * lax.rsqrt(ms + RMS_EPS) * gain
        rot = jnp.where(first_half, pltpu.roll(un, LANES - 32, 1), pltpu.roll(un, 32, 1))
        return (un * cos + rot * sin) * scale

    n_qkv = ATT_WIDTH + 2 * ATT_KV_WIDTH
    u_att = [_dot(xb, w_ref[:, c:c + 2 * LANES]) for c in range(0, n_qkv, 2 * LANES)]
    col = n_qkv

    def hg_proj():
        nonlocal col
        u = _dot(xb, w_ref[:, col:col + HG_WIDTH])
        col += HG_WIDTH
        return u

    u = hg_proj()
    hq_out[...] = (u * _sigmoid(u)).astype(BF16)
    for j in range(ATT_WIDTH // (2 * LANES)):
        for h in range(2):
            c0 = (2 * j + h) * LANES
            q_out[:, c0:c0 + LANES] = normed_rope(u_att[j][:, h * LANES:(h + 1) * LANES], qg_ref[...],
                                                  ATT_HEAD_DIM ** -0.5).astype(BF16)
        u = hg_proj()
        lb = (lbf_ref, lbb_ref)[j][...]
        (lff_out, lfb_out)[j][...] = jnp.log(lb + (1.0 - lb) * _sigmoid(u))
    k_out[...] = normed_rope(u_att[-1][:, :LANES], kg_ref[...], 1.0).astype(BF16)
    v_out[:, :LANES] = u_att[-1][:, LANES:].astype(BF16)
    v_out[:, LANES:] = jnp.ones((tm, LANES), BF16)
    hi_out[...] = hg_proj().astype(BF16)
    u = hg_proj()
    hg_out[...] = (u * _sigmoid(u)).astype(BF16)


def _inproj(x0, x1, g, b, w, cos_t, sin_t, qg, kg, lbf, lbb, pos_block):
    D = x0.shape[1]
    T = x0.shape[0] + x1.shape[0]
    tm = TOKEN_TILE
    tiles_group0 = x0.shape[0] // tm
    n_in = w.shape[1]
    row = lambda i: (i, 0)
    const = lambda i: (0, 0)
    out_shape = (
        jax.ShapeDtypeStruct((T, ATT_WIDTH), BF16),
        jax.ShapeDtypeStruct((T, ATT_KV_WIDTH), BF16),
        jax.ShapeDtypeStruct((T, 2 * ATT_KV_WIDTH), BF16),
        jax.ShapeDtypeStruct((T, HG_WIDTH), BF16),
        jax.ShapeDtypeStruct((T, HG_WIDTH), BF16),
        jax.ShapeDtypeStruct((T, HG_WIDTH), BF16),
        jax.ShapeDtypeStruct((T, HG_WIDTH), F32),
        jax.ShapeDtypeStruct((T, HG_WIDTH), F32),
    )
    out_specs = (
        pl.BlockSpec((tm, ATT_WIDTH), row),
        pl.BlockSpec((tm, ATT_KV_WIDTH), row),
        pl.BlockSpec((tm, 2 * ATT_KV_WIDTH), row),
        pl.BlockSpec((tm, HG_WIDTH), row),
        pl.BlockSpec((tm, HG_WIDTH), row),
        pl.BlockSpec((tm, HG_WIDTH), row),
        pl.BlockSpec((tm, HG_WIDTH), row),
        pl.BlockSpec((tm, HG_WIDTH), row),
    )
    return pl.pallas_call(
        functools.partial(_inproj_kernel, tiles_group0=tiles_group0),
        out_shape=out_shape,
        grid=(T // tm,),
        in_specs=_group_specs(tm, D, tiles_group0) + [
            pl.BlockSpec((1, D), const),
            pl.BlockSpec((1, D), const),
            pl.BlockSpec((D, n_in), const),
            pl.BlockSpec((tm, LANES), lambda i: (pos_block(i), 0)),
            pl.BlockSpec((tm, LANES), lambda i: (pos_block(i), 0)),
            pl.BlockSpec((1, LANES), const),
            pl.BlockSpec((1, LANES), const),
            pl.BlockSpec((1, HG_WIDTH), const),
            pl.BlockSpec((1, HG_WIDTH), const),
        ],
        out_specs=out_specs,
        compiler_params=_cparams(("parallel",)),
        name="inproj",
    )(x0, x1, g, b, w, cos_t, sin_t, qg, kg, lbf, lbb)


class _Groups:
    def __init__(self, n0, s0, n1, s1):
        self.n0, self.s0, self.n1, self.s1 = n0, s0, n1, s1
        self.t0 = n0 * s0
        self.total = self.t0 + n1 * s1

    def split(self, i, tile):
        nt0 = self.t0 // tile
        per0, per1 = self.s0 // tile, self.s1 // tile
        in1 = i >= nt0
        j1 = jnp.maximum(i - nt0, 0)
        seq = jnp.where(in1, self.n0 + j1 // per1, i // per0)
        pos = jnp.where(in1, j1 % per1, i % per0)
        return in1, seq, pos, jnp.where(in1, per1, per0)

    def seq_of_tile(self, i, tile):
        return self.split(i, tile)[1]

    def pos_of_tile(self, i, tile):
        return self.split(i, tile)[2]

    def reversed_tile(self, i, tile):
        _, _, pos, per = self.split(i, tile)
        return i - pos + (per - 1 - pos)


def _attention_kernel(*refs, n_kv_blocks, groups):
    bound_ref, q_ref = refs[:2]
    k_refs = refs[2:2 + n_kv_blocks]
    v_refs = refs[2 + n_kv_blocks:2 + 2 * n_kv_blocks]
    o_ref, out_ref = refs[2 + 2 * n_kv_blocks:]
    tq = q_ref.shape[0]
    kv_rows = k_refs[0].shape[0]
    tk = min(ATT_K_TILE, kv_rows)
    n_k = kv_rows // tk
    group = ATT_HEADS // ATT_KV_HEADS
    rows = group * tq
    lane = lax.broadcasted_iota(I32, (tq, LANES), 1)
    low_half = lane < ATT_HEAD_DIM
    in_group1 = groups.split(pl.program_id(0), tq)[0]
    extra_trips = jnp.where(in_group1, n_k, 0)
    unshifted_ok = bound_ref[0] <= ATT_SAFE_SCORE

    for kvh in range(ATT_KV_HEADS):
        keep = low_half if kvh == 0 else jnp.logical_not(low_half)
        parts = []
        for g in range(group):
            c0 = kvh * 2 * LANES + (g // 2) * LANES
            pair = q_ref[:, c0:c0 + LANES].astype(F32)
            if g % 2 != kvh:
                pair = pltpu.roll(pair, ATT_HEAD_DIM, 1)
            parts.append(jnp.where(keep, pair, 0.0).astype(BF16))
        qp = jnp.concatenate(parts, axis=0)

        def run(make_body, carry):
            carry = lax.fori_loop(0, n_k, make_body(k_refs[0], v_refs[0]), carry)
            for kb in range(1, n_kv_blocks):
                carry = lax.fori_loop(0, extra_trips, make_body(k_refs[kb], v_refs[kb]), carry)
            return carry

        @pl.when(unshifted_ok)
        def _():
            def make_body(k_ref, v_ref):
                def body(kc, acc):
                    r0 = pl.multiple_of(kc * tk, tk)
                    p = jnp.exp(_dot_nt(qp, k_ref[pl.ds(r0, tk), :]))
                    return acc + _dot(p.astype(BF16), v_ref[pl.ds(r0, tk), :])
                return body

            acc = run(make_body, jnp.zeros((rows, 2 * LANES), F32))
            out_ref[...] = acc[:, :LANES] / acc[:, LANES:]

        @pl.when(jnp.logical_not(unshifted_ok))
        def _():
            def make_body(k_ref, v_ref):
                def body(kc, carry):
                    m, l, acc = carry
                    r0 = pl.multiple_of(kc * tk, tk)
                    s = _dot_nt(qp, k_ref[pl.ds(r0, tk), :])
                    m_new = jnp.maximum(m, jnp.max(s, axis=-1, keepdims=True))
                    alpha = jnp.exp(m - m_new)
                    p = jnp.exp(s - m_new)
                    l_new = alpha * l + jnp.sum(p, axis=-1, keepdims=True)
                    acc_new = alpha * acc + _dot(p.astype(BF16), v_ref[pl.ds(r0, tk), :LANES])
                    return m_new, l_new, acc_new
                return body

            init = (jnp.full((rows, 1), NEG_INF, F32), jnp.zeros((rows, 1), F32), jnp.zeros((rows, LANES), F32))
            m, l, acc = run(make_body, init)
            out_ref[...] = acc / l

        out = out_ref[...]
        for j in range(group // 2):
            a = out[(2 * j) * tq:(2 * j + 1) * tq]
            b = out[(2 * j + 1) * tq:(2 * j + 2) * tq]
            if kvh == 1:
                a = pltpu.roll(a, ATT_HEAD_DIM, 1)
            else:
                b = pltpu.roll(b, ATT_HEAD_DIM, 1)
            c0 = kvh * 2 * LANES + j * LANES
            o_ref[:, c0:c0 + LANES] = jnp.where(low_half, a, b).astype(BF16)


def _attention(score_bound, q, k, v, groups):
    T = q.shape[0]
    tq = ATT_Q_TILE
    kv_rows = groups.s0
    n_kv_blocks = groups.s1 // groups.s0

    def kv_map(j):
        def index_map(i):
            in1, seq, _, _ = groups.split(i, tq)
            first = jnp.where(in1, groups.t0 // kv_rows + (seq - groups.n0) * n_kv_blocks, seq)
            return (first + jnp.where(in1, j, 0), 0)
        return index_map

    k_specs = [pl.BlockSpec((kv_rows, ATT_KV_WIDTH), kv_map(j)) for j in range(n_kv_blocks)]
    v_specs = [pl.BlockSpec((kv_rows, 2 * ATT_KV_WIDTH), kv_map(j)) for j in range(n_kv_blocks)]
    return pl.pallas_call(
        functools.partial(_attention_kernel, n_kv_blocks=n_kv_blocks, groups=groups),
        out_shape=jax.ShapeDtypeStruct((T, ATT_WIDTH), BF16),
        grid=(T // tq,),
        in_specs=[pl.BlockSpec(memory_space=pltpu.SMEM),
                  pl.BlockSpec((tq, ATT_WIDTH), lambda i: (i, 0))] + k_specs + v_specs,
        out_specs=pl.BlockSpec((tq, ATT_WIDTH), lambda i: (i, 0)),
        scratch_shapes=[pltpu.VMEM((ATT_HEADS // ATT_KV_HEADS * tq, LANES), F32)],
        compiler_params=_cparams(("parallel",)),
        name="attention",
    )(score_bound, q, *([k] * n_kv_blocks), *([v] * n_kv_blocks))


def _hgrn_kernel(*refs, reverse, finalize, groups):
    if finalize:
        q_ref, v_ref, lf_ref, of_ref, g_ref, gain_ref, o_ref, st_ref = refs
    else:
        q_ref, v_ref, lf_ref, o_ref, st_ref = refs
    tb = q_ref.shape[0]
    n_heads = q_ref.shape[1] // HG_DK
    C = min(HG_CHUNK, tb)
    n_chunk = tb // C
    n_sub = C // HG_SUB

    @pl.when(groups.pos_of_tile(pl.program_id(1), tb) == 0)
    def _():
        st_ref[...] = jnp.zeros_like(st_ref)

    ri = lax.broadcasted_iota(I32, (C, C), 0)
    ci = lax.broadcasted_iota(I32, (C, C), 1)
    sub_lo = (ri >> 4) << 4
    if reverse:
        loc_m = (ci >= ri) & (ci < sub_lo + HG_SUB)
    else:
        loc_m = (ci <= ri) & (ci >= sub_lo)
    loc_mat = jnp.where(loc_m, 1.0, 0.0).astype(BF16)
    row_id = lax.broadcasted_iota(I32, (C, 1), 0)
    sub_row = lax.broadcasted_iota(I32, (HG_SUB, 1), 0)
    lane_c = lax.broadcasted_iota(I32, (HG_SUB, C), 1)
    edge = 0 if reverse else C - 1

    def cumulative(lf):
        h1, h2, h3 = _split3(lf)
        loc = _dot(loc_mat, h1) + _dot(loc_mat, h2) + _dot(loc_mat, h3)
        subs = [loc[s * HG_SUB:(s + 1) * HG_SUB] for s in range(n_sub)]
        order = range(n_sub - 1, -1, -1) if reverse else range(n_sub)
        carry = jnp.zeros((1, HG_DK), F32)
        for s in order:
            sub_total = subs[s][0:1, :] if reverse else subs[s][HG_SUB - 1:HG_SUB, :]
            subs[s] = subs[s] + carry
            carry = carry + sub_total
        return jnp.concatenate(subs, axis=0), loc

    def head_cols(head):
        return slice(head * HG_DK, (head + 1) * HG_DK)

    def emit(r0, head, o):
        rows, cols = pl.ds(r0, C), head_cols(head)
        if finalize:
            tot = o + of_ref[rows, cols]
            ms = jnp.mean(tot * tot, axis=-1, keepdims=True)
            res = tot * lax.rsqrt(ms + RMS_EPS) * gain_ref[...] * g_ref[rows, cols].astype(F32)
            o_ref[rows, cols] = res.astype(o_ref.dtype)
        else:
            o_ref[rows, cols] = o

    def pairwise_chunk(step, carry, *, head):
        cidx = (n_chunk - 1 - step) if reverse else step
        r0 = pl.multiple_of(cidx * C, C)
        rows, cols = pl.ds(r0, C), head_cols(head)
        lf = lf_ref[rows, cols]
        q = q_ref[rows, cols].astype(F32)
        v = v_ref[rows, cols]
        cum, loc = cumulative(lf)
        kk = 1.0 - jnp.exp(lf)
        total = cum[edge:edge + 1, :]
        st = st_ref[head]
        o_inter = _dot_nt((q * jnp.exp(cum)).astype(BF16), st.astype(BF16))
        k_state = (kk * jnp.exp(total - cum)).astype(BF16)
        st_ref[head] = st * jnp.exp(total) + _dot_tn(v, k_state)
        qh_b = (q * jnp.exp(loc)).astype(BF16)
        blocks = []
        for s in range(n_sub):
            lo = s * HG_SUB
            hi = lo + HG_SUB
            has_off = (s < n_sub - 1) if reverse else (s > 0)
            if has_off:
                if reverse:
                    expo = jnp.where(row_id >= hi, cum[hi:hi + 1, :] - cum, NEG_INF)
                else:
                    expo = jnp.where(row_id < lo, cum[lo - 1:lo, :] - cum, NEG_INF)
                k_off = (kk * jnp.exp(expo)).astype(BF16)
                a_s = _dot_nt(qh_b[lo:hi], k_off)
            else:
                a_s = jnp.zeros((HG_SUB, C), F32)
            loc_s = loc[lo:hi]
            q_s = q[lo:hi]
            for j in range(HG_SUB):
                jj = lo + j
                d = loc_s - loc[jj:jj + 1, :]
                msk = (sub_row <= j) if reverse else (sub_row >= j)
                e = jnp.exp(jnp.where(msk, d, NEG_INF))
                colv = jnp.sum(q_s * e * kk[jj:jj + 1, :], axis=-1, keepdims=True)
                a_s = jnp.where(lane_c == jj, colv, a_s)
            blocks.append(a_s)
        a = jnp.concatenate(blocks, axis=0)
        emit(r0, head, o_inter + _dot(a.astype(BF16), v))
        return carry

    def staged_tile():
        items = [(((n_chunk - 1 - s) if reverse else s) * C, head) for s in range(n_chunk) for head in range(n_heads)]
        lfs = [lf_ref[r0:r0 + C, head_cols(h)] for r0, h in items]
        qs = [q_ref[r0:r0 + C, head_cols(h)].astype(F32) for r0, h in items]
        vs = [v_ref[r0:r0 + C, head_cols(h)] for r0, h in items]
        cums, locs = zip(*[cumulative(lf) for lf in lfs])
        kks = [1.0 - jnp.exp(lf) for lf in lfs]
        totals = [cum[edge:edge + 1, :] for cum in cums]
        a_mats = []
        for q, kk, cum, loc in zip(qs, kks, cums, locs):
            qh_b = (q * jnp.exp(loc)).astype(BF16)
            blocks = []
            for s in range(n_sub):
                lo = s * HG_SUB
                hi = lo + HG_SUB
                if reverse:
                    edge_row = cum[hi:hi + 1, :] if s < n_sub - 1 else jnp.zeros((1, HG_DK), F32)
                    expo = jnp.where(row_id >= lo, edge_row - cum, NEG_INF)
                    tri = lane_c >= lo + sub_row
                else:
                    edge_row = cum[lo - 1:lo, :] if s > 0 else jnp.zeros((1, HG_DK), F32)
                    expo = jnp.where(row_id < hi, edge_row - cum, NEG_INF)
                    tri = lane_c <= lo + sub_row
                k_rel = (kk * jnp.exp(expo)).astype(BF16)
                blocks.append(jnp.where(tri, _dot_nt(qh_b[lo:hi], k_rel), 0.0))
            a_mats.append(jnp.concatenate(blocks, axis=0).astype(BF16))
        intras = [_dot(a, v) for a, v in zip(a_mats, vs)]
        updates = [_dot_tn(v, (kk * jnp.exp(tot - cum)).astype(BF16))
                   for v, kk, tot, cum in zip(vs, kks, totals, cums)]
        q_ins = [(q * jnp.exp(cum)).astype(BF16) for q, cum in zip(qs, cums)]
        states = [st_ref[h] for h in range(n_heads)]
        for (r0, h), q_in, intra, tot, upd in zip(items, q_ins, intras, totals, updates):
            emit(r0, h, _dot_nt(q_in, states[h].astype(BF16)) + intra)
            states[h] = states[h] * jnp.exp(tot) + upd
        for h in range(n_heads):
            st_ref[h] = states[h]

    mild_decay = jnp.min(lf_ref[...]) >= -HG_SAFE_LOG

    @pl.when(mild_decay)
    def _():
        staged_tile()

    @pl.when(jnp.logical_not(mild_decay))
    def _():
        for head in range(n_heads):
            lax.fori_loop(0, n_chunk, functools.partial(pairwise_chunk, head=head), 0)


def _hgrn_pass(q, v, lf, groups, *, reverse, finalize_args=None):
    T = q.shape[0]
    tb = TOKEN_TILE
    finalize = finalize_args is not None

    def tmap(h, i):
        return (groups.reversed_tile(i, tb) if reverse else i, h)

    spec = pl.BlockSpec((tb, HG_DK * HG_HEADS_PER_STEP), tmap)
    in_specs = [spec, spec, spec]
    args = [q, v, lf]
    if finalize:
        o_fwd, gate, gain = finalize_args
        in_specs += [spec, spec, pl.BlockSpec((1, HG_DK), lambda h, i: (0, 0))]
        args += [o_fwd, gate, gain]
    return pl.pallas_call(
        functools.partial(_hgrn_kernel, reverse=reverse, finalize=finalize, groups=groups),
        out_shape=jax.ShapeDtypeStruct((T, HG_WIDTH), BF16 if finalize else F32),
        grid=(HG_HEADS // HG_HEADS_PER_STEP, T // tb),
        in_specs=in_specs,
        out_specs=spec,
        scratch_shapes=[pltpu.VMEM((HG_HEADS_PER_STEP, HG_DK, HG_DK), F32)],
        compiler_params=_cparams(("parallel", "arbitrary")),
        name="hgrn_bwd" if reverse else "hgrn_fwd",
    )(*args)


def _kvproj_kernel(m_ref, w_ref, o_ref):
    o_ref[...] = _dot(m_ref[...].astype(BF16), w_ref[...]).astype(BF16)


def _kvproj(mem, w):
    R, D = mem.shape
    n = w.shape[1]
    tr = 256
    return pl.pallas_call(
        _kvproj_kernel,
        out_shape=jax.ShapeDtypeStruct((R, n), BF16),
        grid=(R // tr,),
        in_specs=[pl.BlockSpec((tr, D), lambda i: (i, 0)), pl.BlockSpec((D, n), lambda i: (0, 0))],
        out_specs=pl.BlockSpec((tr, n), lambda i: (i, 0)),
        compiler_params=_cparams(("parallel",)),
        name="kvproj",
    )(mem, w)


def _mix_xattn_kernel(x0_ref, x1_ref, att_ref, rec_ref, g0_ref, b0_ref, wo_ref, g1_ref, b1_ref, wq_ref, kv_ref,
                      wxo_ref, g2_ref, b2_ref, wr_ref, br_ref, x2_ref, p_ref, idx_ref, gate_ref, rank_ref, cnt_ref,
                      run_ref, *, tiles_group0):
    D = x0_ref.shape[1]
    dh = D // X_HEADS
    x_tile = _group_rows(x0_ref, x1_ref, tiles_group0)
    x2_parts = []

    def stages(rows):
        mix = _dot(att_ref[rows, :], wo_ref[:ATT_WIDTH, :]) + _dot(rec_ref[rows, :], wo_ref[ATT_WIDTH:, :])
        yield
        xn = _layer_norm(x_tile[rows], g0_ref[...], b0_ref[...])
        x1 = _layer_norm(DEEPNORM_ALPHA * xn + mix, g1_ref[...], b1_ref[...])
        qx = (_dot(x1.astype(BF16), wq_ref[...]) * (dh ** -0.5)).astype(BF16)
        yield
        scores = [_dot_nt(qx[:, h * dh:(h + 1) * dh], kv_ref[0, :, h * dh:(h + 1) * dh]) for h in range(X_HEADS)]
        yield
        heads = []
        for h, s in enumerate(scores):
            s = s - jnp.max(s, axis=-1, keepdims=True)
            p = jnp.exp(s)
            p = p / jnp.sum(p, axis=-1, keepdims=True)
            heads.append(_dot(p.astype(BF16), kv_ref[0, :, D + h * dh:D + (h + 1) * dh]).astype(BF16))
        yield
        xo = _dot(jnp.concatenate(heads, axis=-1), wxo_ref[...])
        yield
        x2 = _layer_norm(DEEPNORM_ALPHA * x1 + xo, g2_ref[...], b2_ref[...])
        x2_ref[rows, :] = x2
        _store_slabs(p_ref, _pack_bf16_pair(x2[:, :D // 2], x2[:, D // 2:]), row0=rows.start)
        x2_parts.append(x2)

    _staggered(stages, x_tile.shape[0])
    _route(jnp.concatenate(x2_parts, axis=0), wr_ref, br_ref, idx_ref, gate_ref, rank_ref, cnt_ref, run_ref)


def _mix_xattn(x0, x1, att, rec, g0, b0, wo, g1, b1, wq, kv, wxo, g2, b2, w_router_t, b_router, seq_of_tile):
    D = x0.shape[1]
    T = x0.shape[0] + x1.shape[0]
    tm = TOKEN_TILE
    tiles_group0 = x0.shape[0] // tm
    n_mem = kv.shape[1]
    row = lambda i: (i, 0)
    const = lambda i: (0, 0)
    vec = pl.BlockSpec((1, D), const)
    tok = pl.BlockSpec((TOP_K, tm), lambda i: (0, i))
    return pl.pallas_call(
        functools.partial(_mix_xattn_kernel, tiles_group0=tiles_group0),
        out_shape=(jax.ShapeDtypeStruct((T, D), F32), jax.ShapeDtypeStruct((T * SLAB, LANES), U32),
                   jax.ShapeDtypeStruct((TOP_K, T), I32), jax.ShapeDtypeStruct((TOP_K, T), F32),
                   jax.ShapeDtypeStruct((TOP_K, T), I32), jax.ShapeDtypeStruct((N_EXPERTS, 1), I32)),
        grid=(T // tm,),
        in_specs=_group_specs(tm, D, tiles_group0) + [
            pl.BlockSpec((tm, ATT_WIDTH), row),
            pl.BlockSpec((tm, HG_WIDTH), row),
            vec, vec,
            pl.BlockSpec((ATT_WIDTH + HG_WIDTH, D), const),
            vec, vec,
            pl.BlockSpec((D, D), const),
            pl.BlockSpec((1, n_mem, 2 * D), lambda i: (seq_of_tile(i), 0, 0)),
            pl.BlockSpec((D, D), const),
            vec, vec,
            pl.BlockSpec((N_EXPERTS, D), const),
            pl.BlockSpec((N_EXPERTS, 1), const),
        ],
        out_specs=(pl.BlockSpec((tm, D), row), pl.BlockSpec((tm * SLAB, LANES), row), tok, tok, tok,
                   pl.BlockSpec((N_EXPERTS, 1), const)),
        scratch_shapes=[pltpu.VMEM((N_EXPERTS, 1), F32)],
        compiler_params=_cparams(("arbitrary",)),
        name="mix_xattn",
    )(x0, x1, att, rec, g0, b0, wo, g1, b1, wq, kv, wxo, g2, b2, w_router_t, b_router)


def _route(x, wt_ref, b_ref, idx_ref, gate_ref, rank_ref, cnt_ref, run_ref):
    tm = x.shape[0]

    @pl.when(pl.program_id(0) == 0)
    def _():
        run_ref[...] = jnp.zeros_like(run_ref)

    xh, xl, _ = _split3(x)
    wh, wl, _ = _split3(wt_ref[...])
    logits = _dot_nt(wh, xh) + _dot_nt(wh, xl) + _dot_nt(wl, xh) + b_ref[...]
    e_id = lax.broadcasted_iota(I32, (N_EXPERTS, tm), 0)
    work = logits
    vals, idxs = [], []
    chosen = jnp.zeros((N_EXPERTS, tm), F32)
    for _ in range(TOP_K):
        m = jnp.max(work, axis=0, keepdims=True)
        idx = jnp.min(jnp.where(work == m, e_id, N_EXPERTS), axis=0, keepdims=True)
        hit = e_id == idx
        chosen = jnp.where(hit, 1.0, chosen)
        work = jnp.where(hit, NEG_INF, work)
        vals.append(m)
        idxs.append(idx)
    ex = [jnp.exp(v - vals[0]) for v in vals]
    den = ex[0] + ex[1] + ex[2] + ex[3]
    t_r = lax.broadcasted_iota(I32, (tm, tm), 0)
    t_c = lax.broadcasted_iota(I32, (tm, tm), 1)
    before = jnp.where(t_r < t_c, 1.0, 0.0).astype(BF16)
    rank_e = _dot(chosen.astype(BF16), before) + run_ref[...]
    run_ref[...] = run_ref[...] + jnp.sum(chosen, axis=1, keepdims=True)
    for k in range(TOP_K):
        idx_ref[k:k + 1, :] = idxs[k]
        gate_ref[k:k + 1, :] = ex[k] / den
        rank_ref[k:k + 1, :] = jnp.sum(jnp.where(e_id == idxs[k], rank_e, 0.0), axis=0, keepdims=True).astype(I32)
    cnt_ref[...] = run_ref[...].astype(I32)


def _dispatch_kernel(fill_from_ref, fill_len_ref, dest_ref, p_ref, out_hbm, sem, fill_sem, *, n_fill):
    td = DISPATCH_TILE

    @pl.when(pl.program_id(0) == 0)
    def _():
        for r in range(fill_from_ref.shape[0]):
            first, n = fill_from_ref[r], fill_len_ref[r]
            n_chunks = lax.shift_right_logical(n, FILL_CHUNK_LOG2)

            def fill_chunk(j, c):
                rows = pl.ds(first + j * FILL_CHUNK, FILL_CHUNK)
                pltpu.make_async_copy(p_ref.at[pl.ds(0, FILL_CHUNK)], out_hbm.at[rows], fill_sem).start()
                return c

            def fill_row(j, c):
                pltpu.make_async_copy(p_ref.at[0], out_hbm.at[first + n_chunks * FILL_CHUNK + j], fill_sem).start()
                return c

            lax.fori_loop(0, n_chunks, fill_chunk, 0)
            lax.fori_loop(0, n - n_chunks * FILL_CHUNK, fill_row, 0)
        filled = out_hbm.at[pl.ds(0, n_fill)]
        pltpu.make_async_copy(filled, filled, fill_sem).wait()

    def start(t, c):
        for k in range(TOP_K):
            d = dest_ref[0, 0, k * td + t]
            pltpu.make_async_copy(p_ref.at[t], out_hbm.at[d], sem).start(priority=k % DMA_PRIORITIES)
        return c

    lax.fori_loop(0, td, start, 0, unroll=DMA_ISSUE_UNROLL)
    everything = out_hbm.at[pl.ds(0, TOP_K * td)]
    pltpu.make_async_copy(everything, everything, sem).wait()


def _dispatch(fill_from, fill_len, dest_tiles, p, n_rows):
    T = p.shape[0]
    td = DISPATCH_TILE
    grid_spec = pltpu.PrefetchScalarGridSpec(
        num_scalar_prefetch=2,
        grid=(T // td,),
        in_specs=[pl.BlockSpec((1, 1, TOP_K * td), lambda i, ff, fl: (i, 0, 0), memory_space=pltpu.SMEM),
                  pl.BlockSpec((td,) + p.shape[1:], lambda i, ff, fl: (i, 0, 0))],
        out_specs=pl.BlockSpec(memory_space=pl.ANY),
        scratch_shapes=[pltpu.SemaphoreType.DMA, pltpu.SemaphoreType.DMA],
    )
    return pl.pallas_call(
        functools.partial(_dispatch_kernel, n_fill=n_rows - T * TOP_K),
        out_shape=jax.ShapeDtypeStruct((n_rows,) + p.shape[1:], p.dtype),
        grid_spec=grid_spec,
        compiler_params=_cparams(("arbitrary",)),
        name="dispatch",
    )(fill_from, fill_len, dest_tiles, p)


def _expert_kernel(blk_e_ref, nvalid_ref, p_ref, wu_ref, bu_ref, wd_ref, bd_ref, y_ref, wu_b, wd_b):
    i = pl.program_id(0)
    dff = wd_ref.shape[1]
    half = SLAB * LANES
    live = i < nvalid_ref[0]
    new_expert = (i == 0) | (blk_e_ref[i] != blk_e_ref[jnp.maximum(i - 1, 0)])

    @pl.when(live & new_expert)
    def _():
        wu_b[...] = wu_ref[0].astype(BF16)
        wd_b[...] = wd_ref[0].astype(BF16)

    @pl.when(live)
    def _():
        lo, hi = _unpack_bf16_pair(_load_slabs(p_ref))
        lo = lo.astype(BF16)
        hi = hi.astype(BF16)

        def up(c0):
            return (_dot(lo, wu_b[:half, c0:c0 + dff]) + _dot(hi, wu_b[half:, c0:c0 + dff])
                    + bu_ref[0, :, c0:c0 + dff])

        glu = jnp.minimum(up(0), SWIGLU_LIMIT)
        lin = jnp.clip(up(dff), -SWIGLU_LIMIT, SWIGLU_LIMIT)
        a = glu * _sigmoid(SWIGLU_ALPHA * glu) * (lin + 1.0)
        y = _dot(a.astype(BF16), wd_b[...]) + bd_ref[0]
        _store_slabs(y_ref, _pack_bf16_pair(y[:, :half], y[:, half:]))

    @pl.when(i >= nvalid_ref[0])
    def _():
        y_ref[...] = jnp.zeros_like(y_ref)


def _experts(blk_expert, n_valid, ps, wu, bu, wd, bd):
    R = ps.shape[0] // SLAB
    te = EXPERT_TILE
    E, D, two_f = wu.shape
    dff = wd.shape[1]
    grid_spec = pltpu.PrefetchScalarGridSpec(
        num_scalar_prefetch=2,
        grid=(R // te,),
        in_specs=[
            pl.BlockSpec((te * SLAB, LANES), lambda i, be, nv: (i, 0)),
            pl.BlockSpec((1, D, two_f), lambda i, be, nv: (be[i], 0, 0)),
            pl.BlockSpec((1, 1, two_f), lambda i, be, nv: (be[i], 0, 0)),
            pl.BlockSpec((1, dff, D), lambda i, be, nv: (be[i], 0, 0)),
            pl.BlockSpec((1, 1, D), lambda i, be, nv: (be[i], 0, 0)),
        ],
        out_specs=pl.BlockSpec((te * SLAB, LANES), lambda i, be, nv: (i, 0)),
        scratch_shapes=[pltpu.VMEM((D, two_f), BF16), pltpu.VMEM((dff, D), BF16)],
    )
    return pl.pallas_call(
        _expert_kernel,
        out_shape=jax.ShapeDtypeStruct((R * SLAB, LANES), U32),
        grid_spec=grid_spec,
        compiler_params=_cparams(("arbitrary",)),
        name="experts",
    )(blk_expert, n_valid, ps, wu, bu, wd, bd)


def _combine_kernel(dest_ref, dest_next_ref, ys_hbm, x2_ref, gate_ref, g_ref, b_ref, o_ref, buf, sems):
    tc = x2_ref.shape[0]
    half = SLAB * LANES
    i = pl.program_id(0)
    slot = i % 2

    def start_tile(idx_ref, s):
        def body(t, c):
            rows = pl.ds(pl.multiple_of(t * SLAB, SLAB), SLAB)
            for k in range(TOP_K):
                d = idx_ref[0, 0, k * tc + t]
                pltpu.make_async_copy(ys_hbm.at[d], buf.at[s, k, rows], sems.at[s]).start(
                    priority=k % DMA_PRIORITIES)
            return c
        lax.fori_loop(0, tc, body, 0, unroll=DMA_ISSUE_UNROLL)

    @pl.when(i == 0)
    def _():
        start_tile(dest_ref, 0)

    for s in range(2):
        @pl.when((i + 1 < pl.num_programs(0)) & (slot == s))
        def _():
            start_tile(dest_next_ref, 1 - s)

    for s in range(2):
        @pl.when(slot == s)
        def _():
            pltpu.make_async_copy(buf.at[s], buf.at[s], sems.at[s]).wait()

    gates = gate_ref[...]
    acc_lo = jnp.zeros((tc, half), F32)
    acc_hi = jnp.zeros((tc, half), F32)
    for k in range(TOP_K):
        lo, hi = _unpack_bf16_pair(_load_slabs(buf.at[slot, k]))
        gk = gates[:, k:k + 1]
        acc_lo = acc_lo + gk * lo
        acc_hi = acc_hi + gk * hi
    ffn = jnp.concatenate([acc_lo, acc_hi], axis=-1)
    o_ref[...] = _layer_norm(DEEPNORM_ALPHA * x2_ref[...] + ffn, g_ref[...], b_ref[...])


def _combine(dest_tiles, ys, x2, gates, g, b, *, tile0, n_tiles):
    D = x2.shape[1]
    tc = COMBINE_TILE
    return pl.pallas_call(
        _combine_kernel,
        out_shape=jax.ShapeDtypeStruct((n_tiles * tc, D), F32),
        grid=(n_tiles,),
        in_specs=[pl.BlockSpec((1, 1, TOP_K * tc), lambda i: (tile0 + i, 0, 0), memory_space=pltpu.SMEM),
                  pl.BlockSpec((1, 1, TOP_K * tc), lambda i: (tile0 + jnp.minimum(i + 1, n_tiles - 1), 0, 0),
                               memory_space=pltpu.SMEM),
                  pl.BlockSpec(memory_space=pl.ANY),
                  pl.BlockSpec((tc, D), lambda i: (tile0 + i, 0)),
                  pl.BlockSpec((tc, TOP_K), lambda i: (tile0 + i, 0)),
                  pl.BlockSpec((1, D), lambda i: (0, 0)),
                  pl.BlockSpec((1, D), lambda i: (0, 0))],
        out_specs=pl.BlockSpec((tc, D), lambda i: (i, 0)),
        scratch_shapes=[pltpu.VMEM((2, TOP_K, tc * SLAB, LANES), U32), pltpu.SemaphoreType.DMA((2,))],
        compiler_params=_cparams(("arbitrary",)),
        name="combine",
    )(dest_tiles, dest_tiles, ys, x2, gates, g, b)


def _rope_tables(seq_len):
    rows = seq_len // GRID_W
    row = jnp.repeat(jnp.arange(rows, dtype=F32), GRID_W)
    colp = (jnp.arange(seq_len) % GRID_W).astype(F32)
    axis_dim = ATT_HEAD_DIM // 2
    inv_freq = ROPE_THETA ** (-jnp.arange(0, axis_dim, 2, dtype=F32) / axis_dim)
    ang = jnp.concatenate([row[:, None] * inv_freq, colp[:, None] * inv_freq], axis=-1)
    cos, sin = jnp.cos(ang), jnp.sin(ang)
    cos_h = jnp.concatenate([cos, cos], axis=-1)
    sin_h = jnp.concatenate([-sin, sin], axis=-1)
    return jnp.tile(cos_h, (1, LANES // ATT_HEAD_DIM)), jnp.tile(sin_h, (1, LANES // ATT_HEAD_DIM))


def _tiles_of(idx_t, tile):
    K, T = idx_t.shape
    return idx_t.reshape(K, T // tile, tile).transpose(1, 0, 2).reshape(T // tile, 1, K * tile)


def kernel(x_prompt, x_sample, mem_prompt, mem_sample, ln_in_g, ln_in_b, w_in, q_norm_g, k_norm_g, hg_lb_fwd, hg_lb_bwd, hg_norm_g, w_out, ln1_g, ln1_b, w_xq, w_xkv, w_xo, ln2_g, ln2_b, w_router, b_router, w_up, b_up, w_down, b_down, ln3_g, ln3_b):
    Bp, Sp, D = x_prompt.shape
    Bs, Ss, _ = x_sample.shape
    Tp, Ts = Bp * Sp, Bs * Ss
    T = Tp + Ts
    tm = TOKEN_TILE
    assert Sp % tm == 0 and Ss % Sp == 0
    n_mem = mem_prompt.shape[1]
    layer = 0

    x0, x1 = x_prompt.reshape(Tp, D), x_sample.reshape(Ts, D)
    mem = jnp.concatenate([mem_prompt.reshape(Bp * n_mem, D), mem_sample.reshape(Bs * n_mem, D)], axis=0)

    def deinterleave(a):
        lead = a.shape[:-1]
        a = a.reshape(*lead, -1, ATT_HEAD_DIM // 2, 2)
        return jnp.swapaxes(a, -1, -2).reshape(*lead, -1)

    n_qk = ATT_WIDTH + ATT_KV_WIDTH
    w_in_l = w_in[layer]
    w_in_b = jnp.concatenate([deinterleave(w_in_l[:, :n_qk]), w_in_l[:, n_qk:]], axis=1).astype(BF16)
    pair = LANES // ATT_HEAD_DIM
    qg = jnp.tile(deinterleave(q_norm_g[layer]), pair).reshape(1, LANES)
    kg = jnp.tile(deinterleave(k_norm_g[layer]), pair).reshape(1, LANES)

    def lower_bound(logits):
        sm = jax.nn.softmax(logits.astype(F32), axis=0)
        return (jnp.cumsum(sm, axis=0)[layer + 1] - sm[0]).reshape(1, HG_WIDTH)

    lbf, lbb = lower_bound(hg_lb_fwd), lower_bound(hg_lb_bwd)
    cos_t, sin_t = _rope_tables(max(Sp, Ss))
    vec = lambda a: a.reshape(1, -1)

    groups = _Groups(Bp, Sp, Bs, Ss)
    pos_block = lambda i: groups.pos_of_tile(i, tm)
    seq_of_tile = lambda i: groups.seq_of_tile(i, tm)

    q, k, v, hq, hi, hg, lff, lfb = _inproj(x0, x1, vec(ln_in_g), vec(ln_in_b), w_in_b, cos_t, sin_t, qg, kg,
                                           lbf, lbb, pos_block)

    score_bound = (jnp.max(jnp.abs(q_norm_g[layer])) * jnp.max(jnp.abs(k_norm_g[layer]))
                   * (ATT_HEAD_DIM ** 0.5 * ATT_NORM_SLACK)).astype(F32).reshape(1)
    att = _attention(score_bound, q, k, v, groups)
    o_fwd = _hgrn_pass(hq, hi, lff, groups, reverse=False)
    gain = hg_norm_g[layer].reshape(1, HG_DK)
    rec = _hgrn_pass(hq, hi, lfb, groups, reverse=True, finalize_args=(o_fwd, hg, gain))

    kv = _kvproj(mem, w_xkv[layer].astype(BF16)).reshape(Bp + Bs, n_mem, 2 * D)
    x2, packed, idx_t, gate_t, rank_t, counts = _mix_xattn(
        x0, x1, att, rec, vec(ln_in_g), vec(ln_in_b), w_out[layer].astype(BF16), vec(ln1_g[layer]),
        vec(ln1_b[layer]), w_xq[layer].astype(BF16), kv, w_xo[layer].astype(BF16), vec(ln2_g[layer]),
        vec(ln2_b[layer]), w_router[layer].T, b_router[layer].reshape(N_EXPERTS, 1), seq_of_tile)

    te = EXPERT_TILE
    counts = counts.reshape(N_EXPERTS)
    padded = (counts + te - 1) // te * te
    pad_end = jnp.cumsum(padded)
    pad_start = pad_end - padded
    n_blk = (T * TOP_K + N_EXPERTS * (te - 1) + te - 1) // te
    e_ids = jnp.arange(N_EXPERTS, dtype=I32)
    start_of = jnp.sum(jnp.where(idx_t[None] == e_ids[:, None, None], pad_start[:, None, None], 0), axis=0)
    dest_t = start_of + rank_t
    blk_row0 = jnp.arange(n_blk, dtype=I32) * te
    blk_expert = jnp.minimum(jnp.sum((pad_end[None, :] <= blk_row0[:, None]).astype(I32), axis=1), N_EXPERTS - 1)
    n_valid = (pad_end[-1] // te).astype(I32).reshape(1)

    n_rows = n_blk * te
    fill_from = jnp.concatenate([pad_start + counts, pad_end[-1:]]).astype(I32)
    fill_len = jnp.concatenate([padded - counts, n_rows - pad_end[-1:]]).astype(I32)
    sorted_rows = _dispatch(fill_from, fill_len, _tiles_of(dest_t, DISPATCH_TILE),
                            packed.reshape(T, SLAB, LANES), n_rows)
    ys = _experts(blk_expert, n_valid, sorted_rows.reshape(n_rows * SLAB, LANES), w_up[layer],
                  b_up[layer].reshape(N_EXPERTS, 1, -1), w_down[layer], b_down[layer].reshape(N_EXPERTS, 1, -1))
    ys = ys.reshape(n_rows, SLAB, LANES)

    dest_c = _tiles_of(dest_t, COMBINE_TILE)
    gates = gate_t.T
    tc = COMBINE_TILE
    y_p = _combine(dest_c, ys, x2, gates, vec(ln3_g[layer]), vec(ln3_b[layer]), tile0=0, n_tiles=Tp // tc)
    y_s = _combine(dest_c, ys, x2, gates, vec(ln3_g[layer]), vec(ln3_b[layer]), tile0=Tp // tc, n_tiles=Ts // tc)
    return y_p.reshape(Bp, Sp, D), y_s.reshape(Bs, Ss, D)
```

```python
import functools

import jax
import jax.numpy as jnp
import numpy as np
from jax import lax
from jax.experimental import pallas as pl
from jax.experimental.pallas import tpu as pltpu

F32 = jnp.float32
BF16 = jnp.bfloat16
I32 = jnp.int32
U32 = jnp.uint32

GRID_W = 64
ATT_HEADS = 8
ATT_KV_HEADS = 2
ATT_HEAD_DIM = 64
ATT_WIDTH = ATT_HEADS * ATT_HEAD_DIM
ATT_KV_WIDTH = ATT_KV_HEADS * ATT_HEAD_DIM
ROPE_THETA = 10000.0
HG_HEADS = 4
HG_DK = 128
HG_WIDTH = HG_HEADS * HG_DK
X_HEADS = 4
N_EXPERTS = 32
TOP_K = 4
SWIGLU_LIMIT = 7.0
SWIGLU_ALPHA = 1.702
LN_EPS = 1e-5
RMS_EPS = 1e-6
DEPTH = 1
DEEPNORM_ALPHA = (2.0 * DEPTH) ** 0.25

LANES = 128
VMEM_LIMIT_BYTES = 56 * 1024 * 1024
DMA_PRIORITIES = 2
DMA_ISSUE_UNROLL = 4

TOKEN_TILE = 512
ATT_Q_TILE = 256
ATT_K_TILE = 4096
HG_CHUNK = 64
HG_SUB = 16
HG_HEADS_PER_STEP = 4

ATT_SAFE_SCORE = 60.0
ATT_NORM_SLACK = 1.02
HG_SAFE_LOG = 5.0
EXPERT_TILE = 512
COMBINE_TILE = 512
DISPATCH_TILE = 1024
FILL_CHUNK_LOG2 = 5
FILL_CHUNK = 1 << FILL_CHUNK_LOG2
ROW_PARTS = 2
NEG_INF = float("-inf")


def _cparams(sem):
    return pltpu.CompilerParams(dimension_semantics=sem, vmem_limit_bytes=VMEM_LIMIT_BYTES)


def _layer_norm(x, g, b):
    mu = jnp.mean(x, axis=-1, keepdims=True)
    xc = x - mu
    var = jnp.mean(xc * xc, axis=-1, keepdims=True)
    return xc * lax.rsqrt(var + LN_EPS) * g + b


def _sigmoid(x):
    return 1.0 / (1.0 + jnp.exp(-x))


def _dot(a, b):
    return jnp.dot(a, b, preferred_element_type=F32)


def _dot_nt(a, b):
    return lax.dot_general(a, b, (((1,), (1,)), ((), ())), preferred_element_type=F32)


def _dot_tn(a, b):
    return lax.dot_general(a, b, (((0,), (0,)), ((), ())), preferred_element_type=F32)


def _split3(x):
    h1 = x.astype(BF16)
    r1 = x - h1.astype(F32)
    h2 = r1.astype(BF16)
    h3 = (r1 - h2.astype(F32)).astype(BF16)
    return h1, h2, h3


def _pack_bf16_pair(lo, hi):
    lo_b = pltpu.bitcast(lo.astype(BF16).astype(F32), U32)
    hi_b = pltpu.bitcast(hi.astype(BF16).astype(F32), U32)
    return (hi_b & jnp.uint32(0xFFFF0000)) | (lo_b >> 16)


def _unpack_bf16_pair(p):
    lo = pltpu.bitcast(p << 16, F32)
    hi = pltpu.bitcast(p & jnp.uint32(0xFFFF0000), F32)
    return lo, hi


SLAB = 4


def _store_slabs(ref, value, row0=0):
    n = value.shape[0]
    for s in range(SLAB):
        ref[pl.ds(row0 * SLAB + s, n, stride=SLAB), :] = value[:, s * LANES:(s + 1) * LANES]


def _load_slabs(ref, row0=0, n=None):
    n = ref.shape[0] // SLAB if n is None else n
    return jnp.concatenate([ref[pl.ds(row0 * SLAB + s, n, stride=SLAB), :] for s in range(SLAB)], axis=-1)


def _row_parts(n_rows):
    part = n_rows // ROW_PARTS
    return [slice(r * part, (r + 1) * part) for r in range(ROW_PARTS)]


def _staggered(stage_fn, n_rows):
    pending = [stage_fn(rows) for rows in _row_parts(n_rows)]
    while pending:
        for g in list(pending):
            try:
                next(g)
            except StopIteration:
                pending.remove(g)


def _group_rows(x0_ref, x1_ref, tiles_group0):
    return jnp.where(pl.program_id(0) < tiles_group0, x0_ref[...], x1_ref[...])


def _group_specs(tile, width, tiles_group0):
    return [pl.BlockSpec((tile, width), lambda i: (jnp.minimum(i, tiles_group0 - 1), 0)),
            pl.BlockSpec((tile, width), lambda i: (jnp.maximum(i - tiles_group0, 0), 0))]


def _inproj_kernel(x0_ref, x1_ref, g_ref, b_ref, w_ref, cos_ref, sin_ref, qg_ref, kg_ref, lbf_ref, lbb_ref,
                   q_out, k_out, v_out, hq_out, hi_out, hg_out, lff_out, lfb_out, *, tiles_group0):
    xn = _layer_norm(_group_rows(x0_ref, x1_ref, tiles_group0), g_ref[...], b_ref[...])
    xb = xn.astype(BF16)
    cos = cos_ref[...]
    sin = sin_ref[...]
    tm = xb.shape[0]

    lane = lax.broadcasted_iota(I32, (tm, LANES), 1)
    first_half = (lane & (ATT_HEAD_DIM - 1)) < (ATT_HEAD_DIM // 2)
    r = lax.broadcasted_iota(I32, (LANES, LANES), 0)
    c = lax.broadcasted_iota(I32, (LANES, LANES), 1)
    head_ones = jnp.where((r >> 6) == (c >> 6), 1.0, 0.0).astype(BF16)

    def normed_rope(u, gain, scale):
        sq = u * u
        s_hi = sq.astype(BF16)
        s_lo = (sq - s_hi.astype(F32)).astype(BF16)
        ms = (_dot(s_hi, head_ones) + _dot(s_lo, head_ones)) * (1.0 / ATT_HEAD_DIM)
        un = u * lax.rsqrt(ms + RMS_EPS) * gain
        rot = jnp.where(first_half, pltpu.roll(un, LANES - 32, 1), pltpu.roll(un, 32, 1))
        return (un * cos + rot * sin) * scale

    n_qkv = ATT_WIDTH + 2 * ATT_KV_WIDTH
    u_att = [_dot(xb, w_ref[:, c:c + 2 * LANES]) for c in range(0, n_qkv, 2 * LANES)]
    col = n_qkv

    def hg_proj():
        nonlocal col
        u = _dot(xb, w_ref[:, col:col + HG_WIDTH])
        col += HG_WIDTH
        return u

    u = hg_proj()
    hq_out[...] = (u * _sigmoid(u)).astype(BF16)
    for j in range(ATT_WIDTH // (2 * LANES)):
        for h in range(2):
            c0 = (2 * j + h) * LANES
            q_out[:, c0:c0 + LANES] = normed_rope(u_att[j][:, h * LANES:(h + 1) * LANES], qg_ref[...],
                                                  ATT_HEAD_DIM ** -0.5).astype(BF16)
        u = hg_proj()
        lb = (lbf_ref, lbb_ref)[j][...]
        (lff_out, lfb_out)[j][...] = jnp.log(lb + (1.0 - lb) * _sigmoid(u))
    k_out[...] = normed_rope(u_att[-1][:, :LANES], kg_ref[...], 1.0).astype(BF16)
    v_out[:, :LANES] = u_att[-1][:, LANES:].astype(BF16)
    v_out[:, LANES:] = jnp.ones((tm, LANES), BF16)
    hi_out[...] = hg_proj().astype(BF16)
    u = hg_proj()
    hg_out[...] = (u * _sigmoid(u)).astype(BF16)


def _inproj(x0, x1, g, b, w, cos_t, sin_t, qg, kg, lbf, lbb, pos_block):
    D = x0.shape[1]
    T = x0.shape[0] + x1.shape[0]
    tm = TOKEN_TILE
    tiles_group0 = x0.shape[0] // tm
    n_in = w.shape[1]
    row = lambda i: (i, 0)
    const = lambda i: (0, 0)
    out_shape = (
        jax.ShapeDtypeStruct((T, ATT_WIDTH), BF16),
        jax.ShapeDtypeStruct((T, ATT_KV_WIDTH), BF16),
        jax.ShapeDtypeStruct((T, 2 * ATT_KV_WIDTH), BF16),
        jax.ShapeDtypeStruct((T, HG_WIDTH), BF16),
        jax.ShapeDtypeStruct((T, HG_WIDTH), BF16),
        jax.ShapeDtypeStruct((T, HG_WIDTH), BF16),
        jax.ShapeDtypeStruct((T, HG_WIDTH), F32),
        jax.ShapeDtypeStruct((T, HG_WIDTH), F32),
    )
    out_specs = (
        pl.BlockSpec((tm, ATT_WIDTH), row),
        pl.BlockSpec((tm, ATT_KV_WIDTH), row),
        pl.BlockSpec((tm, 2 * ATT_KV_WIDTH), row),
        pl.BlockSpec((tm, HG_WIDTH), row),
        pl.BlockSpec((tm, HG_WIDTH), row),
        pl.BlockSpec((tm, HG_WIDTH), row),
        pl.BlockSpec((tm, HG_WIDTH), row),
        pl.BlockSpec((tm, HG_WIDTH), row),
    )
    return pl.pallas_call(
        functools.partial(_inproj_kernel, tiles_group0=tiles_group0),
        out_shape=out_shape,
        grid=(T // tm,),
        in_specs=_group_specs(tm, D, tiles_group0) + [
            pl.BlockSpec((1, D), const),
            pl.BlockSpec((1, D), const),
            pl.BlockSpec((D, n_in), const, pipeline_mode=pl.Buffered(1)),
            pl.BlockSpec((tm, LANES), lambda i: (pos_block(i), 0)),
            pl.BlockSpec((tm, LANES), lambda i: (pos_block(i), 0)),
            pl.BlockSpec((1, LANES), const),
            pl.BlockSpec((1, LANES), const),
            pl.BlockSpec((1, HG_WIDTH), const),
            pl.BlockSpec((1, HG_WIDTH), const),
        ],
        out_specs=out_specs,
        compiler_params=_cparams(("parallel",)),
        name="inproj",
    )(x0, x1, g, b, w, cos_t, sin_t, qg, kg, lbf, lbb)


class _Groups:
    def __init__(self, n0, s0, n1, s1):
        self.n0, self.s0, self.n1, self.s1 = n0, s0, n1, s1
        self.t0 = n0 * s0
        self.total = self.t0 + n1 * s1

    def split(self, i, tile):
        nt0 = self.t0 // tile
        per0, per1 = self.s0 // tile, self.s1 // tile
        in1 = i >= nt0
        j1 = jnp.maximum(i - nt0, 0)
        seq = jnp.where(in1, self.n0 + j1 // per1, i // per0)
        pos = jnp.where(in1, j1 % per1, i % per0)
        return in1, seq, pos, jnp.where(in1, per1, per0)

    def seq_of_tile(self, i, tile):
        return self.split(i, tile)[1]

    def pos_of_tile(self, i, tile):
        return self.split(i, tile)[2]

    def reversed_tile(self, i, tile):
        _, _, pos, per = self.split(i, tile)
        return i - pos + (per - 1 - pos)


def _attention_kernel(*refs, n_kv_blocks, groups):
    bound_ref, q_ref = refs[:2]
    k_refs = refs[2:2 + n_kv_blocks]
    v_refs = refs[2 + n_kv_blocks:2 + 2 * n_kv_blocks]
    o_ref, out_ref = refs[2 + 2 * n_kv_blocks:]
    tq = q_ref.shape[0]
    kv_rows = k_refs[0].shape[0]
    tk = min(ATT_K_TILE, kv_rows)
    n_k = kv_rows // tk
    group = ATT_HEADS // ATT_KV_HEADS
    rows = group * tq
    lane = lax.broadcasted_iota(I32, (tq, LANES), 1)
    low_half = lane < ATT_HEAD_DIM
    in_group1 = groups.split(pl.program_id(0), tq)[0]
    extra_trips = jnp.where(in_group1, n_k, 0)
    unshifted_ok = bound_ref[0] <= ATT_SAFE_SCORE

    for kvh in range(ATT_KV_HEADS):
        keep = low_half if kvh == 0 else jnp.logical_not(low_half)
        parts = []
        for g in range(group):
            c0 = kvh * 2 * LANES + (g // 2) * LANES
            pair = q_ref[:, c0:c0 + LANES].astype(F32)
            if g % 2 != kvh:
                pair = pltpu.roll(pair, ATT_HEAD_DIM, 1)
            parts.append(jnp.where(keep, pair, 0.0).astype(BF16))
        qp = jnp.concatenate(parts, axis=0)

        def run(make_body, carry):
            carry = lax.fori_loop(0, n_k, make_body(k_refs[0], v_refs[0]), carry)
            for kb in range(1, n_kv_blocks):
                carry = lax.fori_loop(0, extra_trips, make_body(k_refs[kb], v_refs[kb]), carry)
            return carry

        @pl.when(unshifted_ok)
        def _():
            def make_body(k_ref, v_ref):
                def body(kc, acc):
                    r0 = pl.multiple_of(kc * tk, tk)
                    p = jnp.exp(_dot_nt(qp, k_ref[pl.ds(r0, tk), :]))
                    return acc + _dot(p.astype(BF16), v_ref[pl.ds(r0, tk), :])
                return body

            acc = run(make_body, jnp.zeros((rows, 2 * LANES), F32))
            out_ref[...] = acc[:, :LANES] / acc[:, LANES:]

        @pl.when(jnp.logical_not(unshifted_ok))
        def _():
            def make_body(k_ref, v_ref):
                def body(kc, carry):
                    m, l, acc = carry
                    r0 = pl.multiple_of(kc * tk, tk)
                    s = _dot_nt(qp, k_ref[pl.ds(r0, tk), :])
                    m_new = jnp.maximum(m, jnp.max(s, axis=-1, keepdims=True))
                    alpha = jnp.exp(m - m_new)
                    p = jnp.exp(s - m_new)
                    l_new = alpha * l + jnp.sum(p, axis=-1, keepdims=True)
                    acc_new = alpha * acc + _dot(p.astype(BF16), v_ref[pl.ds(r0, tk), :LANES])
                    return m_new, l_new, acc_new
                return body

            init = (jnp.full((rows, 1), NEG_INF, F32), jnp.zeros((rows, 1), F32), jnp.zeros((rows, LANES), F32))
            m, l, acc = run(make_body, init)
            out_ref[...] = acc / l

        out = out_ref[...]
        for j in range(group // 2):
            a = out[(2 * j) * tq:(2 * j + 1) * tq]
            b = out[(2 * j + 1) * tq:(2 * j + 2) * tq]
            if kvh == 1:
                a = pltpu.roll(a, ATT_HEAD_DIM, 1)
            else:
                b = pltpu.roll(b, ATT_HEAD_DIM, 1)
            c0 = kvh * 2 * LANES + j * LANES
            o_ref[:, c0:c0 + LANES] = jnp.where(low_half, a, b).astype(BF16)


def _attention(score_bound, q, k, v, groups):
    T = q.shape[0]
    tq = ATT_Q_TILE
    kv_rows = groups.s0
    n_kv_blocks = groups.s1 // groups.s0

    def kv_map(j):
        def index_map(i):
            in1, seq, _, _ = groups.split(i, tq)
            first = jnp.where(in1, groups.t0 // kv_rows + (seq - groups.n0) * n_kv_blocks, seq)
            return (first + jnp.where(in1, j, 0), 0)
        return index_map

    k_specs = [pl.BlockSpec((kv_rows, ATT_KV_WIDTH), kv_map(j)) for j in range(n_kv_blocks)]
    v_specs = [pl.BlockSpec((kv_rows, 2 * ATT_KV_WIDTH), kv_map(j)) for j in range(n_kv_blocks)]
    return pl.pallas_call(
        functools.partial(_attention_kernel, n_kv_blocks=n_kv_blocks, groups=groups),
        out_shape=jax.ShapeDtypeStruct((T, ATT_WIDTH), BF16),
        grid=(T // tq,),
        in_specs=[pl.BlockSpec(memory_space=pltpu.SMEM),
                  pl.BlockSpec((tq, ATT_WIDTH), lambda i: (i, 0))] + k_specs + v_specs,
        out_specs=pl.BlockSpec((tq, ATT_WIDTH), lambda i: (i, 0)),
        scratch_shapes=[pltpu.VMEM((ATT_HEADS // ATT_KV_HEADS * tq, LANES), F32)],
        compiler_params=_cparams(("parallel",)),
        name="attention",
    )(score_bound, q, *([k] * n_kv_blocks), *([v] * n_kv_blocks))


def _hgrn_kernel(*refs, reverse, finalize, groups):
    if finalize:
        q_ref, v_ref, lf_ref, of_ref, g_ref, gain_ref, o_ref, st_ref = refs
    else:
        q_ref, v_ref, lf_ref, o_ref, st_ref = refs
    tb = q_ref.shape[0]
    n_heads = q_ref.shape[1] // HG_DK
    C = min(HG_CHUNK, tb)
    n_chunk = tb // C
    n_sub = C // HG_SUB

    @pl.when(groups.pos_of_tile(pl.program_id(1), tb) == 0)
    def _():
        st_ref[...] = jnp.zeros_like(st_ref)

    ri = lax.broadcasted_iota(I32, (C, C), 0)
    ci = lax.broadcasted_iota(I32, (C, C), 1)
    sub_lo = (ri >> 4) << 4
    if reverse:
        loc_m = (ci >= ri) & (ci < sub_lo + HG_SUB)
    else:
        loc_m = (ci <= ri) & (ci >= sub_lo)
    loc_mat = jnp.where(loc_m, 1.0, 0.0).astype(BF16)
    row_id = lax.broadcasted_iota(I32, (C, 1), 0)
    sub_row = lax.broadcasted_iota(I32, (HG_SUB, 1), 0)
    lane_c = lax.broadcasted_iota(I32, (HG_SUB, C), 1)
    edge = 0 if reverse else C - 1

    def cumulative(lf):
        h1, h2, h3 = _split3(lf)
        loc = _dot(loc_mat, h1) + _dot(loc_mat, h2) + _dot(loc_mat, h3)
        subs = [loc[s * HG_SUB:(s + 1) * HG_SUB] for s in range(n_sub)]
        order = range(n_sub - 1, -1, -1) if reverse else range(n_sub)
        carry = jnp.zeros((1, HG_DK), F32)
        for s in order:
            sub_total = subs[s][0:1, :] if reverse else subs[s][HG_SUB - 1:HG_SUB, :]
            subs[s] = subs[s] + carry
            carry = carry + sub_total
        return jnp.concatenate(subs, axis=0), loc

    def head_cols(head):
        return slice(head * HG_DK, (head + 1) * HG_DK)

    def emit(r0, head, o):
        rows, cols = pl.ds(r0, C), head_cols(head)
        if finalize:
            tot = o + of_ref[rows, cols]
            ms = jnp.mean(tot * tot, axis=-1, keepdims=True)
            res = tot * lax.rsqrt(ms + RMS_EPS) * gain_ref[...] * g_ref[rows, cols].astype(F32)
            o_ref[rows, cols] = res.astype(o_ref.dtype)
        else:
            o_ref[rows, cols] = o

    def pairwise_chunk(step, carry, *, head):
        cidx = (n_chunk - 1 - step) if reverse else step
        r0 = pl.multiple_of(cidx * C, C)
        rows, cols = pl.ds(r0, C), head_cols(head)
        lf = lf_ref[rows, cols]
        q = q_ref[rows, cols].astype(F32)
        v = v_ref[rows, cols]
        cum, loc = cumulative(lf)
        kk = 1.0 - jnp.exp(lf)
        total = cum[edge:edge + 1, :]
        st = st_ref[head]
        o_inter = _dot_nt((q * jnp.exp(cum)).astype(BF16), st.astype(BF16))
        k_state = (kk * jnp.exp(total - cum)).astype(BF16)
        st_ref[head] = st * jnp.exp(total) + _dot_tn(v, k_state)
        qh_b = (q * jnp.exp(loc)).astype(BF16)
        blocks = []
        for s in range(n_sub):
            lo = s * HG_SUB
            hi = lo + HG_SUB
            has_off = (s < n_sub - 1) if reverse else (s > 0)
            if has_off:
                if reverse:
                    expo = jnp.where(row_id >= hi, cum[hi:hi + 1, :] - cum, NEG_INF)
                else:
                    expo = jnp.where(row_id < lo, cum[lo - 1:lo, :] - cum, NEG_INF)
                k_off = (kk * jnp.exp(expo)).astype(BF16)
                a_s = _dot_nt(qh_b[lo:hi], k_off)
            else:
                a_s = jnp.zeros((HG_SUB, C), F32)
            loc_s = loc[lo:hi]
            q_s = q[lo:hi]
            for j in range(HG_SUB):
                jj = lo + j
                d = loc_s - loc[jj:jj + 1, :]
                msk = (sub_row <= j) if reverse else (sub_row >= j)
                e = jnp.exp(jnp.where(msk, d, NEG_INF))
                colv = jnp.sum(q_s * e * kk[jj:jj + 1, :], axis=-1, keepdims=True)
                a_s = jnp.where(lane_c == jj, colv, a_s)
            blocks.append(a_s)
        a = jnp.concatenate(blocks, axis=0)
        emit(r0, head, o_inter + _dot(a.astype(BF16), v))
        return carry

    def staged_tile():
        items = [(((n_chunk - 1 - s) if reverse else s) * C, head) for s in range(n_chunk) for head in range(n_heads)]
        lfs = [lf_ref[r0:r0 + C, head_cols(h)] for r0, h in items]
        qs = [q_ref[r0:r0 + C, head_cols(h)].astype(F32) for r0, h in items]
        vs = [v_ref[r0:r0 + C, head_cols(h)] for r0, h in items]
        cums, locs = zip(*[cumulative(lf) for lf in lfs])
        kks = [1.0 - jnp.exp(lf) for lf in lfs]
        totals = [cum[edge:edge + 1, :] for cum in cums]
        a_mats = []
        for q, kk, cum, loc in zip(qs, kks, cums, locs):
            qh_b = (q * jnp.exp(loc)).astype(BF16)
            blocks = []
            for s in range(n_sub):
                lo = s * HG_SUB
                hi = lo + HG_SUB
                if reverse:
                    edge_row = cum[hi:hi + 1, :] if s < n_sub - 1 else jnp.zeros((1, HG_DK), F32)
                    expo = jnp.where(row_id >= lo, edge_row - cum, NEG_INF)
                    tri = lane_c >= lo + sub_row
                else:
                    edge_row = cum[lo - 1:lo, :] if s > 0 else jnp.zeros((1, HG_DK), F32)
                    expo = jnp.where(row_id < hi, edge_row - cum, NEG_INF)
                    tri = lane_c <= lo + sub_row
                k_rel = (kk * jnp.exp(expo)).astype(BF16)
                blocks.append(jnp.where(tri, _dot_nt(qh_b[lo:hi], k_rel), 0.0))
            a_mats.append(jnp.concatenate(blocks, axis=0).astype(BF16))
        intras = [_dot(a, v) for a, v in zip(a_mats, vs)]
        updates = [_dot_tn(v, (kk * jnp.exp(tot - cum)).astype(BF16))
                   for v, kk, tot, cum in zip(vs, kks, totals, cums)]
        q_ins = [(q * jnp.exp(cum)).astype(BF16) for q, cum in zip(qs, cums)]
        states = [st_ref[h] for h in range(n_heads)]
        for (r0, h), q_in, intra, tot, upd in zip(items, q_ins, intras, totals, updates):
            emit(r0, h, _dot_nt(q_in, states[h].astype(BF16)) + intra)
            states[h] = states[h] * jnp.exp(tot) + upd
        for h in range(n_heads):
            st_ref[h] = states[h]

    mild_decay = jnp.min(lf_ref[...]) >= -HG_SAFE_LOG

    @pl.when(mild_decay)
    def _():
        staged_tile()

    @pl.when(jnp.logical_not(mild_decay))
    def _():
        for head in range(n_heads):
            lax.fori_loop(0, n_chunk, functools.partial(pairwise_chunk, head=head), 0)


def _hgrn_pass(q, v, lf, groups, *, reverse, finalize_args=None):
    T = q.shape[0]
    tb = TOKEN_TILE
    finalize = finalize_args is not None

    def tmap(h, i):
        return (groups.reversed_tile(i, tb) if reverse else i, h)

    spec = pl.BlockSpec((tb, HG_DK * HG_HEADS_PER_STEP), tmap)
    in_specs = [spec, spec, spec]
    args = [q, v, lf]
    if finalize:
        o_fwd, gate, gain = finalize_args
        in_specs += [spec, spec, pl.BlockSpec((1, HG_DK), lambda h, i: (0, 0))]
        args += [o_fwd, gate, gain]
    return pl.pallas_call(
        functools.partial(_hgrn_kernel, reverse=reverse, finalize=finalize, groups=groups),
        out_shape=jax.ShapeDtypeStruct((T, HG_WIDTH), BF16 if finalize else F32),
        grid=(HG_HEADS // HG_HEADS_PER_STEP, T // tb),
        in_specs=in_specs,
        out_specs=spec,
        scratch_shapes=[pltpu.VMEM((HG_HEADS_PER_STEP, HG_DK, HG_DK), F32)],
        compiler_params=_cparams(("parallel", "arbitrary")),
        name="hgrn_bwd" if reverse else "hgrn_fwd",
    )(*args)


def _kvproj_kernel(m_ref, w_ref, o_ref):
    o_ref[...] = _dot(m_ref[...].astype(BF16), w_ref[...]).astype(BF16)


def _kvproj(mem, w):
    R, D = mem.shape
    n = w.shape[1]
    tr = 256
    return pl.pallas_call(
        _kvproj_kernel,
        out_shape=jax.ShapeDtypeStruct((R, n), BF16),
        grid=(R // tr,),
        in_specs=[pl.BlockSpec((tr, D), lambda i: (i, 0)), pl.BlockSpec((D, n), lambda i: (0, 0))],
        out_specs=pl.BlockSpec((tr, n), lambda i: (i, 0)),
        compiler_params=_cparams(("parallel",)),
        name="kvproj",
    )(mem, w)


def _mix_xattn_kernel(x0_ref, x1_ref, att_ref, rec_ref, g0_ref, b0_ref, wo_ref, g1_ref, b1_ref, wq_ref, kv_ref,
                      wxo_ref, g2_ref, b2_ref, wr_ref, br_ref, x2_ref, p_ref, idx_ref, gate_ref, rank_ref, cnt_ref,
                      run_ref, *, tiles_group0):
    D = x0_ref.shape[1]
    dh = D // X_HEADS
    x_tile = _group_rows(x0_ref, x1_ref, tiles_group0)
    x2_parts = []

    def stages(rows):
        mix = _dot(att_ref[rows, :], wo_ref[:ATT_WIDTH, :]) + _dot(rec_ref[rows, :], wo_ref[ATT_WIDTH:, :])
        yield
        xn = _layer_norm(x_tile[rows], g0_ref[...], b0_ref[...])
        x1 = _layer_norm(DEEPNORM_ALPHA * xn + mix, g1_ref[...], b1_ref[...])
        qx = (_dot(x1.astype(BF16), wq_ref[...]) * (dh ** -0.5)).astype(BF16)
        yield
        scores = [_dot_nt(qx[:, h * dh:(h + 1) * dh], kv_ref[0, :, h * dh:(h + 1) * dh]) for h in range(X_HEADS)]
        yield
        heads = []
        for h, s in enumerate(scores):
            s = s - jnp.max(s, axis=-1, keepdims=True)
            p = jnp.exp(s)
            p = p / jnp.sum(p, axis=-1, keepdims=True)
            heads.append(_dot(p.astype(BF16), kv_ref[0, :, D + h * dh:D + (h + 1) * dh]).astype(BF16))
        yield
        xo = _dot(jnp.concatenate(heads, axis=-1), wxo_ref[...])
        yield
        x2 = _layer_norm(DEEPNORM_ALPHA * x1 + xo, g2_ref[...], b2_ref[...])
        x2_ref[rows, :] = x2
        _store_slabs(p_ref, _pack_bf16_pair(x2[:, :D // 2], x2[:, D // 2:]), row0=rows.start)
        x2_parts.append(x2)

    _staggered(stages, x_tile.shape[0])
    _route(jnp.concatenate(x2_parts, axis=0), wr_ref, br_ref, idx_ref, gate_ref, rank_ref, cnt_ref, run_ref)


def _mix_xattn(x0, x1, att, rec, g0, b0, wo, g1, b1, wq, kv, wxo, g2, b2, w_router_t, b_router, seq_of_tile):
    D = x0.shape[1]
    T = x0.shape[0] + x1.shape[0]
    tm = TOKEN_TILE
    tiles_group0 = x0.shape[0] // tm
    n_mem = kv.shape[1]
    row = lambda i: (i, 0)
    const = lambda i: (0, 0)
    vec = pl.BlockSpec((1, D), const)
    tok = pl.BlockSpec((TOP_K, tm), lambda i: (0, i))
    return pl.pallas_call(
        functools.partial(_mix_xattn_kernel, tiles_group0=tiles_group0),
        out_shape=(jax.ShapeDtypeStruct((T, D), F32), jax.ShapeDtypeStruct((T * SLAB, LANES), U32),
                   jax.ShapeDtypeStruct((TOP_K, T), I32), jax.ShapeDtypeStruct((TOP_K, T), F32),
                   jax.ShapeDtypeStruct((TOP_K, T), I32), jax.ShapeDtypeStruct((N_EXPERTS, 1), I32)),
        grid=(T // tm,),
        in_specs=_group_specs(tm, D, tiles_group0) + [
            pl.BlockSpec((tm, ATT_WIDTH), row),
            pl.BlockSpec((tm, HG_WIDTH), row),
            vec, vec,
            pl.BlockSpec((ATT_WIDTH + HG_WIDTH, D), const, pipeline_mode=pl.Buffered(1)),
            vec, vec,
            pl.BlockSpec((D, D), const, pipeline_mode=pl.Buffered(1)),
            pl.BlockSpec((1, n_mem, 2 * D), lambda i: (seq_of_tile(i), 0, 0)),
            pl.BlockSpec((D, D), const, pipeline_mode=pl.Buffered(1)),
            vec, vec,
            pl.BlockSpec((N_EXPERTS, D), const),
            pl.BlockSpec((N_EXPERTS, 1), const),
        ],
        out_specs=(pl.BlockSpec((tm, D), row), pl.BlockSpec((tm * SLAB, LANES), row), tok, tok, tok,
                   pl.BlockSpec((N_EXPERTS, 1), const)),
        scratch_shapes=[pltpu.VMEM((N_EXPERTS, 1), F32)],
        compiler_params=_cparams(("arbitrary",)),
        name="mix_xattn",
    )(x0, x1, att, rec, g0, b0, wo, g1, b1, wq, kv, wxo, g2, b2, w_router_t, b_router)


def _route(x, wt_ref, b_ref, idx_ref, gate_ref, rank_ref, cnt_ref, run_ref):
    tm = x.shape[0]

    @pl.when(pl.program_id(0) == 0)
    def _():
        run_ref[...] = jnp.zeros_like(run_ref)

    xh, xl, _ = _split3(x)
    wh, wl, _ = _split3(wt_ref[...])
    logits = _dot_nt(wh, xh) + _dot_nt(wh, xl) + _dot_nt(wl, xh) + b_ref[...]
    e_id = lax.broadcasted_iota(I32, (N_EXPERTS, tm), 0)
    work = logits
    vals, idxs = [], []
    chosen = jnp.zeros((N_EXPERTS, tm), F32)
    for _ in range(TOP_K):
        m = jnp.max(work, axis=0, keepdims=True)
        idx = jnp.min(jnp.where(work == m, e_id, N_EXPERTS), axis=0, keepdims=True)
        hit = e_id == idx
        chosen = jnp.where(hit, 1.0, chosen)
        work = jnp.where(hit, NEG_INF, work)
        vals.append(m)
        idxs.append(idx)
    ex = [jnp.exp(v - vals[0]) for v in vals]
    den = ex[0] + ex[1] + ex[2] + ex[3]
    t_r = lax.broadcasted_iota(I32, (tm, tm), 0)
    t_c = lax.broadcasted_iota(I32, (tm, tm), 1)
    before = jnp.where(t_r < t_c, 1.0, 0.0).astype(BF16)
    rank_e = _dot(chosen.astype(BF16), before) + run_ref[...]
    run_ref[...] = run_ref[...] + jnp.sum(chosen, axis=1, keepdims=True)
    for k in range(TOP_K):
        idx_ref[k:k + 1, :] = idxs[k]
        gate_ref[k:k + 1, :] = ex[k] / den
        rank_ref[k:k + 1, :] = jnp.sum(jnp.where(e_id == idxs[k], rank_e, 0.0), axis=0, keepdims=True).astype(I32)
    cnt_ref[...] = run_ref[...].astype(I32)


def _dispatch_kernel(fill_from_ref, fill_len_ref, dest_ref, p_ref, out_hbm, sem, fill_sem, *, n_fill):
    td = DISPATCH_TILE

    @pl.when(pl.program_id(0) == 0)
    def _():
        for r in range(fill_from_ref.shape[0]):
            first, n = fill_from_ref[r], fill_len_ref[r]
            n_chunks = lax.shift_right_logical(n, FILL_CHUNK_LOG2)

            def fill_chunk(j, c):
                rows = pl.ds(first + j * FILL_CHUNK, FILL_CHUNK)
                pltpu.make_async_copy(p_ref.at[pl.ds(0, FILL_CHUNK)], out_hbm.at[rows], fill_sem).start()
                return c

            def fill_row(j, c):
                pltpu.make_async_copy(p_ref.at[0], out_hbm.at[first + n_chunks * FILL_CHUNK + j], fill_sem).start()
                return c

            lax.fori_loop(0, n_chunks, fill_chunk, 0)
            lax.fori_loop(0, n - n_chunks * FILL_CHUNK, fill_row, 0)
        filled = out_hbm.at[pl.ds(0, n_fill)]
        pltpu.make_async_copy(filled, filled, fill_sem).wait()

    def start(t, c):
        for k in range(TOP_K):
            d = dest_ref[0, 0, k * td + t]
            pltpu.make_async_copy(p_ref.at[t], out_hbm.at[d], sem).start(priority=k % DMA_PRIORITIES)
        return c

    lax.fori_loop(0, td, start, 0, unroll=DMA_ISSUE_UNROLL)
    everything = out_hbm.at[pl.ds(0, TOP_K * td)]
    pltpu.make_async_copy(everything, everything, sem).wait()


def _dispatch(fill_from, fill_len, dest_tiles, p, n_rows):
    T = p.shape[0]
    td = DISPATCH_TILE
    grid_spec = pltpu.PrefetchScalarGridSpec(
        num_scalar_prefetch=2,
        grid=(T // td,),
        in_specs=[pl.BlockSpec((1, 1, TOP_K * td), lambda i, ff, fl: (i, 0, 0), memory_space=pltpu.SMEM),
                  pl.BlockSpec((td,) + p.shape[1:], lambda i, ff, fl: (i, 0, 0))],
        out_specs=pl.BlockSpec(memory_space=pl.ANY),
        scratch_shapes=[pltpu.SemaphoreType.DMA, pltpu.SemaphoreType.DMA],
    )
    return pl.pallas_call(
        functools.partial(_dispatch_kernel, n_fill=n_rows - T * TOP_K),
        out_shape=jax.ShapeDtypeStruct((n_rows,) + p.shape[1:], p.dtype),
        grid_spec=grid_spec,
        compiler_params=_cparams(("arbitrary",)),
        name="dispatch",
    )(fill_from, fill_len, dest_tiles, p)


def _expert_kernel(blk_e_ref, nvalid_ref, p_ref, wu_ref, bu_ref, wd_ref, bd_ref, y_ref, wu_b, wd_b):
    i = pl.program_id(0)
    dff = wd_ref.shape[1]
    half = SLAB * LANES
    live = i < nvalid_ref[0]
    new_expert = (i == 0) | (blk_e_ref[i] != blk_e_ref[jnp.maximum(i - 1, 0)])

    @pl.when(live & new_expert)
    def _():
        wu_b[...] = wu_ref[0].astype(BF16)
        wd_b[...] = wd_ref[0].astype(BF16)

    @pl.when(live)
    def _():
        lo, hi = _unpack_bf16_pair(_load_slabs(p_ref))
        lo = lo.astype(BF16)
        hi = hi.astype(BF16)

        def up(c0):
            return (_dot(lo, wu_b[:half, c0:c0 + dff]) + _dot(hi, wu_b[half:, c0:c0 + dff])
                    + bu_ref[0, :, c0:c0 + dff])

        glu = jnp.minimum(up(0), SWIGLU_LIMIT)
        lin = jnp.clip(up(dff), -SWIGLU_LIMIT, SWIGLU_LIMIT)
        a = glu * _sigmoid(SWIGLU_ALPHA * glu) * (lin + 1.0)
        y = _dot(a.astype(BF16), wd_b[...]) + bd_ref[0]
        _store_slabs(y_ref, _pack_bf16_pair(y[:, :half], y[:, half:]))

    @pl.when(i >= nvalid_ref[0])
    def _():
        y_ref[...] = jnp.zeros_like(y_ref)


def _experts(blk_expert, n_valid, ps, wu, bu, wd, bd):
    R = ps.shape[0] // SLAB
    te = EXPERT_TILE
    E, D, two_f = wu.shape
    dff = wd.shape[1]
    grid_spec = pltpu.PrefetchScalarGridSpec(
        num_scalar_prefetch=2,
        grid=(R // te,),
        in_specs=[
            pl.BlockSpec((te * SLAB, LANES), lambda i, be, nv: (i, 0)),
            pl.BlockSpec((1, D, two_f), lambda i, be, nv: (be[i], 0, 0)),
            pl.BlockSpec((1, 1, two_f), lambda i, be, nv: (be[i], 0, 0)),
            pl.BlockSpec((1, dff, D), lambda i, be, nv: (be[i], 0, 0)),
            pl.BlockSpec((1, 1, D), lambda i, be, nv: (be[i], 0, 0)),
        ],
        out_specs=pl.BlockSpec((te * SLAB, LANES), lambda i, be, nv: (i, 0)),
        scratch_shapes=[pltpu.VMEM((D, two_f), BF16), pltpu.VMEM((dff, D), BF16)],
    )
    return pl.pallas_call(
        _expert_kernel,
        out_shape=jax.ShapeDtypeStruct((R * SLAB, LANES), U32),
        grid_spec=grid_spec,
        compiler_params=_cparams(("arbitrary",)),
        name="experts",
    )(blk_expert, n_valid, ps, wu, bu, wd, bd)


def _combine_kernel(dest_ref, dest_next_ref, ys_hbm, x2_ref, gate_ref, g_ref, b_ref, o_ref, buf, sems):
    tc = x2_ref.shape[0]
    half = SLAB * LANES
    i = pl.program_id(0)
    slot = i % 2

    def start_tile(idx_ref, s):
        def body(t, c):
            rows = pl.ds(pl.multiple_of(t * SLAB, SLAB), SLAB)
            for k in range(TOP_K):
                d = idx_ref[0, 0, k * tc + t]
                pltpu.make_async_copy(ys_hbm.at[d], buf.at[s, k, rows], sems.at[s]).start(
                    priority=k % DMA_PRIORITIES)
            return c
        lax.fori_loop(0, tc, body, 0, unroll=DMA_ISSUE_UNROLL)

    @pl.when(i == 0)
    def _():
        start_tile(dest_ref, 0)

    for s in range(2):
        @pl.when((i + 1 < pl.num_programs(0)) & (slot == s))
        def _():
            start_tile(dest_next_ref, 1 - s)

    for s in range(2):
        @pl.when(slot == s)
        def _():
            pltpu.make_async_copy(buf.at[s], buf.at[s], sems.at[s]).wait()

    gates = gate_ref[...]
    acc_lo = jnp.zeros((tc, half), F32)
    acc_hi = jnp.zeros((tc, half), F32)
    for k in range(TOP_K):
        lo, hi = _unpack_bf16_pair(_load_slabs(buf.at[slot, k]))
        gk = gates[:, k:k + 1]
        acc_lo = acc_lo + gk * lo
        acc_hi = acc_hi + gk * hi
    ffn = jnp.concatenate([acc_lo, acc_hi], axis=-1)
    o_ref[...] = _layer_norm(DEEPNORM_ALPHA * x2_ref[...] + ffn, g_ref[...], b_ref[...])


def _combine(dest_tiles, ys, x2, gates, g, b, *, tile0, n_tiles):
    D = x2.shape[1]
    tc = COMBINE_TILE
    return pl.pallas_call(
        _combine_kernel,
        out_shape=jax.ShapeDtypeStruct((n_tiles * tc, D), F32),
        grid=(n_tiles,),
        in_specs=[pl.BlockSpec((1, 1, TOP_K * tc), lambda i: (tile0 + i, 0, 0), memory_space=pltpu.SMEM),
                  pl.BlockSpec((1, 1, TOP_K * tc), lambda i: (tile0 + jnp.minimum(i + 1, n_tiles - 1), 0, 0),
                               memory_space=pltpu.SMEM),
                  pl.BlockSpec(memory_space=pl.ANY),
                  pl.BlockSpec((tc, D), lambda i: (tile0 + i, 0)),
                  pl.BlockSpec((tc, TOP_K), lambda i: (tile0 + i, 0)),
                  pl.BlockSpec((1, D), lambda i: (0, 0)),
                  pl.BlockSpec((1, D), lambda i: (0, 0))],
        out_specs=pl.BlockSpec((tc, D), lambda i: (i, 0)),
        scratch_shapes=[pltpu.VMEM((2, TOP_K, tc * SLAB, LANES), U32), pltpu.SemaphoreType.DMA((2,))],
        compiler_params=_cparams(("arbitrary",)),
        name="combine",
    )(dest_tiles, dest_tiles, ys, x2, gates, g, b)


def _rope_tables(seq_len):
    rows = seq_len // GRID_W
    row = jnp.repeat(jnp.arange(rows, dtype=F32), GRID_W)
    colp = (jnp.arange(seq_len) % GRID_W).astype(F32)
    axis_dim = ATT_HEAD_DIM // 2
    inv_freq = ROPE_THETA ** (-jnp.arange(0, axis_dim, 2, dtype=F32) / axis_dim)
    ang = jnp.concatenate([row[:, None] * inv_freq, colp[:, None] * inv_freq], axis=-1)
    cos, sin = jnp.cos(ang), jnp.sin(ang)
    cos_h = jnp.concatenate([cos, cos], axis=-1)
    sin_h = jnp.concatenate([-sin, sin], axis=-1)
    return jnp.tile(cos_h, (1, LANES // ATT_HEAD_DIM)), jnp.tile(sin_h, (1, LANES // ATT_HEAD_DIM))


def _tiles_of(idx_t, tile):
    K, T = idx_t.shape
    return idx_t.reshape(K, T // tile, tile).transpose(1, 0, 2).reshape(T // tile, 1, K * tile)


def kernel(x_prompt, x_sample, mem_prompt, mem_sample, ln_in_g, ln_in_b, w_in, q_norm_g, k_norm_g, hg_lb_fwd, hg_lb_bwd, hg_norm_g, w_out, ln1_g, ln1_b, w_xq, w_xkv, w_xo, ln2_g, ln2_b, w_router, b_router, w_up, b_up, w_down, b_down, ln3_g, ln3_b):
    Bp, Sp, D = x_prompt.shape
    Bs, Ss, _ = x_sample.shape
    Tp, Ts = Bp * Sp, Bs * Ss
    T = Tp + Ts
    tm = TOKEN_TILE
    assert Sp % tm == 0 and Ss % Sp == 0
    n_mem = mem_prompt.shape[1]
    layer = 0

    x0, x1 = x_prompt.reshape(Tp, D), x_sample.reshape(Ts, D)
    mem = jnp.concatenate([mem_prompt.reshape(Bp * n_mem, D), mem_sample.reshape(Bs * n_mem, D)], axis=0)

    def deinterleave(a):
        lead = a.shape[:-1]
        a = a.reshape(*lead, -1, ATT_HEAD_DIM // 2, 2)
        return jnp.swapaxes(a, -1, -2).reshape(*lead, -1)

    n_qk = ATT_WIDTH + ATT_KV_WIDTH
    w_in_l = w_in[layer]
    w_in_b = jnp.concatenate([deinterleave(w_in_l[:, :n_qk]), w_in_l[:, n_qk:]], axis=1).astype(BF16)
    pair = LANES // ATT_HEAD_DIM
    qg = jnp.tile(deinterleave(q_norm_g[layer]), pair).reshape(1, LANES)
    kg = jnp.tile(deinterleave(k_norm_g[layer]), pair).reshape(1, LANES)

    def lower_bound(logits):
        sm = jax.nn.softmax(logits.astype(F32), axis=0)
        return (jnp.cumsum(sm, axis=0)[layer + 1] - sm[0]).reshape(1, HG_WIDTH)

    lbf, lbb = lower_bound(hg_lb_fwd), lower_bound(hg_lb_bwd)
    cos_t, sin_t = _rope_tables(max(Sp, Ss))
    vec = lambda a: a.reshape(1, -1)

    groups = _Groups(Bp, Sp, Bs, Ss)
    pos_block = lambda i: groups.pos_of_tile(i, tm)
    seq_of_tile = lambda i: groups.seq_of_tile(i, tm)

    q, k, v, hq, hi, hg, lff, lfb = _inproj(x0, x1, vec(ln_in_g), vec(ln_in_b), w_in_b, cos_t, sin_t, qg, kg,
                                           lbf, lbb, pos_block)

    score_bound = (jnp.max(jnp.abs(q_norm_g[layer])) * jnp.max(jnp.abs(k_norm_g[layer]))
                   * (ATT_HEAD_DIM ** 0.5 * ATT_NORM_SLACK)).astype(F32).reshape(1)
    att = _attention(score_bound, q, k, v, groups)
    o_fwd = _hgrn_pass(hq, hi, lff, groups, reverse=False)
    gain = hg_norm_g[layer].reshape(1, HG_DK)
    rec = _hgrn_pass(hq, hi, lfb, groups, reverse=True, finalize_args=(o_fwd, hg, gain))

    kv = _kvproj(mem, w_xkv[layer].astype(BF16)).reshape(Bp + Bs, n_mem, 2 * D)
    x2, packed, idx_t, gate_t, rank_t, counts = _mix_xattn(
        x0, x1, att, rec, vec(ln_in_g), vec(ln_in_b), w_out[layer].astype(BF16), vec(ln1_g[layer]),
        vec(ln1_b[layer]), w_xq[layer].astype(BF16), kv, w_xo[layer].astype(BF16), vec(ln2_g[layer]),
        vec(ln2_b[layer]), w_router[layer].T, b_router[layer].reshape(N_EXPERTS, 1), seq_of_tile)

    te = EXPERT_TILE
    counts = counts.reshape(N_EXPERTS)
    padded = (counts + te - 1) // te * te
    pad_end = jnp.cumsum(padded)
    pad_start = pad_end - padded
    n_blk = (T * TOP_K + N_EXPERTS * (te - 1) + te - 1) // te
    e_ids = jnp.arange(N_EXPERTS, dtype=I32)
    start_of = jnp.sum(jnp.where(idx_t[None] == e_ids[:, None, None], pad_start[:, None, None], 0), axis=0)
    dest_t = start_of + rank_t
    blk_row0 = jnp.arange(n_blk, dtype=I32) * te
    blk_expert = jnp.minimum(jnp.sum((pad_end[None, :] <= blk_row0[:, None]).astype(I32), axis=1), N_EXPERTS - 1)
    n_valid = (pad_end[-1] // te).astype(I32).reshape(1)

    n_rows = n_blk * te
    fill_from = jnp.concatenate([pad_start + counts, pad_end[-1:]]).astype(I32)
    fill_len = jnp.concatenate([padded - counts, n_rows - pad_end[-1:]]).astype(I32)
    sorted_rows = _dispatch(fill_from, fill_len, _tiles_of(dest_t, DISPATCH_TILE),
                            packed.reshape(T, SLAB, LANES), n_rows)
    ys = _experts(blk_expert, n_valid, sorted_rows.reshape(n_rows * SLAB, LANES), w_up[layer],
                  b_up[layer].reshape(N_EXPERTS, 1, -1), w_down[layer], b_down[layer].reshape(N_EXPERTS, 1, -1))
    ys = ys.reshape(n_rows, SLAB, LANES)

    dest_c = _tiles_of(dest_t, COMBINE_TILE)
    gates = gate_t.T
    tc = COMBINE_TILE
    y_p = _combine(dest_c, ys, x2, gates, vec(ln3_g[layer]), vec(ln3_b[layer]), tile0=0, n_tiles=Tp // tc)
    y_s = _combine(dest_c, ys, x2, gates, vec(ln3_g[layer]), vec(ln3_b[layer]), tile0=Tp // tc, n_tiles=Ts // tc)
    return y_p.reshape(Bp, Sp, D), y_s.reshape(Bs, Ss, D)
```
